```python
import math
import jax, jax.numpy as jnp
from jax import lax
import numpy as np

D_MODEL = 2048
BATCH = 2
SEQ = 8192
DEPTH = 1

HEAD_DIM = 128
HEADS_PER_GROUP = 4
ATTN_GROUPS = ((128, 1), (512, 4), (2048, 16))
N_ATTN_HEADS = HEADS_PER_GROUP * len(ATTN_GROUPS)
ATTN_OUT_WIDTH = HEADS_PER_GROUP * HEAD_DIM
HQ = N_ATTN_HEADS * HEAD_DIM
SSM_WIDTH = D_MODEL // 2
SSM_GROUP = 16
SSM_GROUPS = SSM_WIDTH // SSM_GROUP
SSM_STATE = 64
D_FF = -(-8 * D_MODEL // (3 * 256)) * 256
IN_SIZES = (HQ, HQ, HQ, SSM_WIDTH, D_MODEL, D_MODEL)
IN_WIDTH = sum(IN_SIZES)
IN_SPLITS = tuple(int(i) for i in np.cumsum(IN_SIZES)[:-1])
EPS = 1e-6
DT_MIN = 1e-3
DT_MAX = 1e-1

kernel_name = 'hybrid_dilated_attn_s5_gated'


def rms_norm(x, gain):
    xf = x.astype(jnp.float32)
    y = xf * lax.rsqrt(jnp.mean(xf * xf, axis=-1, keepdims=True) + EPS)
    return (y * gain.astype(jnp.float32)).astype(x.dtype)


def alibi_slopes(n):
    return jnp.exp2(-8.0 * jnp.arange(1, n + 1, dtype=jnp.float32) / n)


def dilated_window_attention(q, k, v, slopes, window, dilation):
    b, s, h, e = q.shape
    blk = window // dilation
    sub_len = -(-s // dilation)
    n_blk = -(-sub_len // blk)
    s_pad = n_blk * blk * dilation

    def to_blocks(t):
        t = jnp.pad(t, ((0, 0), (0, s_pad - s), (0, 0), (0, 0)))
        t = t.reshape(b, n_blk * blk, dilation, h, e).transpose(0, 2, 1, 3, 4)
        return t.reshape(b, dilation, n_blk, blk, h, e)

    def with_prev(t):
        prev = jnp.pad(t, ((0, 0), (0, 0), (1, 0), (0, 0), (0, 0), (0, 0)))[:, :, :-1]
        return jnp.concatenate([prev, t], axis=3)

    def from_blocks(t):
        f = t.shape[-1]
        t = t.reshape(b, dilation, n_blk * blk, h, f).transpose(0, 2, 1, 3, 4)
        return t.reshape(b, s_pad, h, f)[:, :s]

    qb = to_blocks(q)
    kw = with_prev(to_blocks(k))
    vw = with_prev(to_blocks(v))
    scores = jnp.einsum('brnqhe,brnkhe->brnhqk', qb, kw).astype(jnp.float32) * (e ** -0.5)
    qi = jnp.arange(blk)[:, None]
    ki = jnp.arange(2 * blk)[None, :]
    dist = blk + qi - ki
    blk_idx = jnp.arange(n_blk)[:, None, None]
    valid = ((dist >= 0) & (dist <= blk))[None] & (blk_idx * blk - blk + ki[None] >= 0)
    bias = -slopes[:, None, None] * (dist * dilation).astype(jnp.float32)[None]
    scores = scores + bias[None, None, None]
    scores = jnp.where(valid[None, None, :, None], scores, -jnp.inf)
    lse = jax.nn.logsumexp(scores, axis=-1)
    probs = jnp.exp(scores - lse[..., None])
    out = jnp.einsum('brnhqk,brnkhe->brnqhe', probs.astype(v.dtype), vw)
    lse = from_blocks(lse.transpose(0, 1, 2, 4, 3)[..., None])[..., 0]
    return from_blocks(out), lse


def s5_ssm(u, a_re, a_im, log_dt, b_re, b_im, c_re, c_im, d_skip):
    bsz, s, _ = u.shape
    uf = u.astype(jnp.float32)
    ug = uf.reshape(bsz, s, SSM_GROUPS, SSM_GROUP)
    lam = lax.complex(a_re.astype(jnp.float32), a_im.astype(jnp.float32))
    dt = jnp.exp(log_dt.astype(jnp.float32))[:, None]
    lam_bar = jnp.exp(lam * dt)
    b_cplx = lax.complex(b_re.astype(jnp.float32), b_im.astype(jnp.float32))
    b_bar = ((lam_bar - 1.0) / lam)[..., None] * b_cplx
    bu = lax.complex(jnp.einsum('bsgc,gpc->sbgp', ug, jnp.real(b_bar)),
                     jnp.einsum('bsgc,gpc->sbgp', ug, jnp.imag(b_bar)))
    a_seq = jnp.broadcast_to(lam_bar[None, None], (s, 1) + lam_bar.shape)

    def combine(left, right):
        a_l, b_l = left
        a_r, b_r = right
        return a_r * a_l, a_r * b_l + b_r

    _, states = lax.associative_scan(combine, (a_seq, bu), axis=0)
    y = (jnp.einsum('sbgp,gcp->bsgc', jnp.real(states), c_re.astype(jnp.float32))
         - jnp.einsum('sbgp,gcp->bsgc', jnp.imag(states), c_im.astype(jnp.float32)))
    y = y.reshape(bsz, s, SSM_WIDTH) + d_skip.astype(jnp.float32) * uf
    return y.astype(u.dtype)


def hybrid_layer(x, norm_mix_pre, w_in, w_attn_up, ssm_a_re, ssm_a_im, ssm_log_dt, ssm_b_re, ssm_b_im,
                 ssm_c_re, ssm_c_im, ssm_d, w_glu_v, w_glu_g, w_out, norm_mix_post, norm_ffn_pre,
                 w_ffn_gate, w_ffn_up, w_ffn_down, norm_ffn_post):
    bsz, s, _ = x.shape
    h = rms_norm(x, norm_mix_pre)
    q, k, v, u, gate_a, gate_s = jnp.split(h @ w_in, IN_SPLITS, axis=-1)
    q = q.reshape(bsz, s, N_ATTN_HEADS, HEAD_DIM)
    k = k.reshape(bsz, s, N_ATTN_HEADS, HEAD_DIM)
    v = v.reshape(bsz, s, N_ATTN_HEADS, HEAD_DIM)
    slopes = alibi_slopes(N_ATTN_HEADS)
    outs, lses = [], []
    for g, (window, dilation) in enumerate(ATTN_GROUPS):
        hs = slice(g * HEADS_PER_GROUP, (g + 1) * HEADS_PER_GROUP)
        o, l = dilated_window_attention(q[:, :, hs], k[:, :, hs], v[:, :, hs], slopes[hs], window, dilation)
        outs.append(o)
        lses.append(l)
    mix_w = jax.nn.softmax(jnp.stack(lses), axis=0)
    attn = jnp.sum(mix_w[..., None] * jnp.stack(outs).astype(jnp.float32), axis=0)
    attn_branch = attn.reshape(bsz, s, ATTN_OUT_WIDTH).astype(x.dtype) @ w_attn_up
    y = jax.nn.gelu(s5_ssm(u, ssm_a_re, ssm_a_im, ssm_log_dt, ssm_b_re, ssm_b_im, ssm_c_re, ssm_c_im, ssm_d))
    ssm_branch = (y @ w_glu_v) * jax.nn.sigmoid(y @ w_glu_g)
    merged = jax.nn.sigmoid(gate_a) * attn_branch + jax.nn.sigmoid(gate_s) * ssm_branch
    x = x + rms_norm(merged @ w_out, norm_mix_post)
    h = rms_norm(x, norm_ffn_pre)
    f = (jax.nn.silu(h @ w_ffn_gate) * (h @ w_ffn_up)) @ w_ffn_down
    return x + rms_norm(f, norm_ffn_post)


def setup_inputs(seed: int = 0) -> dict:
    key = jax.random.key(seed)
    ks = jax.random.split(key, 24)

    def nrm(k, shape, scale):
        return jax.random.normal(k, shape, jnp.float32) * scale

    def gain(k, n):
        return 1.0 + 0.02 * jax.random.normal(k, (DEPTH, n), jnp.float32)

    g, p, c = SSM_GROUPS, SSM_STATE, SSM_GROUP
    return {
        'x': nrm(ks[0], (BATCH, SEQ, D_MODEL), 1.0),
        'norm_mix_pre': gain(ks[1], D_MODEL),
        'w_in': nrm(ks[2], (DEPTH, D_MODEL, IN_WIDTH), D_MODEL ** -0.5),
        'w_attn_up': nrm(ks[3], (DEPTH, ATTN_OUT_WIDTH, D_MODEL), ATTN_OUT_WIDTH ** -0.5),
        'ssm_a_re': -0.5 + 0.01 * jax.random.normal(ks[4], (DEPTH, g, p), jnp.float32),
        'ssm_a_im': jnp.pi * jnp.arange(p, dtype=jnp.float32)[None, None, :] + 0.01 * jax.random.normal(ks[5], (DEPTH, g, p), jnp.float32),
        'ssm_log_dt': jax.random.uniform(ks[6], (DEPTH, g), jnp.float32, math.log(DT_MIN), math.log(DT_MAX)),
        'ssm_b_re': nrm(ks[7], (DEPTH, g, p, c), (2 * c) ** -0.5),
        'ssm_b_im': nrm(ks[8], (DEPTH, g, p, c), (2 * c) ** -0.5),
        'ssm_c_re': nrm(ks[9], (DEPTH, g, c, p), (2 * p) ** -0.5 * 4.0),
        'ssm_c_im': nrm(ks[10], (DEPTH, g, c, p), (2 * p) ** -0.5 * 4.0),
        'ssm_d': nrm(ks[11], (DEPTH, SSM_WIDTH), 1.0),
        'w_glu_v': nrm(ks[12], (DEPTH, SSM_WIDTH, D_MODEL), SSM_WIDTH ** -0.5),
        'w_glu_g': nrm(ks[13], (DEPTH, SSM_WIDTH, D_MODEL), SSM_WIDTH ** -0.5),
        'w_out': nrm(ks[14], (DEPTH, D_MODEL, D_MODEL), D_MODEL ** -0.5),
        'norm_mix_post': gain(ks[15], D_MODEL),
        'norm_ffn_pre': gain(ks[16], D_MODEL),
        'w_ffn_gate': nrm(ks[17], (DEPTH, D_MODEL, D_FF), D_MODEL ** -0.5),
        'w_ffn_up': nrm(ks[18], (DEPTH, D_MODEL, D_FF), D_MODEL ** -0.5),
        'w_ffn_down': nrm(ks[19], (DEPTH, D_FF, D_MODEL), D_FF ** -0.5),
        'norm_ffn_post': gain(ks[20], D_MODEL),
    }


def reference(x, norm_mix_pre, w_in, w_attn_up, ssm_a_re, ssm_a_im, ssm_log_dt, ssm_b_re, ssm_b_im,
              ssm_c_re, ssm_c_im, ssm_d, w_glu_v, w_glu_g, w_out, norm_mix_post, norm_ffn_pre,
              w_ffn_gate, w_ffn_up, w_ffn_down, norm_ffn_post):
    for i in range(DEPTH):
        x = hybrid_layer(x, norm_mix_pre[i], w_in[i], w_attn_up[i], ssm_a_re[i], ssm_a_im[i], ssm_log_dt[i],
                         ssm_b_re[i], ssm_b_im[i], ssm_c_re[i], ssm_c_im[i], ssm_d[i], w_glu_v[i], w_glu_g[i],
                         w_out[i], norm_mix_post[i], norm_ffn_pre[i], w_ffn_gate[i], w_ffn_up[i],
                         w_ffn_down[i], norm_ffn_post[i])
    return x
```

```python
import functools
import math

import jax
import jax.numpy as jnp
import numpy as np
from jax import lax
from jax.experimental import pallas as pl
from jax.experimental.pallas import tpu as pltpu

F32 = jnp.float32
BF16 = jnp.bfloat16

EPS = 1e-6
HEAD_DIM = 128
HEADS_PER_GROUP = 4
ATTN_GROUPS = ((128, 1), (512, 4), (2048, 16))
N_GROUPS = len(ATTN_GROUPS)
N_HEADS = HEADS_PER_GROUP * N_GROUPS
GROUP_WIDTH = HEADS_PER_GROUP * HEAD_DIM
ATTN_BLK = 128
ATTN_TILE = 2048
SSM_GROUP = 16
SSM_STATE = 64
SSM_CHUNK = 8
LANES = 128
GROUPS_PER_LANE_BLOCK = LANES // SSM_GROUP
NEG = -1e30
VMEM_LIMIT = 56 * 1024 * 1024


def _params(*sem):
    return pltpu.CompilerParams(dimension_semantics=sem, vmem_limit_bytes=VMEM_LIMIT)


def _rms(x, gain):
    return x * lax.rsqrt(jnp.mean(x * x, axis=-1, keepdims=True) + EPS) * gain


def _inproj_kernel(x_ref, g_ref, w_ref, o_ref, h_ref):
    @pl.when(pl.program_id(1) == 0)
    def _():
        h_ref[...] = _rms(x_ref[...], g_ref[...]).astype(BF16)

    o_ref[...] = jnp.dot(h_ref[...], w_ref[...], preferred_element_type=F32).astype(o_ref.dtype)


def _inproj(x, gain, w, *, tm, tn):
    t, d = x.shape
    n = w.shape[1]
    return pl.pallas_call(
        _inproj_kernel,
        grid=(t // tm, n // tn),
        in_specs=[
            pl.BlockSpec((tm, d), lambda i, j: (i, 0)),
            pl.BlockSpec((1, d), lambda i, j: (0, 0)),
            pl.BlockSpec((d, tn), lambda i, j: (0, j)),
        ],
        out_specs=pl.BlockSpec((tm, tn), lambda i, j: (i, j)),
        out_shape=jax.ShapeDtypeStruct((t, n), BF16),
        scratch_shapes=[pltpu.VMEM((tm, d), BF16)],
        compiler_params=_params("parallel", "arbitrary"),
        name="inproj",
    )(x, gain.reshape(1, d), w)


def _alibi_slope(head):
    return 2.0 ** (-8.0 * (head + 1) / N_HEADS)


def _attn_kernel(*refs):
    in_refs = refs[: 5 * N_GROUPS]
    o_ref = refs[5 * N_GROUPS]
    out_scr, lse_scr = refs[5 * N_GROUPS + 1:]
    tile = pl.program_id(1)
    u = pl.program_id(2)
    units = ATTN_TILE // ATTN_BLK
    row = lax.broadcasted_iota(jnp.int32, (ATTN_BLK, ATTN_BLK), 0)
    col = lax.broadcasted_iota(jnp.int32, (ATTN_BLK, ATTN_BLK), 1)
    diff = (row - col).astype(F32)
    cur_ok = col <= row
    prev_ok = col >= row
    scale = HEAD_DIM ** -0.5
    nt = (((1,), (1,)), ((), ()))

    for g, (_, dil) in enumerate(ATTN_GROUPS):
        q_ref, kc_ref, kp_ref, vc_ref, vp_ref = in_refs[5 * g: 5 * g + 5]
        nb = u // dil
        r = u % dil
        no_prev = jnp.where(jnp.logical_and(tile == 0, nb == 0), NEG, 0.0).astype(F32)
        start = nb * (ATTN_BLK * dil) + r
        rows = pl.ds(start, ATTN_BLK) if dil == 1 else pl.ds(start, ATTN_BLK, stride=dil)
        for h in range(HEADS_PER_GROUP):
            cs = slice(h * HEAD_DIM, (h + 1) * HEAD_DIM)
            slope = _alibi_slope(g * HEADS_PER_GROUP + h) * dil
            q = q_ref[:, cs]
            sc = lax.dot_general(q, kc_ref[:, cs], nt, preferred_element_type=F32) * scale - slope * diff
            sp = lax.dot_general(q, kp_ref[:, cs], nt, preferred_element_type=F32) * scale - slope * (diff + ATTN_BLK)
            sc = jnp.where(cur_ok, sc, NEG)
            sp = jnp.where(prev_ok, sp, NEG) + no_prev
            m = jnp.maximum(jnp.max(sc, axis=1, keepdims=True), jnp.max(sp, axis=1, keepdims=True))
            pc = jnp.exp(sc - m)
            pp = jnp.exp(sp - m)
            l = jnp.sum(pc, axis=1, keepdims=True) + jnp.sum(pp, axis=1, keepdims=True)
            o = jnp.dot(pc.astype(BF16), vc_ref[:, cs], preferred_element_type=F32)
            o = o + jnp.dot(pp.astype(BF16), vp_ref[:, cs], preferred_element_type=F32)
            out_scr[g, h, rows, :] = o / l
            lse_scr[g, h, rows, :] = jnp.broadcast_to(m + jnp.log(l), (ATTN_BLK, HEAD_DIM))

    @pl.when(u == units - 1)
    def _():
        for h in range(HEADS_PER_GROUP):
            lses = [lse_scr[g, h] for g in range(N_GROUPS)]
            top = functools.reduce(jnp.maximum, lses)
            ws = [jnp.exp(x - top) for x in lses]
            num = sum(w * out_scr[g, h] for g, w in enumerate(ws))
            o_ref[:, h * HEAD_DIM:(h + 1) * HEAD_DIM] = (num / sum(ws)).astype(o_ref.dtype)


def _attention(proj, bsz, seq):
    t, width = proj.shape
    assert seq % ATTN_TILE == 0 and width % GROUP_WIDTH == 0
    tiles = seq // ATTN_TILE
    units = ATTN_TILE // ATTN_BLK
    col_blocks = width // GROUP_WIDTH
    operands, in_specs = [], []
    for g, (window, dil) in enumerate(ATTN_GROUPS):
        assert window // dil == ATTN_BLK and units % dil == 0
        view = proj.reshape(t // dil, dil * width)
        blocks_per_seq = seq // (ATTN_BLK * dil)
        blocks_per_tile = units // dil

        def index(b, tile, u, *, part, prev, dil=dil, g=g, bps=blocks_per_seq, bpt=blocks_per_tile):
            blk = tile * bpt + u // dil
            if prev:
                blk = jnp.maximum(blk - 1, 0)
            return b * bps + blk, (u % dil) * col_blocks + part * N_GROUPS + g

        for part, prev in ((0, False), (1, False), (1, True), (2, False), (2, True)):
            operands.append(view)
            in_specs.append(pl.BlockSpec((ATTN_BLK, GROUP_WIDTH), functools.partial(index, part=part, prev=prev)))
    scratch = pltpu.VMEM((N_GROUPS, HEADS_PER_GROUP, ATTN_TILE, HEAD_DIM), F32)
    return pl.pallas_call(
        _attn_kernel,
        grid=(bsz, tiles, units),
        in_specs=in_specs,
        out_specs=pl.BlockSpec((ATTN_TILE, GROUP_WIDTH), lambda b, tile, u: (b * tiles + tile, 0)),
        out_shape=jax.ShapeDtypeStruct((t, GROUP_WIDTH), BF16),
        scratch_shapes=[scratch, scratch],
        compiler_params=_params("parallel", "parallel", "arbitrary"),
        name="dilated_attention",
    )(*operands)


def _ssm_weights(a_re, a_im, log_dt, b_re, b_im, c_re, c_im):
    n_groups = a_re.shape[0]
    nblk = n_groups // GROUPS_PER_LANE_BLOCK
    gl = GROUPS_PER_LANE_BLOCK
    L = SSM_CHUNK
    hi = lax.Precision.HIGHEST
    lam = lax.complex(a_re.astype(F32), a_im.astype(F32))
    dt = jnp.exp(log_dt.astype(F32))[:, None]
    lam_bar = jnp.exp(lam * dt)
    b_bar = ((lam_bar - 1.0) / lam)[..., None] * lax.complex(b_re.astype(F32), b_im.astype(F32))
    c_cplx = lax.complex(c_re.astype(F32), c_im.astype(F32))
    steps = jnp.arange(L + 1, dtype=F32)
    powers = jnp.exp((lam * dt)[:, None, :] * steps[None, :, None])
    eye = jnp.eye(gl, dtype=F32)

    cb = c_cplx.transpose(0, 2, 1)[:, :, :, None] * b_bar[:, :, None, :]
    pw = powers[:, :L]
    kern = (jnp.einsum('gnp,gpoc->gnco', jnp.real(pw), jnp.real(cb), precision=hi)
            - jnp.einsum('gnp,gpoc->gnco', jnp.imag(pw), jnp.imag(cb), precision=hi))
    lag = np.arange(L)[None, :] - np.arange(L)[:, None]
    toe = jnp.where((lag >= 0)[None, :, :, None, None], kern[:, np.maximum(lag, 0)], 0.0)
    toe = toe.reshape(nblk, gl, L, L, SSM_GROUP, SSM_GROUP)
    w_toe = toe.transpose(0, 2, 1, 4, 3, 5)[:, :, :, :, :, None, :] * eye[None, None, :, None, None, :, None]
    w_toe = w_toe.reshape(nblk, L * LANES, L * LANES)

    qb = powers[:, L - 1::-1][:, :L, :, None] * b_bar[:, None]
    qb = jnp.stack([jnp.real(qb), jnp.imag(qb)], axis=0)
    qb = qb.reshape(2, nblk, gl, L, SSM_STATE, SSM_GROUP)
    w_in = qb.transpose(1, 3, 2, 5, 0, 4)[:, :, :, :, :, None, :] * eye[None, None, :, None, None, :, None]
    w_in = w_in.reshape(nblk, L * LANES, 2 * gl * SSM_STATE)

    e = c_cplx[:, None] * powers[:, 1:, None, :]
    e = jnp.stack([jnp.real(e), -jnp.imag(e)], axis=0)
    e = e.reshape(2, nblk, gl, L, SSM_GROUP, SSM_STATE)
    w_out = e.transpose(1, 0, 2, 5, 3, 4)[:, :, :, :, :, None, :] * eye[None, None, :, None, None, :, None]
    w_out = w_out.reshape(nblk, 2 * gl * SSM_STATE, L * LANES)

    a_chunk = powers[:, L]
    a_chunk = jnp.stack([jnp.real(a_chunk), jnp.imag(a_chunk)], axis=0).reshape(2, nblk, gl * SSM_STATE)
    a_chunk = a_chunk.transpose(1, 0, 2).reshape(nblk, 1, 2 * gl * SSM_STATE)
    return w_toe.astype(BF16), w_in.astype(BF16), w_out.astype(BF16), a_chunk


def _ssm_kernel(*refs, tiles_per_seq):
    u_refs = refs[:SSM_CHUNK]
    wt_ref, wb_ref, wc_ref, a_ref, d_ref, y_ref, s_scr, xp_scr, carry_scr = refs[SSM_CHUNK:]
    tc = s_scr.shape[0]
    half = s_scr.shape[1] // 2

    @pl.when(pl.program_id(1) % tiles_per_seq == 0)
    def _():
        carry_scr[...] = jnp.zeros_like(carry_scr)

    ucat = jnp.concatenate([r[...] for r in u_refs], axis=1)
    s_scr[...] = jnp.dot(ucat, wb_ref[0], preferred_element_type=F32)
    ar = a_ref[0, :, :half]
    ai = a_ref[0, :, half:]

    def step(c, carry):
        xr, xi = carry
        xp_scr[pl.ds(c, 1), :half] = xr
        xp_scr[pl.ds(c, 1), half:] = xi
        s = s_scr[pl.ds(c, 1), :]
        return ar * xr - ai * xi + s[:, :half], ar * xi + ai * xr + s[:, half:]

    xr, xi = lax.fori_loop(0, tc, step, (carry_scr[:, :half], carry_scr[:, half:]))
    carry_scr[:, :half] = xr
    carry_scr[:, half:] = xi

    y = jnp.dot(ucat, wt_ref[0], preferred_element_type=F32)
    y = y + jnp.dot(xp_scr[...].astype(BF16), wc_ref[0], preferred_element_type=F32)
    for i in range(SSM_CHUNK):
        cs = slice(i * LANES, (i + 1) * LANES)
        yi = y[:, cs] + d_ref[0] * u_refs[i][...].astype(F32)
        y_ref[0, :, cs] = jax.nn.gelu(yi).astype(y_ref.dtype)


def _ssm(proj, u_col0, ssm_w, d_skip, bsz, seq, *, tc):
    t, width = proj.shape
    w_toe, w_in, w_out, a_chunk = ssm_w
    nblk = w_toe.shape[0]
    L = SSM_CHUNK
    rows = t // L
    rows_per_seq = seq // L
    assert rows_per_seq % tc == 0 and u_col0 % LANES == 0 and width % LANES == 0
    view = proj.reshape(rows, L * width)
    lane_blocks = width // LANES
    u_specs = [
        pl.BlockSpec((tc, LANES), functools.partial(
            lambda blk, i, j: (i, j * lane_blocks + u_col0 // LANES + blk), j=j))
        for j in range(L)
    ]
    wide = L * LANES
    states = w_in.shape[2]
    y = pl.pallas_call(
        functools.partial(_ssm_kernel, tiles_per_seq=rows_per_seq // tc),
        grid=(nblk, rows // tc),
        in_specs=u_specs + [
            pl.BlockSpec((1, wide, wide), lambda blk, i: (blk, 0, 0)),
            pl.BlockSpec((1, wide, states), lambda blk, i: (blk, 0, 0)),
            pl.BlockSpec((1, states, wide), lambda blk, i: (blk, 0, 0)),
            pl.BlockSpec((1, 1, states), lambda blk, i: (blk, 0, 0)),
            pl.BlockSpec((1, 1, LANES), lambda blk, i: (blk, 0, 0)),
        ],
        out_specs=pl.BlockSpec((1, tc, wide), lambda blk, i: (blk, i, 0)),
        out_shape=jax.ShapeDtypeStruct((nblk, rows, wide), BF16),
        scratch_shapes=[
            pltpu.VMEM((tc, states), F32),
            pltpu.VMEM((tc, states), F32),
            pltpu.VMEM((1, states), F32),
        ],
        compiler_params=_params("parallel", "arbitrary"),
        name="s5_chunked",
    )(*([view] * L), w_toe, w_in, w_out, a_chunk, d_skip.astype(F32).reshape(nblk, 1, LANES))
    return y.reshape(nblk, t, LANES)


def _merge_kernel(attn_ref, y_ref, ga_ref, gs_ref, wup_ref, wv_ref, wg_ref, o_ref, ycat_ref):
    @pl.when(pl.program_id(1) == 0)
    def _():
        ycat_ref[...] = jnp.concatenate([y_ref[b] for b in range(y_ref.shape[0])], axis=1)

    ycat = ycat_ref[...]
    attn_branch = jnp.dot(attn_ref[...], wup_ref[...], preferred_element_type=F32)
    val = jnp.dot(ycat, wv_ref[...], preferred_element_type=F32)
    gate = jnp.dot(ycat, wg_ref[...], preferred_element_type=F32)
    ssm_branch = val * jax.nn.sigmoid(gate)
    merged = (jax.nn.sigmoid(ga_ref[...].astype(F32)) * attn_branch
              + jax.nn.sigmoid(gs_ref[...].astype(F32)) * ssm_branch)
    o_ref[...] = merged.astype(o_ref.dtype)


def _merge(attn, y, proj, ga_col0, gs_col0, w_up, w_v, w_g, *, tm, tn):
    t = attn.shape[0]
    n = w_up.shape[1]
    nblk = y.shape[0]
    assert ga_col0 % tn == 0 and gs_col0 % tn == 0
    return pl.pallas_call(
        _merge_kernel,
        grid=(t // tm, n // tn),
        in_specs=[
            pl.BlockSpec((tm, attn.shape[1]), lambda i, j: (i, 0)),
            pl.BlockSpec((nblk, tm, LANES), lambda i, j: (0, i, 0)),
            pl.BlockSpec((tm, tn), lambda i, j: (i, ga_col0 // tn + j)),
            pl.BlockSpec((tm, tn), lambda i, j: (i, gs_col0 // tn + j)),
            pl.BlockSpec((w_up.shape[0], tn), lambda i, j: (0, j)),
            pl.BlockSpec((w_v.shape[0], tn), lambda i, j: (0, j)),
            pl.BlockSpec((w_g.shape[0], tn), lambda i, j: (0, j)),
        ],
        out_specs=pl.BlockSpec((tm, tn), lambda i, j: (i, j)),
        out_shape=jax.ShapeDtypeStruct((t, n), BF16),
        scratch_shapes=[pltpu.VMEM((tm, nblk * LANES), BF16)],
        compiler_params=_params("parallel", "arbitrary"),
        name="gated_merge",
    )(attn, y, proj, proj, w_up, w_v, w_g)


def _outproj_kernel(m_ref, w_ref, x_ref, g_ref, o_ref):
    z = jnp.dot(m_ref[...], w_ref[...], preferred_element_type=F32)
    o_ref[...] = x_ref[...] + _rms(z, g_ref[...])


def _outproj(merged, w, x, gain, *, tm):
    t, d = x.shape
    return pl.pallas_call(
        _outproj_kernel,
        grid=(t // tm,),
        in_specs=[
            pl.BlockSpec((tm, merged.shape[1]), lambda i: (i, 0)),
            pl.BlockSpec(w.shape, lambda i: (0, 0)),
            pl.BlockSpec((tm, d), lambda i: (i, 0)),
            pl.BlockSpec((1, d), lambda i: (0, 0)),
        ],
        out_specs=pl.BlockSpec((tm, d), lambda i: (i, 0)),
        out_shape=jax.ShapeDtypeStruct((t, d), F32),
        compiler_params=_params("parallel"),
        name="outproj_norm_residual",
    )(merged, w, x, gain.reshape(1, d))


def _ffn_kernel(x_ref, gpre_ref, gpost_ref, wg_ref, wu_ref, wd_ref, o_ref, h_ref, acc_ref):
    k = pl.program_id(1)

    @pl.when(k == 0)
    def _():
        h_ref[...] = _rms(x_ref[...], gpre_ref[...]).astype(BF16)
        acc_ref[...] = jnp.zeros_like(acc_ref)

    h = h_ref[...]
    gate = jnp.dot(h, wg_ref[...], preferred_element_type=F32)
    up = jnp.dot(h, wu_ref[...], preferred_element_type=F32)
    f = (jax.nn.silu(gate) * up).astype(BF16)
    acc_ref[...] += jnp.dot(f, wd_ref[...], preferred_element_type=F32)

    @pl.when(k == pl.num_programs(1) - 1)
    def _():
        o_ref[...] = x_ref[...] + _rms(acc_ref[...], gpost_ref[...])


def _ffn(x, gain_pre, gain_post, w_gate, w_up, w_down, *, tm, tf):
    t, d = x.shape
    dff = w_gate.shape[1]
    return pl.pallas_call(
        _ffn_kernel,
        grid=(t // tm, dff // tf),
        in_specs=[
            pl.BlockSpec((tm, d), lambda i, k: (i, 0)),
            pl.BlockSpec((1, d), lambda i, k: (0, 0)),
            pl.BlockSpec((1, d), lambda i, k: (0, 0)),
            pl.BlockSpec((d, tf), lambda i, k: (0, k)),
            pl.BlockSpec((d, tf), lambda i, k: (0, k)),
            pl.BlockSpec((tf, d), lambda i, k: (k, 0)),
        ],
        out_specs=pl.BlockSpec((tm, d), lambda i, k: (i, 0)),
        out_shape=jax.ShapeDtypeStruct((t, d), F32),
        scratch_shapes=[pltpu.VMEM((tm, d), BF16), pltpu.VMEM((tm, d), F32)],
        compiler_params=_params("parallel", "arbitrary"),
        name="swiglu_ffn",
    )(x, gain_pre.reshape(1, d), gain_post.reshape(1, d), w_gate, w_up, w_down)


def _layer(x, norm_mix_pre, w_in, w_attn_up, ssm_a_re, ssm_a_im, ssm_log_dt, ssm_b_re, ssm_b_im,
           ssm_c_re, ssm_c_im, ssm_d, w_glu_v, w_glu_g, w_out, norm_mix_post, norm_ffn_pre,
           w_ffn_gate, w_ffn_up, w_ffn_down, norm_ffn_post):
    bsz, seq, d = x.shape
    t = bsz * seq
    hq = N_HEADS * HEAD_DIM
    ssm_width = ssm_d.shape[0]
    u_col0 = 3 * hq
    ga_col0 = u_col0 + ssm_width
    gs_col0 = ga_col0 + d
    assert w_in.shape[1] == gs_col0 + d

    x2 = x.reshape(t, d)
    proj = _inproj(x2, norm_mix_pre, w_in.astype(BF16), tm=min(1024, t), tn=512)
    attn = _attention(proj, bsz, seq)
    ssm_w = _ssm_weights(ssm_a_re, ssm_a_im, ssm_log_dt, ssm_b_re, ssm_b_im, ssm_c_re, ssm_c_im)
    y = _ssm(proj, u_col0, ssm_w, ssm_d, bsz, seq, tc=min(512, seq // SSM_CHUNK))
    merged = _merge(attn, y, proj, ga_col0, gs_col0, w_attn_up.astype(BF16), w_glu_v.astype(BF16),
                    w_glu_g.astype(BF16), tm=min(1024, t), tn=512)
    x1 = _outproj(merged, w_out.astype(BF16), x2, norm_mix_post, tm=512)
    out = _ffn(x1, norm_ffn_pre, norm_ffn_post, w_ffn_gate.astype(BF16), w_ffn_up.astype(BF16),
               w_ffn_down.astype(BF16), tm=512, tf=512)
    return out.reshape(bsz, seq, d)


def kernel(x, norm_mix_pre, w_in, w_attn_up, ssm_a_re, ssm_a_im, ssm_log_dt, ssm_b_re, ssm_b_im, ssm_c_re, ssm_c_im, ssm_d, w_glu_v, w_glu_g, w_out, norm_mix_post, norm_ffn_pre, w_ffn_gate, w_ffn_up, w_ffn_down, norm_ffn_post):
    stacked = (norm_mix_pre, w_in, w_attn_up, ssm_a_re, ssm_a_im, ssm_log_dt, ssm_b_re, ssm_b_im, ssm_c_re,
               ssm_c_im, ssm_d, w_glu_v, w_glu_g, w_out, norm_mix_post, norm_ffn_pre, w_ffn_gate, w_ffn_up,
               w_ffn_down, norm_ffn_post)
    for layer in range(norm_mix_pre.shape[0]):
        x = _layer(x, *(p[layer] for p in stacked))
    return x
```

```python
import functools

import jax
import jax.numpy as jnp
from jax import lax
from jax.experimental import pallas as pl
from jax.experimental.pallas import tpu as pltpu

F32 = jnp.float32
BF16 = jnp.bfloat16

EPS = 1e-6
HEAD_DIM = 128
HEADS_PER_GROUP = 4
ATTN_GROUPS = ((128, 1), (512, 4), (2048, 16))
N_GROUPS = len(ATTN_GROUPS)
N_HEADS = HEADS_PER_GROUP * N_GROUPS
GROUP_WIDTH = HEADS_PER_GROUP * HEAD_DIM
ATTN_BLK = 128
ATTN_TILE = 2048
SSM_GROUP = 16
SSM_STATE = 64
SSM_CHUNK = 8
LANES = 128
GROUPS_PER_LANE_BLOCK = LANES // SSM_GROUP
PROJ_TILE = 1024
NEG = -1e30
VMEM_LIMIT = 56 * 1024 * 1024


def _params(*sem):
    return pltpu.CompilerParams(dimension_semantics=sem, vmem_limit_bytes=VMEM_LIMIT)


def _rms(x, gain):
    return x * lax.rsqrt(jnp.mean(x * x, axis=-1, keepdims=True) + EPS) * gain


def _inproj_kernel(x_ref, g_ref, w_ref, *refs, u_tiles):
    qkv_refs = refs[:N_GROUPS]
    u_ref, gates_ref, h_ref, acc_ref = refs[N_GROUPS:]
    j = pl.program_id(1)
    slabs, tm, _ = acc_ref.shape

    @pl.when(j == 0)
    def _():
        h_ref[...] = _rms(x_ref[...], g_ref[...]).astype(BF16)

    acc = jnp.dot(h_ref[...], w_ref[...], preferred_element_type=F32)
    for s in range(slabs):
        acc_ref[s] = acc[:, s * LANES:(s + 1) * LANES]

    def emit_residue_major(o_ref, dil):
        for r in range(dil):
            rows = pl.ds(r, tm // dil, stride=dil) if dil > 1 else pl.ds(0, tm)
            for s in range(slabs):
                o_ref[0, r, :, s * LANES:(s + 1) * LANES] = acc_ref[s, rows, :].astype(o_ref.dtype)

    n_qkv = 3 * N_GROUPS
    for g, (_, dil) in enumerate(ATTN_GROUPS):
        pl.when(jnp.logical_and(j < n_qkv, j % N_GROUPS == g))(
            functools.partial(emit_residue_major, qkv_refs[g], dil))
    pl.when(jnp.logical_and(j >= n_qkv, j < n_qkv + u_tiles))(
        functools.partial(emit_residue_major, u_ref, SSM_CHUNK))

    @pl.when(j >= n_qkv + u_tiles)
    def _():
        for s in range(slabs):
            gates_ref[:, s * LANES:(s + 1) * LANES] = acc_ref[s].astype(gates_ref.dtype)


def _inproj(x, gain, w, ssm_width):
    t, d = x.shape
    tm, tn = PROJ_TILE, GROUP_WIDTH
    n = w.shape[1]
    n_qkv = 3 * N_GROUPS
    u_tiles = ssm_width // tn
    gate_tiles = n // tn - n_qkv - u_tiles
    assert t % tm == 0 and n % tn == 0 and ssm_width % tn == 0
    nt = t // tm

    def qkv_index(i, j, *, g):
        return i, 0, 0, jnp.clip((j - g) // N_GROUPS, 0, 2)

    out_shapes, out_specs = [], []
    for g, (_, dil) in enumerate(ATTN_GROUPS):
        assert tm % (dil * 16) == 0
        out_shapes.append(jax.ShapeDtypeStruct((nt, dil, tm // dil, 3 * tn), BF16))
        out_specs.append(pl.BlockSpec((1, dil, tm // dil, tn), functools.partial(qkv_index, g=g)))
    out_shapes.append(jax.ShapeDtypeStruct((nt, SSM_CHUNK, tm // SSM_CHUNK, ssm_width), BF16))
    out_specs.append(pl.BlockSpec((1, SSM_CHUNK, tm // SSM_CHUNK, tn),
                                  lambda i, j: (i, 0, 0, jnp.clip(j - n_qkv, 0, u_tiles - 1))))
    out_shapes.append(jax.ShapeDtypeStruct((t, gate_tiles * tn), BF16))
    out_specs.append(pl.BlockSpec((tm, tn), lambda i, j: (i, jnp.clip(j - n_qkv - u_tiles, 0, gate_tiles - 1))))
    return pl.pallas_call(
        functools.partial(_inproj_kernel, u_tiles=u_tiles),
        grid=(nt, n // tn),
        in_specs=[
            pl.BlockSpec((tm, d), lambda i, j: (i, 0)),
            pl.BlockSpec((1, d), lambda i, j: (0, 0)),
            pl.BlockSpec((d, tn), lambda i, j: (0, j)),
        ],
        out_specs=out_specs,
        out_shape=out_shapes,
        scratch_shapes=[pltpu.VMEM((tm, d), BF16), pltpu.VMEM((tn // LANES, tm, LANES), F32)],
        compiler_params=_params("parallel", "arbitrary"),
        name="inproj",
    )(x, gain.reshape(1, d), w)


def _alibi_slope(head):
    return 2.0 ** (-8.0 * (head + 1) / N_HEADS)


def _attn_kernel(*refs):
    in_refs = refs[: 5 * N_GROUPS]
    o_ref = refs[5 * N_GROUPS]
    out_scr, lse_scr = refs[5 * N_GROUPS + 1:]
    tile = pl.program_id(1)
    u = pl.program_id(2)
    units = ATTN_TILE // ATTN_BLK
    row = lax.broadcasted_iota(jnp.int32, (ATTN_BLK, ATTN_BLK), 0)
    col = lax.broadcasted_iota(jnp.int32, (ATTN_BLK, ATTN_BLK), 1)
    diff = (row - col).astype(F32)
    cur_ok = col <= row
    prev_ok = col >= row
    scale = HEAD_DIM ** -0.5
    nt = (((1,), (1,)), ((), ()))

    def head_block(ref, cs):
        return ref[..., cs].reshape(ATTN_BLK, HEAD_DIM)

    for g, (_, dil) in enumerate(ATTN_GROUPS):
        q_ref, kc_ref, kp_ref, vc_ref, vp_ref = in_refs[5 * g: 5 * g + 5]
        nb = u // dil
        r = u % dil
        no_prev = jnp.where(jnp.logical_and(tile == 0, nb == 0), NEG, 0.0).astype(F32)
        start = nb * (ATTN_BLK * dil) + r
        rows = pl.ds(start, ATTN_BLK) if dil == 1 else pl.ds(start, ATTN_BLK, stride=dil)
        for h in range(HEADS_PER_GROUP):
            cs = slice(h * HEAD_DIM, (h + 1) * HEAD_DIM)
            slope = _alibi_slope(g * HEADS_PER_GROUP + h) * dil
            q = head_block(q_ref, cs)
            sc = lax.dot_general(q, head_block(kc_ref, cs), nt, preferred_element_type=F32) * scale - slope * diff
            sp = (lax.dot_general(q, head_block(kp_ref, cs), nt, preferred_element_type=F32) * scale
                  - slope * (diff + ATTN_BLK))
            sc = jnp.where(cur_ok, sc, NEG)
            sp = jnp.where(prev_ok, sp, NEG) + no_prev
            m = jnp.maximum(jnp.max(sc, axis=1, keepdims=True), jnp.max(sp, axis=1, keepdims=True))
            pc = jnp.exp(sc - m)
            pp = jnp.exp(sp - m)
            l = jnp.sum(pc, axis=1, keepdims=True) + jnp.sum(pp, axis=1, keepdims=True)
            o = jnp.dot(pc.astype(BF16), head_block(vc_ref, cs), preferred_element_type=F32)
            o = o + jnp.dot(pp.astype(BF16), head_block(vp_ref, cs), preferred_element_type=F32)
            out_scr[g, h, rows, :] = o / l
            lse_scr[g, h, rows, :] = jnp.broadcast_to(m + jnp.log(l), (ATTN_BLK, HEAD_DIM))

    @pl.when(u == units - 1)
    def _():
        for h in range(HEADS_PER_GROUP):
            lses = [lse_scr[g, h] for g in range(N_GROUPS)]
            top = functools.reduce(jnp.maximum, lses)
            ws = [jnp.exp(x - top) for x in lses]
            num = sum(w * out_scr[g, h] for g, w in enumerate(ws))
            o_ref[:, h * HEAD_DIM:(h + 1) * HEAD_DIM] = (num / sum(ws)).astype(o_ref.dtype)


def _attention(qkvs, bsz, seq):
    assert seq % ATTN_TILE == 0 and ATTN_TILE % PROJ_TILE == 0
    tiles = seq // ATTN_TILE
    units = ATTN_TILE // ATTN_BLK
    operands, in_specs = [], []
    for (window, dil), arr in zip(ATTN_GROUPS, qkvs):
        assert window // dil == ATTN_BLK and units % dil == 0
        span = ATTN_BLK * dil
        spans_per_tile = ATTN_TILE // span
        if span <= PROJ_TILE:
            spans_per_ptile = PROJ_TILE // span
            block = (None, None, ATTN_BLK, GROUP_WIDTH)

            def index(b, tile, u, *, part, prev, dil=dil, spt=spans_per_tile, sppt=spans_per_ptile,
                      per_seq=seq // PROJ_TILE):
                sp = tile * spt + u // dil
                if prev:
                    sp = jnp.maximum(sp - 1, 0)
                return b * per_seq + sp // sppt, u % dil, sp % sppt, part
        else:
            block = (span // PROJ_TILE, None, PROJ_TILE // dil, GROUP_WIDTH)

            def index(b, tile, u, *, part, prev, dil=dil, spt=spans_per_tile, per_seq=seq // span):
                sp = tile * spt + u // dil
                if prev:
                    sp = jnp.maximum(sp - 1, 0)
                return b * per_seq + sp, u % dil, 0, part

        for part, prev in ((0, False), (1, False), (1, True), (2, False), (2, True)):
            operands.append(arr)
            in_specs.append(pl.BlockSpec(block, functools.partial(index, part=part, prev=prev)))
    scratch = pltpu.VMEM((N_GROUPS, HEADS_PER_GROUP, ATTN_TILE, HEAD_DIM), F32)
    return pl.pallas_call(
        _attn_kernel,
        grid=(bsz, tiles, units),
        in_specs=in_specs,
        out_specs=pl.BlockSpec((ATTN_TILE, GROUP_WIDTH), lambda b, tile, u: (b * tiles + tile, 0)),
        out_shape=jax.ShapeDtypeStruct((bsz * seq, GROUP_WIDTH), BF16),
        scratch_shapes=[scratch, scratch],
        compiler_params=_params("parallel", "parallel", "arbitrary"),
        name="dilated_attention",
    )(*operands)


def _cmul(a, b):
    return a[0] * b[0] - a[1] * b[1], a[0] * b[1] + a[1] * b[0]


def _ssm_weights(a_re, a_im, log_dt, b_re, b_im, c_re, c_im):
    n_groups = a_re.shape[0]
    nblk = n_groups // GROUPS_PER_LANE_BLOCK
    gl = GROUPS_PER_LANE_BLOCK
    L = SSM_CHUNK
    hi = lax.Precision.HIGHEST
    a_re, a_im = a_re.astype(F32), a_im.astype(F32)
    dt = jnp.exp(log_dt.astype(F32))[:, None]
    steps = jnp.arange(L + 1, dtype=F32)[None, :, None]
    mag = jnp.exp((a_re * dt)[:, None, :] * steps)
    ang = (a_im * dt)[:, None, :] * steps
    powers = (mag * jnp.cos(ang), mag * jnp.sin(ang))
    lam_bar = (powers[0][:, 1], powers[1][:, 1])
    den = a_re * a_re + a_im * a_im
    num = (lam_bar[0] - 1.0, lam_bar[1])
    ratio = ((num[0] * a_re + num[1] * a_im) / den, (num[1] * a_re - num[0] * a_im) / den)
    b_bar = _cmul((ratio[0][..., None], ratio[1][..., None]), (b_re.astype(F32), b_im.astype(F32)))
    c_t = (c_re.astype(F32).transpose(0, 2, 1), c_im.astype(F32).transpose(0, 2, 1))
    eye = jnp.eye(gl, dtype=F32)

    cb = _cmul((c_t[0][:, :, :, None], c_t[1][:, :, :, None]),
               (b_bar[0][:, :, None, :], b_bar[1][:, :, None, :]))
    kern = (jnp.einsum('gnp,gpoc->gnco', powers[0][:, :L], cb[0], precision=hi)
            - jnp.einsum('gnp,gpoc->gnco', powers[1][:, :L], cb[1], precision=hi))
    zero = jnp.zeros_like(kern[:, 0])
    toe = jnp.stack([jnp.stack([kern[:, i - j] if i >= j else zero for i in range(L)], axis=1)
                     for j in range(L)], axis=1)
    toe = toe.reshape(nblk, gl, L, L, SSM_GROUP, SSM_GROUP)
    w_toe = toe.transpose(0, 2, 1, 4, 3, 5)[:, :, :, :, :, None, :] * eye[None, None, :, None, None, :, None]
    w_toe = w_toe.reshape(nblk, L * LANES, L * LANES)

    rev = tuple(jnp.stack([p[:, L - 1 - j] for j in range(L)], axis=1) for p in powers)
    qb = _cmul((rev[0][..., None], rev[1][..., None]), (b_bar[0][:, None], b_bar[1][:, None]))
    qb = jnp.stack(qb, axis=0).reshape(2, nblk, gl, L, SSM_STATE, SSM_GROUP)
    w_in = qb.transpose(1, 3, 2, 5, 0, 4)[:, :, :, :, :, None, :] * eye[None, None, :, None, None, :, None]
    w_in = w_in.reshape(nblk, L * LANES, 2 * gl * SSM_STATE)

    c_cp = (c_re.astype(F32)[:, None], c_im.astype(F32)[:, None])
    e = _cmul(c_cp, (powers[0][:, 1:, None, :], powers[1][:, 1:, None, :]))
    e = jnp.stack([e[0], -e[1]], axis=0).reshape(2, nblk, gl, L, SSM_GROUP, SSM_STATE)
    w_out = e.transpose(1, 0, 2, 5, 3, 4)[:, :, :, :, :, None, :] * eye[None, None, :, None, None, :, None]
    w_out = w_out.reshape(nblk, 2 * gl * SSM_STATE, L * LANES)

    a_chunk = jnp.stack([powers[0][:, L], powers[1][:, L]], axis=0).reshape(2, nblk, gl * SSM_STATE)
    a_chunk = a_chunk.transpose(1, 0, 2).reshape(nblk, 1, 2 * gl * SSM_STATE)
    return w_toe.astype(BF16), w_in.astype(BF16), w_out.astype(BF16), a_chunk


def _ssm_kernel(*refs, tiles_per_seq):
    u_refs = refs[:SSM_CHUNK]
    wt_ref, wb_ref, wc_ref, a_ref, d_ref, y_ref, s_scr, xp_scr, carry_scr, y_scr = refs[SSM_CHUNK:]
    tc = s_scr.shape[0]
    half = s_scr.shape[1] // 2

    @pl.when(pl.program_id(1) % tiles_per_seq == 0)
    def _():
        carry_scr[...] = jnp.zeros_like(carry_scr)

    us = [r[...].reshape(tc, LANES) for r in u_refs]
    ucat = jnp.concatenate(us, axis=1)
    s_scr[...] = jnp.dot(ucat, wb_ref[0], preferred_element_type=F32)
    ar = a_ref[0, :, :half]
    ai = a_ref[0, :, half:]

    def step(c, carry):
        xr, xi = carry
        xp_scr[pl.ds(c, 1), :half] = xr
        xp_scr[pl.ds(c, 1), half:] = xi
        s = s_scr[pl.ds(c, 1), :]
        return ar * xr - ai * xi + s[:, :half], ar * xi + ai * xr + s[:, half:]

    xr, xi = lax.fori_loop(0, tc, step, (carry_scr[:, :half], carry_scr[:, half:]))
    carry_scr[:, :half] = xr
    carry_scr[:, half:] = xi

    y = jnp.dot(ucat, wt_ref[0], preferred_element_type=F32)
    y = y + jnp.dot(xp_scr[...].astype(BF16), wc_ref[0], preferred_element_type=F32)
    for i in range(SSM_CHUNK):
        yi = y[:, i * LANES:(i + 1) * LANES] + d_ref[0] * us[i].astype(F32)
        y_scr[pl.ds(i, tc, stride=SSM_CHUNK), :] = jax.nn.gelu(yi)
    y_ref[...] = y_scr[...].astype(y_ref.dtype)


def _ssm(u8, ssm_w, d_skip, bsz, seq, *, ptiles):
    nt, L, crows, width = u8.shape
    w_toe, w_in, w_out, a_chunk = ssm_w
    nblk = w_toe.shape[0]
    assert L == SSM_CHUNK and nblk * LANES == width
    tc = ptiles * crows
    rows_per_seq = seq // L
    assert rows_per_seq % tc == 0 and nt % ptiles == 0
    u_specs = [
        pl.BlockSpec((ptiles, None, crows, LANES), functools.partial(lambda blk, i, j: (i, j, 0, blk), j=j))
        for j in range(L)
    ]
    wide = L * LANES
    states = w_in.shape[2]
    return pl.pallas_call(
        functools.partial(_ssm_kernel, tiles_per_seq=rows_per_seq // tc),
        grid=(nblk, nt // ptiles),
        in_specs=u_specs + [
            pl.BlockSpec((1, wide, wide), lambda blk, i: (blk, 0, 0)),
            pl.BlockSpec((1, wide, states), lambda blk, i: (blk, 0, 0)),
            pl.BlockSpec((1, states, wide), lambda blk, i: (blk, 0, 0)),
            pl.BlockSpec((1, 1, states), lambda blk, i: (blk, 0, 0)),
            pl.BlockSpec((1, 1, LANES), lambda blk, i: (blk, 0, 0)),
        ],
        out_specs=pl.BlockSpec((tc * L, LANES), lambda blk, i: (i, blk)),
        out_shape=jax.ShapeDtypeStruct((nt * PROJ_TILE, width), BF16),
        scratch_shapes=[
            pltpu.VMEM((tc, states), F32),
            pltpu.VMEM((tc, states), F32),
            pltpu.VMEM((1, states), F32),
            pltpu.VMEM((tc * L, LANES), F32),
        ],
        compiler_params=_params("parallel", "arbitrary"),
        name="s5_chunked",
    )(*([u8] * L), w_toe, w_in, w_out, a_chunk, d_skip.astype(F32).reshape(nblk, 1, LANES))


def _merge_kernel(attn_ref, y_ref, ga_ref, gs_ref, wup_ref, wv_ref, wg_ref, o_ref):
    y = y_ref[...]
    attn_branch = jnp.dot(attn_ref[...], wup_ref[...], preferred_element_type=F32)
    val = jnp.dot(y, wv_ref[...], preferred_element_type=F32)
    gate = jnp.dot(y, wg_ref[...], preferred_element_type=F32)
    ssm_branch = val * jax.nn.sigmoid(gate)
    merged = (jax.nn.sigmoid(ga_ref[...].astype(F32)) * attn_branch
              + jax.nn.sigmoid(gs_ref[...].astype(F32)) * ssm_branch)
    o_ref[...] = merged.astype(o_ref.dtype)


def _merge(attn, y, gates, w_up, w_v, w_g, *, tm, tn):
    t = attn.shape[0]
    n = w_up.shape[1]
    assert gates.shape[1] == 2 * n and n % tn == 0
    return pl.pallas_call(
        _merge_kernel,
        grid=(t // tm, n // tn),
        in_specs=[
            pl.BlockSpec((tm, attn.shape[1]), lambda i, j: (i, 0)),
            pl.BlockSpec((tm, y.shape[1]), lambda i, j: (i, 0)),
            pl.BlockSpec((tm, tn), lambda i, j: (i, j)),
            pl.BlockSpec((tm, tn), lambda i, j: (i, n // tn + j)),
            pl.BlockSpec((w_up.shape[0], tn), lambda i, j: (0, j)),
            pl.BlockSpec((w_v.shape[0], tn), lambda i, j: (0, j)),
            pl.BlockSpec((w_g.shape[0], tn), lambda i, j: (0, j)),
        ],
        out_specs=pl.BlockSpec((tm, tn), lambda i, j: (i, j)),
        out_shape=jax.ShapeDtypeStruct((t, n), BF16),
        compiler_params=_params("parallel", "arbitrary"),
        name="gated_merge",
    )(attn, y, gates, gates, w_up, w_v, w_g)


def _outproj_kernel(m_ref, w_ref, x_ref, g_ref, o_ref):
    z = jnp.dot(m_ref[...], w_ref[...], preferred_element_type=F32)
    o_ref[...] = x_ref[...] + _rms(z, g_ref[...])


def _outproj(merged, w, x, gain, *, tm):
    t, d = x.shape
    return pl.pallas_call(
        _outproj_kernel,
        grid=(t // tm,),
        in_specs=[
            pl.BlockSpec((tm, merged.shape[1]), lambda i: (i, 0)),
            pl.BlockSpec(w.shape, lambda i: (0, 0)),
            pl.BlockSpec((tm, d), lambda i: (i, 0)),
            pl.BlockSpec((1, d), lambda i: (0, 0)),
        ],
        out_specs=pl.BlockSpec((tm, d), lambda i: (i, 0)),
        out_shape=jax.ShapeDtypeStruct((t, d), F32),
        compiler_params=_params("parallel"),
        name="outproj_norm_residual",
    )(merged, w, x, gain.reshape(1, d))


def _ffn_kernel(x_ref, gpre_ref, gpost_ref, wg_ref, wu_ref, wd_ref, o_ref, h_ref, acc_ref):
    k = pl.program_id(1)

    @pl.when(k == 0)
    def _():
        h_ref[...] = _rms(x_ref[...], gpre_ref[...]).astype(BF16)
        acc_ref[...] = jnp.zeros_like(acc_ref)

    h = h_ref[...]
    gate = jnp.dot(h, wg_ref[...], preferred_element_type=F32)
    up = jnp.dot(h, wu_ref[...], preferred_element_type=F32)
    f = (jax.nn.silu(gate) * up).astype(BF16)
    acc_ref[...] += jnp.dot(f, wd_ref[...], preferred_element_type=F32)

    @pl.when(k == pl.num_programs(1) - 1)
    def _():
        o_ref[...] = x_ref[...] + _rms(acc_ref[...], gpost_ref[...])


def _ffn(x, gain_pre, gain_post, w_gate, w_up, w_down, *, tm, tf):
    t, d = x.shape
    dff = w_gate.shape[1]
    return pl.pallas_call(
        _ffn_kernel,
        grid=(t // tm, dff // tf),
        in_specs=[
            pl.BlockSpec((tm, d), lambda i, k: (i, 0)),
            pl.BlockSpec((1, d), lambda i, k: (0, 0)),
            pl.BlockSpec((1, d), lambda i, k: (0, 0)),
            pl.BlockSpec((d, tf), lambda i, k: (0, k)),
            pl.BlockSpec((d, tf), lambda i, k: (0, k)),
            pl.BlockSpec((tf, d), lambda i, k: (k, 0)),
        ],
        out_specs=pl.BlockSpec((tm, d), lambda i, k: (i, 0)),
        out_shape=jax.ShapeDtypeStruct((t, d), F32),
        scratch_shapes=[pltpu.VMEM((tm, d), BF16), pltpu.VMEM((tm, d), F32)],
        compiler_params=_params("parallel", "arbitrary"),
        name="swiglu_ffn",
    )(x, gain_pre.reshape(1, d), gain_post.reshape(1, d), w_gate, w_up, w_down)


def _layer(x, norm_mix_pre, w_in, w_attn_up, ssm_a_re, ssm_a_im, ssm_log_dt, ssm_b_re, ssm_b_im,
           ssm_c_re, ssm_c_im, ssm_d, w_glu_v, w_glu_g, w_out, norm_mix_post, norm_ffn_pre,
           w_ffn_gate, w_ffn_up, w_ffn_down, norm_ffn_post):
    bsz, seq, d = x.shape
    t = bsz * seq
    ssm_width = ssm_d.shape[0]
    assert w_in.shape[1] == 3 * N_HEADS * HEAD_DIM + ssm_width + 2 * d

    x2 = x.reshape(t, d)
    *qkvs, u8, gates = _inproj(x2, norm_mix_pre, w_in.astype(BF16), ssm_width)
    attn = _attention(qkvs, bsz, seq)
    ssm_w = _ssm_weights(ssm_a_re, ssm_a_im, ssm_log_dt, ssm_b_re, ssm_b_im, ssm_c_re, ssm_c_im)
    y = _ssm(u8, ssm_w, ssm_d, bsz, seq, ptiles=4)
    merged = _merge(attn, y, gates, w_attn_up.astype(BF16), w_glu_v.astype(BF16), w_glu_g.astype(BF16),
                    tm=1024, tn=512)
    x1 = _outproj(merged, w_out.astype(BF16), x2, norm_mix_post, tm=512)
    out = _ffn(x1, norm_ffn_pre, norm_ffn_post, w_ffn_gate.astype(BF16), w_ffn_up.astype(BF16),
               w_ffn_down.astype(BF16), tm=512, tf=512)
    return out.reshape(bsz, seq, d)


def kernel(x, norm_mix_pre, w_in, w_attn_up, ssm_a_re, ssm_a_im, ssm_log_dt, ssm_b_re, ssm_b_im, ssm_c_re, ssm_c_im, ssm_d, w_glu_v, w_glu_g, w_out, norm_mix_post, norm_ffn_pre, w_ffn_gate, w_ffn_up, w_ffn_down, norm_ffn_post):
    stacked = (norm_mix_pre, w_in, w_attn_up, ssm_a_re, ssm_a_im, ssm_log_dt, ssm_b_re, ssm_b_im, ssm_c_re,
               ssm_c_im, ssm_d, w_glu_v, w_glu_g, w_out, norm_mix_post, norm_ffn_pre, w_ffn_gate, w_ffn_up,
               w_ffn_down, norm_ffn_post)
    for layer in range(norm_mix_pre.shape[0]):
        x = _layer(x, *(p[layer] for p in stacked))
    return x
```

```python
import functools

import jax
import jax.numpy as jnp
import numpy as np
from jax import lax
from jax.experimental import pallas as pl
from jax.experimental.pallas import tpu as pltpu

F32 = jnp.float32
BF16 = jnp.bfloat16

EPS = 1e-6
HEAD_DIM = 128
HEADS_PER_GROUP = 4
ATTN_GROUPS = ((128, 1), (512, 4), (2048, 16))
N_GROUPS = len(ATTN_GROUPS)
N_HEADS = HEADS_PER_GROUP * N_GROUPS
GROUP_WIDTH = HEADS_PER_GROUP * HEAD_DIM
ATTN_BLK = 128
ATTN_TILE = 2048
SSM_GROUP = 16
SSM_STATE = 64
SSM_CHUNK = 8
LANES = 128
GROUPS_PER_LANE_BLOCK = LANES // SSM_GROUP
PROJ_TILE = 1024
NEG = -1e30
LOG2E = 1.4426950408889634
VMEM_LIMIT = 56 * 1024 * 1024


def _params(*sem):
    return pltpu.CompilerParams(dimension_semantics=sem, vmem_limit_bytes=VMEM_LIMIT)


def _rms(x, gain):
    return x * lax.rsqrt(jnp.mean(x * x, axis=-1, keepdims=True) + EPS) * gain


def _inproj_kernel(x_ref, g_ref, w_ref, *refs, u_tiles):
    qkv_refs = refs[:N_GROUPS]
    u_ref, gates_ref, h_ref, acc_ref = refs[N_GROUPS:]
    j = pl.program_id(1)
    slabs, tm, _ = acc_ref.shape

    @pl.when(j == 0)
    def _():
        h_ref[...] = _rms(x_ref[...], g_ref[...]).astype(BF16)

    acc = jnp.dot(h_ref[...], w_ref[...], preferred_element_type=F32)
    for s in range(slabs):
        acc_ref[s] = acc[:, s * LANES:(s + 1) * LANES]

    def emit_residue_major(o_ref, dil):
        for r in range(dil):
            rows = pl.ds(r, tm // dil, stride=dil) if dil > 1 else pl.ds(0, tm)
            for s in range(slabs):
                o_ref[0, r, :, s * LANES:(s + 1) * LANES] = acc_ref[s, rows, :].astype(o_ref.dtype)

    n_qkv = 3 * N_GROUPS
    for g, (_, dil) in enumerate(ATTN_GROUPS):
        pl.when(jnp.logical_and(j < n_qkv, j % N_GROUPS == g))(
            functools.partial(emit_residue_major, qkv_refs[g], dil))
    pl.when(jnp.logical_and(j >= n_qkv, j < n_qkv + u_tiles))(
        functools.partial(emit_residue_major, u_ref, SSM_CHUNK))

    @pl.when(j >= n_qkv + u_tiles)
    def _():
        for s in range(slabs):
            gates_ref[:, s * LANES:(s + 1) * LANES] = acc_ref[s].astype(gates_ref.dtype)


def _inproj(x, gain, w, ssm_width):
    t, d = x.shape
    tm, tn = PROJ_TILE, GROUP_WIDTH
    n = w.shape[1]
    n_qkv = 3 * N_GROUPS
    u_tiles = ssm_width // tn
    gate_tiles = n // tn - n_qkv - u_tiles
    assert t % tm == 0 and n % tn == 0 and ssm_width % tn == 0
    nt = t // tm

    def qkv_index(i, j, *, g):
        return i, 0, 0, jnp.clip((j - g) // N_GROUPS, 0, 2)

    out_shapes, out_specs = [], []
    for g, (_, dil) in enumerate(ATTN_GROUPS):
        assert tm % (dil * 16) == 0
        out_shapes.append(jax.ShapeDtypeStruct((nt, dil, tm // dil, 3 * tn), BF16))
        out_specs.append(pl.BlockSpec((1, dil, tm // dil, tn), functools.partial(qkv_index, g=g)))
    out_shapes.append(jax.ShapeDtypeStruct((nt, SSM_CHUNK, tm // SSM_CHUNK, ssm_width), BF16))
    out_specs.append(pl.BlockSpec((1, SSM_CHUNK, tm // SSM_CHUNK, tn),
                                  lambda i, j: (i, 0, 0, jnp.clip(j - n_qkv, 0, u_tiles - 1))))
    out_shapes.append(jax.ShapeDtypeStruct((t, gate_tiles * tn), BF16))
    out_specs.append(pl.BlockSpec((tm, tn), lambda i, j: (i, jnp.clip(j - n_qkv - u_tiles, 0, gate_tiles - 1))))
    return pl.pallas_call(
        functools.partial(_inproj_kernel, u_tiles=u_tiles),
        grid=(nt, n // tn),
        in_specs=[
            pl.BlockSpec((tm, d), lambda i, j: (i, 0)),
            pl.BlockSpec((1, d), lambda i, j: (0, 0)),
            pl.BlockSpec((d, tn), lambda i, j: (0, j)),
        ],
        out_specs=out_specs,
        out_shape=out_shapes,
        scratch_shapes=[pltpu.VMEM((tm, d), BF16), pltpu.VMEM((tn // LANES, tm, LANES), F32)],
        compiler_params=_params("parallel", "arbitrary"),
        name="inproj",
    )(x, gain.reshape(1, d), w)


def _attn_bias_table():
    qi = np.arange(ATTN_BLK)[:, None]
    kj = np.arange(ATTN_BLK)[None, :]
    table = np.full((N_GROUPS, 2, HEADS_PER_GROUP * ATTN_BLK, 2 * ATTN_BLK), NEG, np.float32)
    for g, (_, dil) in enumerate(ATTN_GROUPS):
        for h in range(HEADS_PER_GROUP):
            slope = 2.0 ** (-8.0 * (g * HEADS_PER_GROUP + h + 1) / N_HEADS) * dil * LOG2E
            rows = slice(h * ATTN_BLK, (h + 1) * ATTN_BLK)
            cur = np.where(kj <= qi, -slope * (qi - kj), NEG)
            prev = np.where(kj >= qi, -slope * (ATTN_BLK + qi - kj), NEG)
            table[g, :, rows, ATTN_BLK:] = cur
            table[g, 0, rows, :ATTN_BLK] = prev
    return table


def _attn_kernel(bias_ref, *refs):
    qkv_refs = refs[:N_GROUPS]
    o_ref, out_scr, lse_scr = refs[N_GROUPS:N_GROUPS + 3]
    hist_refs = refs[N_GROUPS + 3:]
    tile = pl.program_id(1)
    u = pl.program_id(2)
    units = ATTN_TILE // ATTN_BLK
    nt = (((1,), (1,)), ((), ()))
    k0, v0 = GROUP_WIDTH, 2 * GROUP_WIDTH

    @pl.when(jnp.logical_and(tile == 0, u == 0))
    def _():
        for hist in hist_refs:
            hist[...] = jnp.zeros_like(hist)

    for g, (_, dil) in enumerate(ATTN_GROUPS):
        hist = hist_refs[g]
        nb = u // dil
        r = u % dil
        first = jnp.logical_and(tile == 0, nb == 0).astype(jnp.int32)
        start = nb * (ATTN_BLK * dil) + r
        rows = pl.ds(start, ATTN_BLK) if dil == 1 else pl.ds(start, ATTN_BLK, stride=dil)
        x = qkv_refs[g][...].reshape(ATTN_BLK, 3 * GROUP_WIDTH)
        prev = hist[r]
        scores = []
        for h in range(HEADS_PER_GROUP):
            cs = slice(h * HEAD_DIM, (h + 1) * HEAD_DIM)
            keys = jnp.concatenate([prev[:, cs], x[:, k0:v0][:, cs]], axis=0)
            scores.append(lax.dot_general(x[:, cs], keys, nt, preferred_element_type=F32))
        s = jnp.concatenate(scores, axis=0) * (HEAD_DIM ** -0.5 * LOG2E) + bias_ref[g, first]
        m = jnp.max(s, axis=1, keepdims=True)
        p = jnp.exp2(s - m)
        l = jnp.sum(p, axis=1, keepdims=True)
        p = p.astype(BF16)
        inv = 1.0 / l
        lse = m + jnp.log2(l)
        for h in range(HEADS_PER_GROUP):
            cs = slice(h * HEAD_DIM, (h + 1) * HEAD_DIM)
            hr = slice(h * ATTN_BLK, (h + 1) * ATTN_BLK)
            vals = jnp.concatenate([prev[:, GROUP_WIDTH:][:, cs], x[:, v0:][:, cs]], axis=0)
            o = jnp.dot(p[hr], vals, preferred_element_type=F32)
            out_scr[g, h, rows, :] = o * inv[hr]
            lse_scr[g, h, rows, :] = jnp.broadcast_to(lse[hr], (ATTN_BLK, HEAD_DIM))
        hist[r] = x[:, k0:]

    @pl.when(u == units - 1)
    def _():
        for h in range(HEADS_PER_GROUP):
            lses = [lse_scr[g, h] for g in range(N_GROUPS)]
            top = functools.reduce(jnp.maximum, lses)
            ws = [jnp.exp2(x - top) for x in lses]
            num = sum(w * out_scr[g, h] for g, w in enumerate(ws))
            o_ref[:, h * HEAD_DIM:(h + 1) * HEAD_DIM] = (num / sum(ws)).astype(o_ref.dtype)


def _attention(qkvs, bsz, seq):
    assert seq % ATTN_TILE == 0 and ATTN_TILE % PROJ_TILE == 0
    tiles = seq // ATTN_TILE
    units = ATTN_TILE // ATTN_BLK
    bias = jnp.asarray(_attn_bias_table())
    in_specs = [pl.BlockSpec(bias.shape, lambda b, tile, u: (0, 0, 0, 0))]
    hist = []
    for (window, dil), arr in zip(ATTN_GROUPS, qkvs):
        assert window // dil == ATTN_BLK and units % dil == 0
        span = ATTN_BLK * dil
        spans_per_tile = ATTN_TILE // span
        if span <= PROJ_TILE:
            block = (None, None, ATTN_BLK, 3 * GROUP_WIDTH)

            def index(b, tile, u, *, dil=dil, spt=spans_per_tile, sppt=PROJ_TILE // span, per_seq=seq // PROJ_TILE):
                sp = tile * spt + u // dil
                return b * per_seq + sp // sppt, u % dil, sp % sppt, 0
        else:
            block = (span // PROJ_TILE, None, PROJ_TILE // dil, 3 * GROUP_WIDTH)

            def index(b, tile, u, *, dil=dil, spt=spans_per_tile, per_seq=seq // span):
                return b * per_seq + tile * spt + u // dil, u % dil, 0, 0

        in_specs.append(pl.BlockSpec(block, index))
        hist.append(pltpu.VMEM((dil, ATTN_BLK, 2 * GROUP_WIDTH), BF16))
    scratch = pltpu.VMEM((N_GROUPS, HEADS_PER_GROUP, ATTN_TILE, HEAD_DIM), F32)
    return pl.pallas_call(
        _attn_kernel,
        grid=(bsz, tiles, units),
        in_specs=in_specs,
        out_specs=pl.BlockSpec((ATTN_TILE, GROUP_WIDTH), lambda b, tile, u: (b * tiles + tile, 0)),
        out_shape=jax.ShapeDtypeStruct((bsz * seq, GROUP_WIDTH), BF16),
        scratch_shapes=[scratch, scratch] + hist,
        compiler_params=_params("arbitrary", "arbitrary", "arbitrary"),
        name="dilated_attention",
    )(bias, *qkvs)


def _cmul(a, b):
    return a[0] * b[0] - a[1] * b[1], a[0] * b[1] + a[1] * b[0]


def _ssm_weights(a_re, a_im, log_dt, b_re, b_im, c_re, c_im):
    n_groups = a_re.shape[0]
    nblk = n_groups // GROUPS_PER_LANE_BLOCK
    gl = GROUPS_PER_LANE_BLOCK
    L = SSM_CHUNK
    hi = lax.Precision.HIGHEST
    a_re, a_im = a_re.astype(F32), a_im.astype(F32)
    dt = jnp.exp(log_dt.astype(F32))[:, None]
    steps = jnp.arange(L + 1, dtype=F32)[None, :, None]
    mag = jnp.exp((a_re * dt)[:, None, :] * steps)
    ang = (a_im * dt)[:, None, :] * steps
    powers = (mag * jnp.cos(ang), mag * jnp.sin(ang))
    lam_bar = (powers[0][:, 1], powers[1][:, 1])
    den = a_re * a_re + a_im * a_im
    num = (lam_bar[0] - 1.0, lam_bar[1])
    ratio = ((num[0] * a_re + num[1] * a_im) / den, (num[1] * a_re - num[0] * a_im) / den)
    b_bar = _cmul((ratio[0][..., None], ratio[1][..., None]), (b_re.astype(F32), b_im.astype(F32)))
    c_t = (c_re.astype(F32).transpose(0, 2, 1), c_im.astype(F32).transpose(0, 2, 1))
    eye = jnp.eye(gl, dtype=F32)

    def block_diag(m):
        rows, cols = m.shape[1:]
        m = m.reshape(nblk, gl, rows, 1, cols) * eye[None, :, None, :, None]
        return m.reshape(nblk, gl * rows, gl * cols)

    cb = _cmul((c_t[0][:, :, :, None], c_t[1][:, :, :, None]),
               (b_bar[0][:, :, None, :], b_bar[1][:, :, None, :]))
    kern = (jnp.einsum('gnp,gpoc->gnco', powers[0][:, :L], cb[0], precision=hi)
            - jnp.einsum('gnp,gpoc->gnco', powers[1][:, :L], cb[1], precision=hi))
    toe = jnp.stack([block_diag(kern[:, n]) for n in range(L)], axis=1).astype(BF16)
    b_in = jnp.stack([block_diag(b.transpose(0, 2, 1)) for b in b_bar], axis=1)
    c_out = jnp.stack([block_diag(c) for c in c_t], axis=1)
    pw = jnp.stack(powers, axis=0).reshape(2, nblk, gl, L + 1, SSM_STATE)
    pw_row = pw.transpose(1, 0, 3, 2, 4).reshape(nblk, 2, L + 1, gl * SSM_STATE)
    pw_col = pw_row.transpose(0, 1, 3, 2)
    return toe, b_in, c_out, pw_row, pw_col


def _ssm_kernel(*refs, tiles_per_seq):
    u_refs = refs[:SSM_CHUNK]
    (toe_ref, bin_ref, cout_ref, pwr_ref, pwc_ref, d_ref, y_ref,
     wt_scr, wb_scr, wc_scr, s_scr, xp_scr, carry_scr, y_scr) = refs[SSM_CHUNK:]
    L = SSM_CHUNK
    tc = s_scr.shape[0]
    half = s_scr.shape[1] // 2

    @pl.when(pl.program_id(1) == 0)
    def _():
        for j in range(L):
            rows = slice(j * LANES, (j + 1) * LANES)
            for i in range(L):
                cols = slice(i * LANES, (i + 1) * LANES)
                wt_scr[rows, cols] = toe_ref[0, i - j] if i >= j else jnp.zeros((LANES, LANES), BF16)
            pr = pwr_ref[0, 0, L - 1 - j:L - j, :]
            pi = pwr_ref[0, 1, L - 1 - j:L - j, :]
            wb_scr[rows, :half] = (bin_ref[0, 0] * pr - bin_ref[0, 1] * pi).astype(BF16)
            wb_scr[rows, half:] = (bin_ref[0, 0] * pi + bin_ref[0, 1] * pr).astype(BF16)
            pr = pwc_ref[0, 0, :, j + 1:j + 2]
            pi = pwc_ref[0, 1, :, j + 1:j + 2]
            wc_scr[:half, rows] = (cout_ref[0, 0] * pr - cout_ref[0, 1] * pi).astype(BF16)
            wc_scr[half:, rows] = (-(cout_ref[0, 0] * pi + cout_ref[0, 1] * pr)).astype(BF16)

    @pl.when(pl.program_id(1) % tiles_per_seq == 0)
    def _():
        carry_scr[...] = jnp.zeros_like(carry_scr)

    us = [r[...].reshape(tc, LANES) for r in u_refs]
    ucat = jnp.concatenate(us, axis=1)
    s_scr[...] = jnp.dot(ucat, wb_scr[...], preferred_element_type=F32)
    ar = pwr_ref[0, 0, L:L + 1, :]
    ai = pwr_ref[0, 1, L:L + 1, :]

    def step(c, carry):
        xr, xi = carry
        xp_scr[pl.ds(c, 1), :half] = xr
        xp_scr[pl.ds(c, 1), half:] = xi
        s = s_scr[pl.ds(c, 1), :]
        return ar * xr - ai * xi + s[:, :half], ar * xi + ai * xr + s[:, half:]

    xr, xi = lax.fori_loop(0, tc, step, (carry_scr[:, :half], carry_scr[:, half:]))
    carry_scr[:, :half] = xr
    carry_scr[:, half:] = xi

    y = jnp.dot(ucat, wt_scr[...], preferred_element_type=F32)
    y = y + jnp.dot(xp_scr[...].astype(BF16), wc_scr[...], preferred_element_type=F32)
    for i in range(L):
        yi = y[:, i * LANES:(i + 1) * LANES] + d_ref[0] * us[i].astype(F32)
        y_scr[pl.ds(i, tc, stride=L), :] = jax.nn.gelu(yi)
    y_ref[...] = y_scr[...].astype(y_ref.dtype)


def _ssm(u8, ssm_w, d_skip, bsz, seq, *, ptiles):
    nt, L, crows, width = u8.shape
    nblk = ssm_w[0].shape[0]
    assert L == SSM_CHUNK and nblk * LANES == width
    tc = ptiles * crows
    rows_per_seq = seq // L
    assert rows_per_seq % tc == 0 and nt % ptiles == 0
    u_specs = [
        pl.BlockSpec((ptiles, None, crows, LANES), functools.partial(lambda blk, i, j: (i, j, 0, blk), j=j))
        for j in range(L)
    ]
    w_specs = [pl.BlockSpec((1,) + w.shape[1:], lambda blk, i: (blk, 0, 0, 0)) for w in ssm_w]
    wide = L * LANES
    states = 2 * GROUPS_PER_LANE_BLOCK * SSM_STATE
    return pl.pallas_call(
        functools.partial(_ssm_kernel, tiles_per_seq=rows_per_seq // tc),
        grid=(nblk, nt // ptiles),
        in_specs=u_specs + w_specs + [pl.BlockSpec((1, 1, LANES), lambda blk, i: (blk, 0, 0))],
        out_specs=pl.BlockSpec((tc * L, LANES), lambda blk, i: (i, blk)),
        out_shape=jax.ShapeDtypeStruct((nt * PROJ_TILE, width), BF16),
        scratch_shapes=[
            pltpu.VMEM((wide, wide), BF16),
            pltpu.VMEM((wide, states), BF16),
            pltpu.VMEM((states, wide), BF16),
            pltpu.VMEM((tc, states), F32),
            pltpu.VMEM((tc, states), F32),
            pltpu.VMEM((1, states), F32),
            pltpu.VMEM((tc * L, LANES), F32),
        ],
        compiler_params=_params("parallel", "arbitrary"),
        name="s5_chunked",
    )(*([u8] * L), *ssm_w, d_skip.astype(F32).reshape(nblk, 1, LANES))


def _merge_kernel(attn_ref, y_ref, ga_ref, gs_ref, wup_ref, wv_ref, wg_ref, o_ref):
    y = y_ref[...]
    attn_branch = jnp.dot(attn_ref[...], wup_ref[...], preferred_element_type=F32)
    val = jnp.dot(y, wv_ref[...], preferred_element_type=F32)
    gate = jnp.dot(y, wg_ref[...], preferred_element_type=F32)
    ssm_branch = val * jax.nn.sigmoid(gate)
    merged = (jax.nn.sigmoid(ga_ref[...].astype(F32)) * attn_branch
              + jax.nn.sigmoid(gs_ref[...].astype(F32)) * ssm_branch)
    o_ref[...] = merged.astype(o_ref.dtype)


def _merge(attn, y, gates, w_up, w_v, w_g, *, tm, tn):
    t = attn.shape[0]
    n = w_up.shape[1]
    assert gates.shape[1] == 2 * n and n % tn == 0
    return pl.pallas_call(
        _merge_kernel,
        grid=(t // tm, n // tn),
        in_specs=[
            pl.BlockSpec((tm, attn.shape[1]), lambda i, j: (i, 0)),
            pl.BlockSpec((tm, y.shape[1]), lambda i, j: (i, 0)),
            pl.BlockSpec((tm, tn), lambda i, j: (i, j)),
            pl.BlockSpec((tm, tn), lambda i, j: (i, n // tn + j)),
            pl.BlockSpec((w_up.shape[0], tn), lambda i, j: (0, j)),
            pl.BlockSpec((w_v.shape[0], tn), lambda i, j: (0, j)),
            pl.BlockSpec((w_g.shape[0], tn), lambda i, j: (0, j)),
        ],
        out_specs=pl.BlockSpec((tm, tn), lambda i, j: (i, j)),
        out_shape=jax.ShapeDtypeStruct((t, n), BF16),
        compiler_params=_params("parallel", "arbitrary"),
        name="gated_merge",
    )(attn, y, gates, gates, w_up, w_v, w_g)


def _outproj_kernel(m_ref, w_ref, x_ref, g_ref, o_ref):
    z = jnp.dot(m_ref[...], w_ref[...], preferred_element_type=F32)
    o_ref[...] = x_ref[...] + _rms(z, g_ref[...])


def _outproj(merged, w, x, gain, *, tm):
    t, d = x.shape
    return pl.pallas_call(
        _outproj_kernel,
        grid=(t // tm,),
        in_specs=[
            pl.BlockSpec((tm, merged.shape[1]), lambda i: (i, 0)),
            pl.BlockSpec(w.shape, lambda i: (0, 0)),
            pl.BlockSpec((tm, d), lambda i: (i, 0)),
            pl.BlockSpec((1, d), lambda i: (0, 0)),
        ],
        out_specs=pl.BlockSpec((tm, d), lambda i: (i, 0)),
        out_shape=jax.ShapeDtypeStruct((t, d), F32),
        compiler_params=_params("parallel"),
        name="outproj_norm_residual",
    )(merged, w, x, gain.reshape(1, d))


def _ffn_kernel(x_ref, gpre_ref, gpost_ref, wg_ref, wu_ref, wd_ref, o_ref, h_ref, acc_ref):
    k = pl.program_id(1)

    @pl.when(k == 0)
    def _():
        h_ref[...] = _rms(x_ref[...], gpre_ref[...]).astype(BF16)
        acc_ref[...] = jnp.zeros_like(acc_ref)

    h = h_ref[...]
    gate = jnp.dot(h, wg_ref[...], preferred_element_type=F32)
    up = jnp.dot(h, wu_ref[...], preferred_element_type=F32)
    f = (jax.nn.silu(gate) * up).astype(BF16)
    acc_ref[...] += jnp.dot(f, wd_ref[...], preferred_element_type=F32)

    @pl.when(k == pl.num_programs(1) - 1)
    def _():
        o_ref[...] = x_ref[...] + _rms(acc_ref[...], gpost_ref[...])


def _ffn(x, gain_pre, gain_post, w_gate, w_up, w_down, *, tm, tf):
    t, d = x.shape
    dff = w_gate.shape[1]
    return pl.pallas_call(
        _ffn_kernel,
        grid=(t // tm, dff // tf),
        in_specs=[
            pl.BlockSpec((tm, d), lambda i, k: (i, 0)),
            pl.BlockSpec((1, d), lambda i, k: (0, 0)),
            pl.BlockSpec((1, d), lambda i, k: (0, 0)),
            pl.BlockSpec((d, tf), lambda i, k: (0, k)),
            pl.BlockSpec((d, tf), lambda i, k: (0, k)),
            pl.BlockSpec((tf, d), lambda i, k: (k, 0)),
        ],
        out_specs=pl.BlockSpec((tm, d), lambda i, k: (i, 0)),
        out_shape=jax.ShapeDtypeStruct((t, d), F32),
        scratch_shapes=[pltpu.VMEM((tm, d), BF16), pltpu.VMEM((tm, d), F32)],
        compiler_params=_params("parallel", "arbitrary"),
        name="swiglu_ffn",
    )(x, gain_pre.reshape(1, d), gain_post.reshape(1, d), w_gate, w_up, w_down)


def _layer(x, norm_mix_pre, w_in, w_attn_up, ssm_a_re, ssm_a_im, ssm_log_dt, ssm_b_re, ssm_b_im,
           ssm_c_re, ssm_c_im, ssm_d, w_glu_v, w_glu_g, w_out, norm_mix_post, norm_ffn_pre,
           w_ffn_gate, w_ffn_up, w_ffn_down, norm_ffn_post):
    bsz, seq, d = x.shape
    t = bsz * seq
    ssm_width = ssm_d.shape[0]
    assert w_in.shape[1] == 3 * N_HEADS * HEAD_DIM + ssm_width + 2 * d

    x2 = x.reshape(t, d)
    *qkvs, u8, gates = _inproj(x2, norm_mix_pre, w_in.astype(BF16), ssm_width)
    attn = _attention(qkvs, bsz, seq)
    ssm_w = _ssm_weights(ssm_a_re, ssm_a_im, ssm_log_dt, ssm_b_re, ssm_b_im, ssm_c_re, ssm_c_im)
    y = _ssm(u8, ssm_w, ssm_d, bsz, seq, ptiles=4)
    merged = _merge(attn, y, gates, w_attn_up.astype(BF16), w_glu_v.astype(BF16), w_glu_g.astype(BF16),
                    tm=1024, tn=512)
    x1 = _outproj(merged, w_out.astype(BF16), x2, norm_mix_post, tm=512)
    out = _ffn(x1, norm_ffn_pre, norm_ffn_post, w_ffn_gate.astype(BF16), w_ffn_up.astype(BF16),
               w_ffn_down.astype(BF16), tm=512, tf=512)
    return out.reshape(bsz, seq, d)


def kernel(x, norm_mix_pre, w_in, w_attn_up, ssm_a_re, ssm_a_im, ssm_log_dt, ssm_b_re, ssm_b_im, ssm_c_re, ssm_c_im, ssm_d, w_glu_v, w_glu_g, w_out, norm_mix_post, norm_ffn_pre, w_ffn_gate, w_ffn_up, w_ffn_down, norm_ffn_post):
    stacked = (norm_mix_pre, w_in, w_attn_up, ssm_a_re, ssm_a_im, ssm_log_dt, ssm_b_re, ssm_b_im, ssm_c_re,
               ssm_c_im, ssm_d, w_glu_v, w_glu_g, w_out, norm_mix_post, norm_ffn_pre, w_ffn_gate, w_ffn_up,
               w_ffn_down, norm_ffn_post)
    for layer in range(norm_mix_pre.shape[0]):
        x = _layer(x, *(p[layer] for p in stacked))
    return x
```

```python
import functools

import jax
import jax.numpy as jnp
import numpy as np
from jax import lax
from jax.experimental import pallas as pl
from jax.experimental.pallas import tpu as pltpu

F32 = jnp.float32
BF16 = jnp.bfloat16

EPS = 1e-6
HEAD_DIM = 128
HEADS_PER_GROUP = 4
ATTN_GROUPS = ((128, 1), (512, 4), (2048, 16))
N_GROUPS = len(ATTN_GROUPS)
N_HEADS = HEADS_PER_GROUP * N_GROUPS
GROUP_WIDTH = HEADS_PER_GROUP * HEAD_DIM
ATTN_BLK = 128
ATTN_TILE = 2048
SSM_GROUP = 16
SSM_STATE = 64
SSM_CHUNK = 8
LANES = 128
GROUPS_PER_LANE_BLOCK = LANES // SSM_GROUP
PROJ_TILE = 1024
ROW_ORDERS = tuple(dil for _, dil in ATTN_GROUPS) + (SSM_CHUNK,)
NEG = -1e30
LOG2E = 1.4426950408889634
VMEM_LIMIT = 56 * 1024 * 1024


def _params(*sem):
    return pltpu.CompilerParams(dimension_semantics=sem, vmem_limit_bytes=VMEM_LIMIT)


def _rms(x, gain):
    return x * lax.rsqrt(jnp.mean(x * x, axis=-1, keepdims=True) + EPS) * gain


def _inproj_kernel(x_ref, g_ref, w_ref, o_ref, h_ref, hn_ref, *, n_qkv, u_tiles):
    j = pl.program_id(1)
    slabs, tm, _ = hn_ref.shape

    @pl.when(j == 0)
    def _():
        hn = _rms(x_ref[...], g_ref[...])
        h_ref[0] = hn.astype(BF16)
        for s in range(slabs):
            hn_ref[s] = hn[:, s * LANES:(s + 1) * LANES]
        for v, dil in enumerate(ROW_ORDERS[1:], start=1):
            n = tm // dil
            for r in range(dil):
                for s in range(slabs):
                    h_ref[v, r * n:(r + 1) * n, s * LANES:(s + 1) * LANES] = (
                        hn_ref[s, pl.ds(r, n, stride=dil), :].astype(BF16))

    order = jnp.where(j < n_qkv, j % N_GROUPS, jnp.where(j < n_qkv + u_tiles, N_GROUPS, 0))
    o_ref[0] = jnp.dot(h_ref[order], w_ref[...], preferred_element_type=F32).astype(o_ref.dtype)


def _inproj(x, gain, w, ssm_width):
    t, d = x.shape
    tm, tn = PROJ_TILE, GROUP_WIDTH
    n = w.shape[1]
    assert t % tm == 0 and n % tn == 0 and ssm_width % tn == 0 and d % LANES == 0
    assert all(tm % (dil * 16) == 0 for dil in ROW_ORDERS)
    return pl.pallas_call(
        functools.partial(_inproj_kernel, n_qkv=3 * N_GROUPS, u_tiles=ssm_width // tn),
        grid=(t // tm, n // tn),
        in_specs=[
            pl.BlockSpec((tm, d), lambda i, j: (i, 0)),
            pl.BlockSpec((1, d), lambda i, j: (0, 0)),
            pl.BlockSpec((d, tn), lambda i, j: (0, j)),
        ],
        out_specs=pl.BlockSpec((1, tm, tn), lambda i, j: (i, 0, j)),
        out_shape=jax.ShapeDtypeStruct((t // tm, tm, n), BF16),
        scratch_shapes=[pltpu.VMEM((len(ROW_ORDERS), tm, d), BF16), pltpu.VMEM((d // LANES, tm, LANES), F32)],
        compiler_params=_params("parallel", "arbitrary"),
        name="inproj",
    )(x, gain.reshape(1, d), w)


def _attn_bias_table():
    qi = np.arange(ATTN_BLK)[:, None]
    kj = np.arange(ATTN_BLK)[None, :]
    table = np.full((N_GROUPS, 2, HEADS_PER_GROUP * ATTN_BLK, 2 * ATTN_BLK), NEG, np.float32)
    for g, (_, dil) in enumerate(ATTN_GROUPS):
        for h in range(HEADS_PER_GROUP):
            slope = 2.0 ** (-8.0 * (g * HEADS_PER_GROUP + h + 1) / N_HEADS) * dil * LOG2E
            rows = slice(h * ATTN_BLK, (h + 1) * ATTN_BLK)
            cur = np.where(kj <= qi, -slope * (qi - kj), NEG)
            prev = np.where(kj >= qi, -slope * (ATTN_BLK + qi - kj), NEG)
            table[g, :, rows, ATTN_BLK:] = cur
            table[g, 0, rows, :ATTN_BLK] = prev
    return table


def _attn_kernel(bias_ref, *refs):
    qkv_refs = refs[:3 * N_GROUPS]
    o_ref, out_scr, lse_scr = refs[3 * N_GROUPS:3 * N_GROUPS + 3]
    hist_refs = refs[3 * N_GROUPS + 3:]
    tile = pl.program_id(1)
    u = pl.program_id(2)
    units = ATTN_TILE // ATTN_BLK
    nt = (((1,), (1,)), ((), ()))

    @pl.when(jnp.logical_and(tile == 0, u == 0))
    def _():
        for hist in hist_refs:
            hist[...] = jnp.zeros_like(hist)

    for g, (_, dil) in enumerate(ATTN_GROUPS):
        hist = hist_refs[g]
        nb = u // dil
        r = u % dil
        first = jnp.logical_and(tile == 0, nb == 0).astype(jnp.int32)
        start = nb * (ATTN_BLK * dil) + r
        rows = pl.ds(start, ATTN_BLK) if dil == 1 else pl.ds(start, ATTN_BLK, stride=dil)
        q, k, v = (ref[...].reshape(ATTN_BLK, GROUP_WIDTH) for ref in qkv_refs[3 * g:3 * g + 3])
        prev = hist[r]
        scores = []
        for h in range(HEADS_PER_GROUP):
            cs = slice(h * HEAD_DIM, (h + 1) * HEAD_DIM)
            keys = jnp.concatenate([prev[:, cs], k[:, cs]], axis=0)
            scores.append(lax.dot_general(q[:, cs], keys, nt, preferred_element_type=F32))
        s = jnp.concatenate(scores, axis=0) * (HEAD_DIM ** -0.5 * LOG2E) + bias_ref[g, first]
        m = jnp.max(s, axis=1, keepdims=True)
        p = jnp.exp2(s - m)
        l = jnp.sum(p, axis=1, keepdims=True)
        p = p.astype(BF16)
        inv = 1.0 / l
        lse = m + jnp.log2(l)
        for h in range(HEADS_PER_GROUP):
            cs = slice(h * HEAD_DIM, (h + 1) * HEAD_DIM)
            hr = slice(h * ATTN_BLK, (h + 1) * ATTN_BLK)
            vals = jnp.concatenate([prev[:, GROUP_WIDTH:][:, cs], v[:, cs]], axis=0)
            o = jnp.dot(p[hr], vals, preferred_element_type=F32)
            out_scr[g, h, rows, :] = o * inv[hr]
            lse_scr[g, h, rows, :] = jnp.broadcast_to(lse[hr], (ATTN_BLK, HEAD_DIM))
        hist[r, :, :GROUP_WIDTH] = k
        hist[r, :, GROUP_WIDTH:] = v

    @pl.when(u == units - 1)
    def _():
        for h in range(HEADS_PER_GROUP):
            lses = [lse_scr[g, h] for g in range(N_GROUPS)]
            top = functools.reduce(jnp.maximum, lses)
            ws = [jnp.exp2(x - top) for x in lses]
            num = sum(w * out_scr[g, h] for g, w in enumerate(ws))
            o_ref[:, h * HEAD_DIM:(h + 1) * HEAD_DIM] = (num / sum(ws)).astype(o_ref.dtype)


def _attention(proj, bsz, seq):
    assert seq % ATTN_TILE == 0 and ATTN_TILE % PROJ_TILE == 0
    tiles = seq // ATTN_TILE
    units = ATTN_TILE // ATTN_BLK
    bias = jnp.asarray(_attn_bias_table())
    in_specs = [pl.BlockSpec(bias.shape, lambda b, tile, u: (0, 0, 0, 0))]
    hist = []
    for g, (window, dil) in enumerate(ATTN_GROUPS):
        assert window // dil == ATTN_BLK and units % dil == 0
        span = ATTN_BLK * dil
        spans_per_tile = ATTN_TILE // span
        if span <= PROJ_TILE:
            block = (None, ATTN_BLK, GROUP_WIDTH)

            def index(b, tile, u, *, col, dil=dil, spt=spans_per_tile, sppt=PROJ_TILE // span,
                      per_seq=seq // PROJ_TILE):
                sp = tile * spt + u // dil
                return b * per_seq + sp // sppt, (u % dil) * sppt + sp % sppt, col
        else:
            block = (span // PROJ_TILE, PROJ_TILE // dil, GROUP_WIDTH)

            def index(b, tile, u, *, col, dil=dil, spt=spans_per_tile, per_seq=seq // span):
                return b * per_seq + tile * spt + u // dil, u % dil, col

        for part in range(3):
            in_specs.append(pl.BlockSpec(block, functools.partial(index, col=part * N_GROUPS + g)))
        hist.append(pltpu.VMEM((dil, ATTN_BLK, 2 * GROUP_WIDTH), BF16))
    scratch = pltpu.VMEM((N_GROUPS, HEADS_PER_GROUP, ATTN_TILE, HEAD_DIM), F32)
    return pl.pallas_call(
        _attn_kernel,
        grid=(bsz, tiles, units),
        in_specs=in_specs,
        out_specs=pl.BlockSpec((ATTN_TILE, GROUP_WIDTH), lambda b, tile, u: (b * tiles + tile, 0)),
        out_shape=jax.ShapeDtypeStruct((bsz * seq, GROUP_WIDTH), BF16),
        scratch_shapes=[scratch, scratch] + hist,
        compiler_params=_params("arbitrary", "arbitrary", "arbitrary"),
        name="dilated_attention",
    )(bias, *([proj] * (3 * N_GROUPS)))


def _cmul(a, b):
    return a[0] * b[0] - a[1] * b[1], a[0] * b[1] + a[1] * b[0]


def _ssm_weights(a_re, a_im, log_dt, b_re, b_im, c_re, c_im):
    n_groups = a_re.shape[0]
    nblk = n_groups // GROUPS_PER_LANE_BLOCK
    gl = GROUPS_PER_LANE_BLOCK
    L = SSM_CHUNK
    a_re, a_im = a_re.astype(F32), a_im.astype(F32)
    dt = jnp.exp(log_dt.astype(F32))[:, None]
    steps = jnp.arange(L + 1, dtype=F32)[None, :, None]
    mag = jnp.exp((a_re * dt)[:, None, :] * steps)
    ang = (a_im * dt)[:, None, :] * steps
    powers = (mag * jnp.cos(ang), mag * jnp.sin(ang))
    lam_bar = (powers[0][:, 1], powers[1][:, 1])
    den = a_re * a_re + a_im * a_im
    num = (lam_bar[0] - 1.0, lam_bar[1])
    ratio = ((num[0] * a_re + num[1] * a_im) / den, (num[1] * a_re - num[0] * a_im) / den)
    b_bar = _cmul((ratio[0][..., None], ratio[1][..., None]), (b_re.astype(F32), b_im.astype(F32)))
    c_t = (c_re.astype(F32).transpose(0, 2, 1), c_im.astype(F32).transpose(0, 2, 1))
    eye = jnp.eye(gl, dtype=F32)

    def block_diag(m):
        rows, cols = m.shape[1:]
        m = m.reshape(nblk, gl, rows, 1, cols) * eye[None, :, None, :, None]
        return m.reshape(nblk, gl * rows, gl * cols)

    b_in = jnp.stack([block_diag(b.transpose(0, 2, 1)) for b in b_bar], axis=1)
    c_out = jnp.stack([block_diag(c) for c in c_t], axis=1)
    pw = jnp.stack(powers, axis=0).reshape(2, nblk, gl, L + 1, SSM_STATE)
    pw_row = pw.transpose(1, 0, 3, 2, 4).reshape(nblk, 2, L + 1, gl * SSM_STATE)
    pw_col = pw_row.transpose(0, 1, 3, 2)
    return b_in, c_out, pw_row, pw_col


def _ssm_kernel(*refs, tiles_per_seq):
    u_refs = refs[:SSM_CHUNK]
    (bin_ref, cout_ref, pwr_ref, pwc_ref, d_ref, y_ref,
     wt_scr, wb_scr, wc_scr, s_scr, xp_scr, carry_scr, y_scr) = refs[SSM_CHUNK:]
    L = SSM_CHUNK
    tc = s_scr.shape[0]
    half = s_scr.shape[1] // 2

    @pl.when(pl.program_id(1) == 0)
    def _():
        hi = lax.Precision.HIGHEST
        for j in range(L):
            rows = slice(j * LANES, (j + 1) * LANES)
            n = L - 1 - j
            pr = pwr_ref[0, 0, n:n + 1, :]
            pi = pwr_ref[0, 1, n:n + 1, :]
            qr = bin_ref[0, 0] * pr - bin_ref[0, 1] * pi
            qi = bin_ref[0, 0] * pi + bin_ref[0, 1] * pr
            wb_scr[rows, :half] = qr.astype(BF16)
            wb_scr[rows, half:] = qi.astype(BF16)
            kn = (jnp.dot(qr, cout_ref[0, 0], precision=hi, preferred_element_type=F32)
                  - jnp.dot(qi, cout_ref[0, 1], precision=hi, preferred_element_type=F32)).astype(BF16)
            for jj in range(L):
                ii = jj + n
                if ii < L:
                    wt_scr[jj * LANES:(jj + 1) * LANES, ii * LANES:(ii + 1) * LANES] = kn
                if j < jj:
                    wt_scr[jj * LANES:(jj + 1) * LANES, rows] = jnp.zeros((LANES, LANES), BF16)
            pr = pwc_ref[0, 0, :, j + 1:j + 2]
            pi = pwc_ref[0, 1, :, j + 1:j + 2]
            wc_scr[:half, rows] = (cout_ref[0, 0] * pr - cout_ref[0, 1] * pi).astype(BF16)
            wc_scr[half:, rows] = (-(cout_ref[0, 0] * pi + cout_ref[0, 1] * pr)).astype(BF16)

    @pl.when(pl.program_id(1) % tiles_per_seq == 0)
    def _():
        carry_scr[...] = jnp.zeros_like(carry_scr)

    us = [r[...].reshape(tc, LANES) for r in u_refs]
    ucat = jnp.concatenate(us, axis=1)
    s_scr[...] = jnp.dot(ucat, wb_scr[...], preferred_element_type=F32)
    ar = pwr_ref[0, 0, L:L + 1, :]
    ai = pwr_ref[0, 1, L:L + 1, :]

    def step(c, carry):
        xr, xi = carry
        xp_scr[pl.ds(c, 1), :half] = xr
        xp_scr[pl.ds(c, 1), half:] = xi
        s = s_scr[pl.ds(c, 1), :]
        return ar * xr - ai * xi + s[:, :half], ar * xi + ai * xr + s[:, half:]

    xr, xi = lax.fori_loop(0, tc, step, (carry_scr[:, :half], carry_scr[:, half:]))
    carry_scr[:, :half] = xr
    carry_scr[:, half:] = xi

    y = jnp.dot(ucat, wt_scr[...], preferred_element_type=F32)
    y = y + jnp.dot(xp_scr[...].astype(BF16), wc_scr[...], preferred_element_type=F32)
    for i in range(L):
        yi = y[:, i * LANES:(i + 1) * LANES] + d_ref[0] * us[i].astype(F32)
        y_scr[pl.ds(i, tc, stride=L), :] = jax.nn.gelu(yi)
    y_ref[...] = y_scr[...].astype(y_ref.dtype)


def _ssm(proj, u_col0, ssm_w, d_skip, bsz, seq, *, ptiles):
    nt, tm, _ = proj.shape
    L = SSM_CHUNK
    crows = tm // L
    nblk = ssm_w[0].shape[0]
    width = nblk * LANES
    assert crows == LANES and u_col0 % LANES == 0
    tc = ptiles * crows
    rows_per_seq = seq // L
    assert rows_per_seq % tc == 0 and nt % ptiles == 0
    u_specs = [
        pl.BlockSpec((ptiles, crows, LANES), functools.partial(lambda blk, i, j: (i, j, u_col0 // LANES + blk), j=j))
        for j in range(L)
    ]
    w_specs = [pl.BlockSpec((1,) + w.shape[1:], lambda blk, i: (blk, 0, 0, 0)) for w in ssm_w]
    wide = L * LANES
    states = 2 * GROUPS_PER_LANE_BLOCK * SSM_STATE
    return pl.pallas_call(
        functools.partial(_ssm_kernel, tiles_per_seq=rows_per_seq // tc),
        grid=(nblk, nt // ptiles),
        in_specs=u_specs + w_specs + [pl.BlockSpec((1, 1, LANES), lambda blk, i: (blk, 0, 0))],
        out_specs=pl.BlockSpec((tc * L, LANES), lambda blk, i: (i, blk)),
        out_shape=jax.ShapeDtypeStruct((nt * tm, width), BF16),
        scratch_shapes=[
            pltpu.VMEM((wide, wide), BF16),
            pltpu.VMEM((wide, states), BF16),
            pltpu.VMEM((states, wide), BF16),
            pltpu.VMEM((tc, states), F32),
            pltpu.VMEM((tc, states), F32),
            pltpu.VMEM((1, states), F32),
            pltpu.VMEM((tc * L, LANES), F32),
        ],
        compiler_params=_params("parallel", "arbitrary"),
        name="s5_chunked",
    )(*([proj] * L), *ssm_w, d_skip.astype(F32).reshape(nblk, 1, LANES))


def _merge_kernel(attn_ref, y_ref, ga_ref, gs_ref, wup_ref, wv_ref, wg_ref, o_ref):
    y = y_ref[...]
    attn_branch = jnp.dot(attn_ref[...], wup_ref[...].astype(BF16), preferred_element_type=F32)
    val = jnp.dot(y, wv_ref[...].astype(BF16), preferred_element_type=F32)
    gate = jnp.dot(y, wg_ref[...].astype(BF16), preferred_element_type=F32)
    ssm_branch = val * jax.nn.sigmoid(gate)
    merged = (jax.nn.sigmoid(ga_ref[...].astype(F32)) * attn_branch
              + jax.nn.sigmoid(gs_ref[...].astype(F32)) * ssm_branch)
    o_ref[...] = merged.astype(o_ref.dtype)


def _merge(attn, y, proj, gate_col0, w_up, w_v, w_g, *, tm, tn):
    t = attn.shape[0]
    n = w_up.shape[1]
    assert gate_col0 % tn == 0 and n % tn == 0
    return pl.pallas_call(
        _merge_kernel,
        grid=(t // tm, n // tn),
        in_specs=[
            pl.BlockSpec((tm, attn.shape[1]), lambda i, j: (i, 0)),
            pl.BlockSpec((tm, y.shape[1]), lambda i, j: (i, 0)),
            pl.BlockSpec((tm, tn), lambda i, j: (i, gate_col0 // tn + j)),
            pl.BlockSpec((tm, tn), lambda i, j: (i, (gate_col0 + n) // tn + j)),
            pl.BlockSpec((w_up.shape[0], tn), lambda i, j: (0, j)),
            pl.BlockSpec((w_v.shape[0], tn), lambda i, j: (0, j)),
            pl.BlockSpec((w_g.shape[0], tn), lambda i, j: (0, j)),
        ],
        out_specs=pl.BlockSpec((tm, tn), lambda i, j: (i, j)),
        out_shape=jax.ShapeDtypeStruct((t, n), BF16),
        compiler_params=_params("parallel", "arbitrary"),
        name="gated_merge",
    )(attn, y, proj, proj, w_up, w_v, w_g)


def _outproj_kernel(m_ref, w_ref, x_ref, g_ref, o_ref):
    z = jnp.dot(m_ref[...], w_ref[...], preferred_element_type=F32)
    o_ref[...] = x_ref[...] + _rms(z, g_ref[...])


def _outproj(merged, w, x, gain, *, tm):
    t, d = x.shape
    return pl.pallas_call(
        _outproj_kernel,
        grid=(t // tm,),
        in_specs=[
            pl.BlockSpec((tm, merged.shape[1]), lambda i: (i, 0)),
            pl.BlockSpec(w.shape, lambda i: (0, 0)),
            pl.BlockSpec((tm, d), lambda i: (i, 0)),
            pl.BlockSpec((1, d), lambda i: (0, 0)),
        ],
        out_specs=pl.BlockSpec((tm, d), lambda i: (i, 0)),
        out_shape=jax.ShapeDtypeStruct((t, d), F32),
        compiler_params=_params("parallel"),
        name="outproj_norm_residual",
    )(merged, w, x, gain.reshape(1, d))


def _ffn_kernel(x_ref, gpre_ref, gpost_ref, wg_ref, wu_ref, wd_ref, o_ref, h_ref, acc_ref):
    k = pl.program_id(1)

    @pl.when(k == 0)
    def _():
        h_ref[...] = _rms(x_ref[...], gpre_ref[...]).astype(BF16)
        acc_ref[...] = jnp.zeros_like(acc_ref)

    h = h_ref[...]
    gate = jnp.dot(h, wg_ref[...], preferred_element_type=F32)
    up = jnp.dot(h, wu_ref[...], preferred_element_type=F32)
    f = (jax.nn.silu(gate) * up).astype(BF16)
    acc_ref[...] += jnp.dot(f, wd_ref[...], preferred_element_type=F32)

    @pl.when(k == pl.num_programs(1) - 1)
    def _():
        o_ref[...] = x_ref[...] + _rms(acc_ref[...], gpost_ref[...])


def _ffn(x, gain_pre, gain_post, w_gate, w_up, w_down, *, tm, tf):
    t, d = x.shape
    dff = w_gate.shape[1]
    return pl.pallas_call(
        _ffn_kernel,
        grid=(t // tm, dff // tf),
        in_specs=[
            pl.BlockSpec((tm, d), lambda i, k: (i, 0)),
            pl.BlockSpec((1, d), lambda i, k: (0, 0)),
            pl.BlockSpec((1, d), lambda i, k: (0, 0)),
            pl.BlockSpec((d, tf), lambda i, k: (0, k)),
            pl.BlockSpec((d, tf), lambda i, k: (0, k)),
            pl.BlockSpec((tf, d), lambda i, k: (k, 0)),
        ],
        out_specs=pl.BlockSpec((tm, d), lambda i, k: (i, 0)),
        out_shape=jax.ShapeDtypeStruct((t, d), F32),
        scratch_shapes=[pltpu.VMEM((tm, d), BF16), pltpu.VMEM((tm, d), F32)],
        compiler_params=_params("parallel", "arbitrary"),
        name="swiglu_ffn",
    )(x, gain_pre.reshape(1, d), gain_post.reshape(1, d), w_gate, w_up, w_down)


def _layer(x, norm_mix_pre, w_in, w_attn_up, ssm_a_re, ssm_a_im, ssm_log_dt, ssm_b_re, ssm_b_im,
           ssm_c_re, ssm_c_im, ssm_d, w_glu_v, w_glu_g, w_out, norm_mix_post, norm_ffn_pre,
           w_ffn_gate, w_ffn_up, w_ffn_down, norm_ffn_post):
    bsz, seq, d = x.shape
    t = bsz * seq
    ssm_width = ssm_d.shape[0]
    u_col0 = 3 * N_HEADS * HEAD_DIM
    assert w_in.shape[1] == u_col0 + ssm_width + 2 * d

    x2 = x.reshape(t, d)
    proj = _inproj(x2, norm_mix_pre, w_in.astype(BF16), ssm_width)
    attn = _attention(proj, bsz, seq)
    ssm_w = _ssm_weights(ssm_a_re, ssm_a_im, ssm_log_dt, ssm_b_re, ssm_b_im, ssm_c_re, ssm_c_im)
    y = _ssm(proj, u_col0, ssm_w, ssm_d, bsz, seq, ptiles=4)
    merged = _merge(attn, y, proj.reshape(t, -1), u_col0 + ssm_width, w_attn_up, w_glu_v, w_glu_g,
                    tm=1024, tn=512)
    x1 = _outproj(merged, w_out.astype(BF16), x2, norm_mix_post, tm=512)
    out = _ffn(x1, norm_ffn_pre, norm_ffn_post, w_ffn_gate.astype(BF16), w_ffn_up.astype(BF16),
               w_ffn_down.astype(BF16), tm=512, tf=512)
    return out.reshape(bsz, seq, d)


def kernel(x, norm_mix_pre, w_in, w_attn_up, ssm_a_re, ssm_a_im, ssm_log_dt, ssm_b_re, ssm_b_im, ssm_c_re, ssm_c_im, ssm_d, w_glu_v, w_glu_g, w_out, norm_mix_post, norm_ffn_pre, w_ffn_gate, w_ffn_up, w_ffn_down, norm_ffn_post):
    stacked = (norm_mix_pre, w_in, w_attn_up, ssm_a_re, ssm_a_im, ssm_log_dt, ssm_b_re, ssm_b_im, ssm_c_re,
               ssm_c_im, ssm_d, w_glu_v, w_glu_g, w_out, norm_mix_post, norm_ffn_pre, w_ffn_gate, w_ffn_up,
               w_ffn_down, norm_ffn_post)
    for layer in range(norm_mix_pre.shape[0]):
        x = _layer(x, *(p[layer] for p in stacked))
    return x
```

```python
import functools

import jax
import jax.numpy as jnp
import numpy as np
from jax import lax
from jax.experimental import pallas as pl
from jax.experimental.pallas import tpu as pltpu

F32 = jnp.float32
BF16 = jnp.bfloat16

EPS = 1e-6
HEAD_DIM = 128
HEADS_PER_GROUP = 4
ATTN_GROUPS = ((128, 1), (512, 4), (2048, 16))
N_GROUPS = len(ATTN_GROUPS)
N_HEADS = HEADS_PER_GROUP * N_GROUPS
GROUP_WIDTH = HEADS_PER_GROUP * HEAD_DIM
ATTN_BLK = 128
ATTN_TILE = 2048
SSM_GROUP = 16
SSM_STATE = 64
SSM_CHUNK = 8
LANES = 128
GROUPS_PER_LANE_BLOCK = LANES // SSM_GROUP
PROJ_TILE = 1024
ROW_ORDERS = tuple(dil for _, dil in ATTN_GROUPS) + (SSM_CHUNK,)
PERM_BASE = 4
NEG = -1e30
LOG2E = 1.4426950408889634
VMEM_LIMIT = 56 * 1024 * 1024


def _params(*sem):
    return pltpu.CompilerParams(dimension_semantics=sem, vmem_limit_bytes=VMEM_LIMIT)


def _rms(x, gain):
    return x * lax.rsqrt(jnp.mean(x * x, axis=-1, keepdims=True) + EPS) * gain


def _residue_block(r, dil):
    if dil <= PERM_BASE:
        return r
    return (r % PERM_BASE) * (dil // PERM_BASE) + r // PERM_BASE


def _inproj_kernel(x_ref, g_ref, w_ref, o_ref, h_ref, hn_ref, hb_ref, *, n_qkv, u_tiles):
    j = pl.program_id(1)
    slabs, tm, _ = hn_ref.shape

    @pl.when(j == 0)
    def _():
        x = x_ref[...]
        inv = lax.rsqrt(jnp.mean(x * x, axis=-1, keepdims=True) + EPS)
        q = tm // PERM_BASE
        for c in range(x_ref.shape[1] // LANES):
            cols = slice(c * LANES, (c + 1) * LANES)
            s = c % slabs
            hn = x_ref[:, cols] * inv * g_ref[:, cols]
            h_ref[0, :, cols] = hn.astype(BF16)
            hn_ref[s] = hn
            for b in range(PERM_BASE):
                part = hn_ref[s, pl.ds(b, q, stride=PERM_BASE), :]
                hb_ref[s, b * q:(b + 1) * q, :] = part
                for v, dil in enumerate(ROW_ORDERS):
                    if dil == PERM_BASE:
                        h_ref[v, b * q:(b + 1) * q, cols] = part.astype(BF16)
            for v, dil in enumerate(ROW_ORDERS):
                if dil > PERM_BASE:
                    k, n = dil // PERM_BASE, tm // dil
                    for b in range(PERM_BASE):
                        for a in range(k):
                            blk = b * k + a
                            h_ref[v, blk * n:(blk + 1) * n, cols] = (
                                hb_ref[s, pl.ds(b * q + a, n, stride=k), :].astype(BF16))

    order = jnp.where(j < n_qkv, j % N_GROUPS, jnp.where(j < n_qkv + u_tiles, N_GROUPS, 0))
    o_ref[0] = jnp.dot(h_ref[order], w_ref[...], preferred_element_type=F32).astype(o_ref.dtype)


def _inproj(x, gain, w, ssm_width):
    t, d = x.shape
    tm, tn = PROJ_TILE, GROUP_WIDTH
    n = w.shape[1]
    assert t % tm == 0 and n % tn == 0 and ssm_width % tn == 0 and d % LANES == 0
    assert ROW_ORDERS[0] == 1 and all(tm % (dil * 16) == 0 for dil in ROW_ORDERS)
    assert all(dil in (1, PERM_BASE) or (dil % PERM_BASE == 0 and dil // PERM_BASE <= PERM_BASE) for dil in ROW_ORDERS)
    slabs = 8
    return pl.pallas_call(
        functools.partial(_inproj_kernel, n_qkv=3 * N_GROUPS, u_tiles=ssm_width // tn),
        grid=(t // tm, n // tn),
        in_specs=[
            pl.BlockSpec((tm, d), lambda i, j: (i, 0)),
            pl.BlockSpec((1, d), lambda i, j: (0, 0)),
            pl.BlockSpec((d, tn), lambda i, j: (0, j)),
        ],
        out_specs=pl.BlockSpec((1, tm, tn), lambda i, j: (i, 0, j)),
        out_shape=jax.ShapeDtypeStruct((t // tm, tm, n), BF16),
        scratch_shapes=[
            pltpu.VMEM((len(ROW_ORDERS), tm, d), BF16),
            pltpu.VMEM((slabs, tm, LANES), F32),
            pltpu.VMEM((slabs, tm, LANES), F32),
        ],
        compiler_params=_params("parallel", "arbitrary"),
        name="inproj",
    )(x, gain.reshape(1, d), w)


def _attn_bias_table():
    qi = np.arange(ATTN_BLK)[:, None]
    kj = np.arange(ATTN_BLK)[None, :]
    table = np.full((N_GROUPS, 2, HEADS_PER_GROUP * ATTN_BLK, 2 * ATTN_BLK), NEG, np.float32)
    for g, (_, dil) in enumerate(ATTN_GROUPS):
        for h in range(HEADS_PER_GROUP):
            slope = 2.0 ** (-8.0 * (g * HEADS_PER_GROUP + h + 1) / N_HEADS) * dil * LOG2E
            rows = slice(h * ATTN_BLK, (h + 1) * ATTN_BLK)
            cur = np.where(kj <= qi, -slope * (qi - kj), NEG)
            prev = np.where(kj >= qi, -slope * (ATTN_BLK + qi - kj), NEG)
            table[g, :, rows, ATTN_BLK:] = cur
            table[g, 0, rows, :ATTN_BLK] = prev
    return table


def _attn_kernel(bias_ref, *refs):
    qkv_refs = refs[:3 * N_GROUPS]
    o_ref, out_scr, lse_scr = refs[3 * N_GROUPS:3 * N_GROUPS + 3]
    hist_refs = refs[3 * N_GROUPS + 3:]
    tile = pl.program_id(1)
    u = pl.program_id(2)
    units = ATTN_TILE // ATTN_BLK
    nt = (((1,), (1,)), ((), ()))

    @pl.when(jnp.logical_and(tile == 0, u == 0))
    def _():
        for hist in hist_refs:
            hist[...] = jnp.zeros_like(hist)

    for g, (_, dil) in enumerate(ATTN_GROUPS):
        hist = hist_refs[g]
        nb = u // dil
        r = u % dil
        first = jnp.logical_and(tile == 0, nb == 0).astype(jnp.int32)
        start = nb * (ATTN_BLK * dil) + r
        rows = pl.ds(start, ATTN_BLK) if dil == 1 else pl.ds(start, ATTN_BLK, stride=dil)
        q, k, v = (ref[...].reshape(ATTN_BLK, GROUP_WIDTH) for ref in qkv_refs[3 * g:3 * g + 3])
        prev = hist[r]
        scores = []
        for h in range(HEADS_PER_GROUP):
            cs = slice(h * HEAD_DIM, (h + 1) * HEAD_DIM)
            keys = jnp.concatenate([prev[:, cs], k[:, cs]], axis=0)
            scores.append(lax.dot_general(q[:, cs], keys, nt, preferred_element_type=F32))
        s = jnp.concatenate(scores, axis=0) * (HEAD_DIM ** -0.5 * LOG2E) + bias_ref[g, first]
        m = jnp.max(s, axis=1, keepdims=True)
        p = jnp.exp2(s - m)
        l = jnp.sum(p, axis=1, keepdims=True)
        p = p.astype(BF16)
        inv = 1.0 / l
        lse = m + jnp.log2(l)
        for h in range(HEADS_PER_GROUP):
            cs = slice(h * HEAD_DIM, (h + 1) * HEAD_DIM)
            hr = slice(h * ATTN_BLK, (h + 1) * ATTN_BLK)
            vals = jnp.concatenate([prev[:, GROUP_WIDTH:][:, cs], v[:, cs]], axis=0)
            o = jnp.dot(p[hr], vals, preferred_element_type=F32)
            out_scr[g, h, rows, :] = o * inv[hr]
            lse_scr[g, h, rows, :] = jnp.broadcast_to(lse[hr], (ATTN_BLK, HEAD_DIM))
        hist[r, :, :GROUP_WIDTH] = k
        hist[r, :, GROUP_WIDTH:] = v

    @pl.when(u == units - 1)
    def _():
        for h in range(HEADS_PER_GROUP):
            lses = [lse_scr[g, h] for g in range(N_GROUPS)]
            top = functools.reduce(jnp.maximum, lses)
            ws = [jnp.exp2(x - top) for x in lses]
            num = sum(w * out_scr[g, h] for g, w in enumerate(ws))
            o_ref[:, h * HEAD_DIM:(h + 1) * HEAD_DIM] = (num / sum(ws)).astype(o_ref.dtype)


def _attention(proj, bsz, seq):
    assert seq % ATTN_TILE == 0 and ATTN_TILE % PROJ_TILE == 0
    tiles = seq // ATTN_TILE
    units = ATTN_TILE // ATTN_BLK
    bias = jnp.asarray(_attn_bias_table())
    in_specs = [pl.BlockSpec(bias.shape, lambda b, tile, u: (0, 0, 0, 0))]
    hist = []
    for g, (window, dil) in enumerate(ATTN_GROUPS):
        assert window // dil == ATTN_BLK and units % dil == 0
        span = ATTN_BLK * dil
        spans_per_tile = ATTN_TILE // span
        if span <= PROJ_TILE:
            block = (None, ATTN_BLK, GROUP_WIDTH)

            def index(b, tile, u, *, col, dil=dil, spt=spans_per_tile, sppt=PROJ_TILE // span,
                      per_seq=seq // PROJ_TILE):
                sp = tile * spt + u // dil
                return b * per_seq + sp // sppt, _residue_block(u % dil, dil) * sppt + sp % sppt, col
        else:
            block = (span // PROJ_TILE, PROJ_TILE // dil, GROUP_WIDTH)

            def index(b, tile, u, *, col, dil=dil, spt=spans_per_tile, per_seq=seq // span):
                return b * per_seq + tile * spt + u // dil, _residue_block(u % dil, dil), col

        for part in range(3):
            in_specs.append(pl.BlockSpec(block, functools.partial(index, col=part * N_GROUPS + g)))
        hist.append(pltpu.VMEM((dil, ATTN_BLK, 2 * GROUP_WIDTH), BF16))
    scratch = pltpu.VMEM((N_GROUPS, HEADS_PER_GROUP, ATTN_TILE, HEAD_DIM), F32)
    return pl.pallas_call(
        _attn_kernel,
        grid=(bsz, tiles, units),
        in_specs=in_specs,
        out_specs=pl.BlockSpec((ATTN_TILE, GROUP_WIDTH), lambda b, tile, u: (b * tiles + tile, 0)),
        out_shape=jax.ShapeDtypeStruct((bsz * seq, GROUP_WIDTH), BF16),
        scratch_shapes=[scratch, scratch] + hist,
        compiler_params=_params("arbitrary", "arbitrary", "arbitrary"),
        name="dilated_attention",
    )(bias, *([proj] * (3 * N_GROUPS)))


def _cmul(a, b):
    return a[0] * b[0] - a[1] * b[1], a[0] * b[1] + a[1] * b[0]


def _ssm_weights(a_re, a_im, log_dt, b_re, b_im, c_re, c_im):
    n_groups = a_re.shape[0]
    nblk = n_groups // GROUPS_PER_LANE_BLOCK
    gl = GROUPS_PER_LANE_BLOCK
    L = SSM_CHUNK
    a_re, a_im = a_re.astype(F32), a_im.astype(F32)
    dt = jnp.exp(log_dt.astype(F32))[:, None]
    steps = jnp.arange(L + 1, dtype=F32)[None, :, None]
    mag = jnp.exp((a_re * dt)[:, None, :] * steps)
    ang = (a_im * dt)[:, None, :] * steps
    powers = (mag * jnp.cos(ang), mag * jnp.sin(ang))
    lam_bar = (powers[0][:, 1], powers[1][:, 1])
    den = a_re * a_re + a_im * a_im
    num = (lam_bar[0] - 1.0, lam_bar[1])
    ratio = ((num[0] * a_re + num[1] * a_im) / den, (num[1] * a_re - num[0] * a_im) / den)
    b_bar = _cmul((ratio[0][..., None], ratio[1][..., None]), (b_re.astype(F32), b_im.astype(F32)))
    c_t = (c_re.astype(F32).transpose(0, 2, 1), c_im.astype(F32).transpose(0, 2, 1))
    eye = jnp.eye(gl, dtype=F32)

    def block_diag(m):
        rows, cols = m.shape[1:]
        m = m.reshape(nblk, gl, rows, 1, cols) * eye[None, :, None, :, None]
        return m.reshape(nblk, gl * rows, gl * cols)

    b_in = jnp.stack([block_diag(b.transpose(0, 2, 1)) for b in b_bar], axis=1)
    c_out = jnp.stack([block_diag(c) for c in c_t], axis=1)
    pw = jnp.stack(powers, axis=0).reshape(2, nblk, gl, L + 1, SSM_STATE)
    pw_row = pw.transpose(1, 0, 3, 2, 4).reshape(nblk, 2, L + 1, gl * SSM_STATE)
    pw_col = pw_row.transpose(0, 1, 3, 2)
    return b_in, c_out, pw_row, pw_col


def _ssm_kernel(*refs, tiles_per_seq):
    u_refs = refs[:SSM_CHUNK]
    (bin_ref, cout_ref, pwr_ref, pwc_ref, d_ref, y_ref,
     wt_scr, wb_scr, wc_scr, s_scr, xp_scr, carry_scr, y_scr) = refs[SSM_CHUNK:]
    L = SSM_CHUNK
    tc = s_scr.shape[0]
    half = s_scr.shape[1] // 2

    @pl.when(pl.program_id(1) == 0)
    def _():
        hi = lax.Precision.HIGHEST
        for j in range(L):
            rows = slice(j * LANES, (j + 1) * LANES)
            n = L - 1 - j
            pr = pwr_ref[0, 0, n:n + 1, :]
            pi = pwr_ref[0, 1, n:n + 1, :]
            qr = bin_ref[0, 0] * pr - bin_ref[0, 1] * pi
            qi = bin_ref[0, 0] * pi + bin_ref[0, 1] * pr
            wb_scr[rows, :half] = qr.astype(BF16)
            wb_scr[rows, half:] = qi.astype(BF16)
            kn = (jnp.dot(qr, cout_ref[0, 0], precision=hi, preferred_element_type=F32)
                  - jnp.dot(qi, cout_ref[0, 1], precision=hi, preferred_element_type=F32)).astype(BF16)
            for jj in range(L):
                ii = jj + n
                if ii < L:
                    wt_scr[jj * LANES:(jj + 1) * LANES, ii * LANES:(ii + 1) * LANES] = kn
                if j < jj:
                    wt_scr[jj * LANES:(jj + 1) * LANES, rows] = jnp.zeros((LANES, LANES), BF16)
            pr = pwc_ref[0, 0, :, j + 1:j + 2]
            pi = pwc_ref[0, 1, :, j + 1:j + 2]
            wc_scr[:half, rows] = (cout_ref[0, 0] * pr - cout_ref[0, 1] * pi).astype(BF16)
            wc_scr[half:, rows] = (-(cout_ref[0, 0] * pi + cout_ref[0, 1] * pr)).astype(BF16)

    @pl.when(pl.program_id(1) % tiles_per_seq == 0)
    def _():
        carry_scr[...] = jnp.zeros_like(carry_scr)

    us = [r[...].reshape(tc, LANES) for r in u_refs]
    ucat = jnp.concatenate(us, axis=1)
    s_scr[...] = jnp.dot(ucat, wb_scr[...], preferred_element_type=F32)
    ar = pwr_ref[0, 0, L:L + 1, :]
    ai = pwr_ref[0, 1, L:L + 1, :]

    def step(c, carry):
        xr, xi = carry
        xp_scr[pl.ds(c, 1), :half] = xr
        xp_scr[pl.ds(c, 1), half:] = xi
        s = s_scr[pl.ds(c, 1), :]
        return ar * xr - ai * xi + s[:, :half], ar * xi + ai * xr + s[:, half:]

    xr, xi = lax.fori_loop(0, tc, step, (carry_scr[:, :half], carry_scr[:, half:]))
    carry_scr[:, :half] = xr
    carry_scr[:, half:] = xi

    y = jnp.dot(ucat, wt_scr[...], preferred_element_type=F32)
    y = y + jnp.dot(xp_scr[...].astype(BF16), wc_scr[...], preferred_element_type=F32)
    for i in range(L):
        yi = y[:, i * LANES:(i + 1) * LANES] + d_ref[0] * us[i].astype(F32)
        y_scr[pl.ds(i, tc, stride=L), :] = jax.nn.gelu(yi)
    y_ref[...] = y_scr[...].astype(y_ref.dtype)


def _ssm(proj, u_col0, ssm_w, d_skip, bsz, seq, *, ptiles):
    nt, tm, _ = proj.shape
    L = SSM_CHUNK
    crows = tm // L
    nblk = ssm_w[0].shape[0]
    width = nblk * LANES
    assert crows == LANES and u_col0 % LANES == 0
    tc = ptiles * crows
    rows_per_seq = seq // L
    assert rows_per_seq % tc == 0 and nt % ptiles == 0
    u_specs = [
        pl.BlockSpec((ptiles, crows, LANES), functools.partial(lambda blk, i, j: (i, j, u_col0 // LANES + blk), j=_residue_block(j, L)))
        for j in range(L)
    ]
    w_specs = [pl.BlockSpec((1,) + w.shape[1:], lambda blk, i: (blk, 0, 0, 0)) for w in ssm_w]
    wide = L * LANES
    states = 2 * GROUPS_PER_LANE_BLOCK * SSM_STATE
    return pl.pallas_call(
        functools.partial(_ssm_kernel, tiles_per_seq=rows_per_seq // tc),
        grid=(nblk, nt // ptiles),
        in_specs=u_specs + w_specs + [pl.BlockSpec((1, 1, LANES), lambda blk, i: (blk, 0, 0))],
        out_specs=pl.BlockSpec((tc * L, LANES), lambda blk, i: (i, blk)),
        out_shape=jax.ShapeDtypeStruct((nt * tm, width), BF16),
        scratch_shapes=[
            pltpu.VMEM((wide, wide), BF16),
            pltpu.VMEM((wide, states), BF16),
            pltpu.VMEM((states, wide), BF16),
            pltpu.VMEM((tc, states), F32),
            pltpu.VMEM((tc, states), F32),
            pltpu.VMEM((1, states), F32),
            pltpu.VMEM((tc * L, LANES), F32),
        ],
        compiler_params=_params("parallel", "arbitrary"),
        name="s5_chunked",
    )(*([proj] * L), *ssm_w, d_skip.astype(F32).reshape(nblk, 1, LANES))


def _merge_kernel(attn_ref, y_ref, ga_ref, gs_ref, wup_ref, wv_ref, wg_ref, o_ref):
    y = y_ref[...]
    attn_branch = jnp.dot(attn_ref[...], wup_ref[...].astype(BF16), preferred_element_type=F32)
    val = jnp.dot(y, wv_ref[...].astype(BF16), preferred_element_type=F32)
    gate = jnp.dot(y, wg_ref[...].astype(BF16), preferred_element_type=F32)
    ssm_branch = val * jax.nn.sigmoid(gate)
    merged = (jax.nn.sigmoid(ga_ref[...].astype(F32)) * attn_branch
              + jax.nn.sigmoid(gs_ref[...].astype(F32)) * ssm_branch)
    o_ref[...] = merged.astype(o_ref.dtype)


def _merge(attn, y, proj, gate_col0, w_up, w_v, w_g, *, tm, tn):
    t = attn.shape[0]
    n = w_up.shape[1]
    assert gate_col0 % tn == 0 and n % tn == 0
    return pl.pallas_call(
        _merge_kernel,
        grid=(t // tm, n // tn),
        in_specs=[
            pl.BlockSpec((tm, attn.shape[1]), lambda i, j: (i, 0)),
            pl.BlockSpec((tm, y.shape[1]), lambda i, j: (i, 0)),
            pl.BlockSpec((tm, tn), lambda i, j: (i, gate_col0 // tn + j)),
            pl.BlockSpec((tm, tn), lambda i, j: (i, (gate_col0 + n) // tn + j)),
            pl.BlockSpec((w_up.shape[0], tn), lambda i, j: (0, j)),
            pl.BlockSpec((w_v.shape[0], tn), lambda i, j: (0, j)),
            pl.BlockSpec((w_g.shape[0], tn), lambda i, j: (0, j)),
        ],
        out_specs=pl.BlockSpec((tm, tn), lambda i, j: (i, j)),
        out_shape=jax.ShapeDtypeStruct((t, n), BF16),
        compiler_params=_params("parallel", "arbitrary"),
        name="gated_merge",
    )(attn, y, proj, proj, w_up, w_v, w_g)


def _outproj_kernel(m_ref, w_ref, x_ref, g_ref, o_ref):
    z = jnp.dot(m_ref[...], w_ref[...], preferred_element_type=F32)
    o_ref[...] = x_ref[...] + _rms(z, g_ref[...])


def _outproj(merged, w, x, gain, *, tm):
    t, d = x.shape
    return pl.pallas_call(
        _outproj_kernel,
        grid=(t // tm,),
        in_specs=[
            pl.BlockSpec((tm, merged.shape[1]), lambda i: (i, 0)),
            pl.BlockSpec(w.shape, lambda i: (0, 0)),
            pl.BlockSpec((tm, d), lambda i: (i, 0)),
            pl.BlockSpec((1, d), lambda i: (0, 0)),
        ],
        out_specs=pl.BlockSpec((tm, d), lambda i: (i, 0)),
        out_shape=jax.ShapeDtypeStruct((t, d), F32),
        compiler_params=_params("parallel"),
        name="outproj_norm_residual",
    )(merged, w, x, gain.reshape(1, d))


def _ffn_kernel(x_ref, gpre_ref, gpost_ref, wg_ref, wu_ref, wd_ref, o_ref, h_ref, acc_ref):
    k = pl.program_id(1)

    @pl.when(k == 0)
    def _():
        h_ref[...] = _rms(x_ref[...], gpre_ref[...]).astype(BF16)
        acc_ref[...] = jnp.zeros_like(acc_ref)

    h = h_ref[...]
    gate = jnp.dot(h, wg_ref[...], preferred_element_type=F32)
    up = jnp.dot(h, wu_ref[...], preferred_element_type=F32)
    f = (jax.nn.silu(gate) * up).astype(BF16)
    acc_ref[...] += jnp.dot(f, wd_ref[...], preferred_element_type=F32)

    @pl.when(k == pl.num_programs(1) - 1)
    def _():
        o_ref[...] = x_ref[...] + _rms(acc_ref[...], gpost_ref[...])


def _ffn(x, gain_pre, gain_post, w_gate, w_up, w_down, *, tm, tf):
    t, d = x.shape
    dff = w_gate.shape[1]
    return pl.pallas_call(
        _ffn_kernel,
        grid=(t // tm, dff // tf),
        in_specs=[
            pl.BlockSpec((tm, d), lambda i, k: (i, 0)),
            pl.BlockSpec((1, d), lambda i, k: (0, 0)),
            pl.BlockSpec((1, d), lambda i, k: (0, 0)),
            pl.BlockSpec((d, tf), lambda i, k: (0, k)),
            pl.BlockSpec((d, tf), lambda i, k: (0, k)),
            pl.BlockSpec((tf, d), lambda i, k: (k, 0)),
        ],
        out_specs=pl.BlockSpec((tm, d), lambda i, k: (i, 0)),
        out_shape=jax.ShapeDtypeStruct((t, d), F32),
        scratch_shapes=[pltpu.VMEM((tm, d), BF16), pltpu.VMEM((tm, d), F32)],
        compiler_params=_params("parallel", "arbitrary"),
        name="swiglu_ffn",
    )(x, gain_pre.reshape(1, d), gain_post.reshape(1, d), w_gate, w_up, w_down)


def _layer(x, norm_mix_pre, w_in, w_attn_up, ssm_a_re, ssm_a_im, ssm_log_dt, ssm_b_re, ssm_b_im,
           ssm_c_re, ssm_c_im, ssm_d, w_glu_v, w_glu_g, w_out, norm_mix_post, norm_ffn_pre,
           w_ffn_gate, w_ffn_up, w_ffn_down, norm_ffn_post):
    bsz, seq, d = x.shape
    t = bsz * seq
    ssm_width = ssm_d.shape[0]
    u_col0 = 3 * N_HEADS * HEAD_DIM
    assert w_in.shape[1] == u_col0 + ssm_width + 2 * d

    x2 = x.reshape(t, d)
    proj = _inproj(x2, norm_mix_pre, w_in.astype(BF16), ssm_width)
    attn = _attention(proj, bsz, seq)
    ssm_w = _ssm_weights(ssm_a_re, ssm_a_im, ssm_log_dt, ssm_b_re, ssm_b_im, ssm_c_re, ssm_c_im)
    y = _ssm(proj, u_col0, ssm_w, ssm_d, bsz, seq, ptiles=4)
    merged = _merge(attn, y, proj.reshape(t, -1), u_col0 + ssm_width, w_attn_up, w_glu_v, w_glu_g,
                    tm=1024, tn=512)
    x1 = _outproj(merged, w_out.astype(BF16), x2, norm_mix_post, tm=512)
    out = _ffn(x1, norm_ffn_pre, norm_ffn_post, w_ffn_gate.astype(BF16), w_ffn_up.astype(BF16),
               w_ffn_down.astype(BF16), tm=512, tf=512)
    return out.reshape(bsz, seq, d)


def kernel(x, norm_mix_pre, w_in, w_attn_up, ssm_a_re, ssm_a_im, ssm_log_dt, ssm_b_re, ssm_b_im, ssm_c_re, ssm_c_im, ssm_d, w_glu_v, w_glu_g, w_out, norm_mix_post, norm_ffn_pre, w_ffn_gate, w_ffn_up, w_ffn_down, norm_ffn_post):
    stacked = (norm_mix_pre, w_in, w_attn_up, ssm_a_re, ssm_a_im, ssm_log_dt, ssm_b_re, ssm_b_im, ssm_c_re,
               ssm_c_im, ssm_d, w_glu_v, w_glu_g, w_out, norm_mix_post, norm_ffn_pre, w_ffn_gate, w_ffn_up,
               w_ffn_down, norm_ffn_post)
    for layer in range(norm_mix_pre.shape[0]):
        x = _layer(x, *(p[layer] for p in stacked))
    return x
```

```python
import functools

import jax
import jax.numpy as jnp
import numpy as np
from jax import lax
from jax.experimental import pallas as pl
from jax.experimental.pallas import tpu as pltpu

F32 = jnp.float32
BF16 = jnp.bfloat16

EPS = 1e-6
HEAD_DIM = 128
HEADS_PER_GROUP = 4
ATTN_GROUPS = ((128, 1), (512, 4), (2048, 16))
N_GROUPS = len(ATTN_GROUPS)
N_HEADS = HEADS_PER_GROUP * N_GROUPS
GROUP_WIDTH = HEADS_PER_GROUP * HEAD_DIM
ATTN_BLK = 128
ATTN_TILE = 2048
SSM_GROUP = 16
SSM_STATE = 64
SSM_CHUNK = 8
LANES = 128
GROUPS_PER_LANE_BLOCK = LANES // SSM_GROUP
PROJ_TILE = 1024
ROW_ORDERS = tuple(dil for _, dil in ATTN_GROUPS) + (SSM_CHUNK,)
PERM_BASE = 4
NEG = -1e30
LOG2E = 1.4426950408889634
VMEM_LIMIT = 56 * 1024 * 1024


def _params(*sem):
    return pltpu.CompilerParams(dimension_semantics=sem, vmem_limit_bytes=VMEM_LIMIT)


def _rms(x, gain):
    return x * lax.rsqrt(jnp.mean(x * x, axis=-1, keepdims=True) + EPS) * gain


def _residue_block(r, dil):
    if dil <= PERM_BASE:
        return r
    return (r % PERM_BASE) * (dil // PERM_BASE) + r // PERM_BASE


def _inproj_kernel(x_ref, g_ref, w_ref, o_ref, h_ref, hn_ref, hb_ref, *, n_qkv, u_tiles):
    j = pl.program_id(1)
    slabs, tm, _ = hn_ref.shape

    @pl.when(j == 0)
    def _():
        x = x_ref[...]
        inv = lax.rsqrt(jnp.mean(x * x, axis=-1, keepdims=True) + EPS)
        q = tm // PERM_BASE
        for c in range(x_ref.shape[1] // LANES):
            cols = slice(c * LANES, (c + 1) * LANES)
            s = c % slabs
            hn = x_ref[:, cols] * inv * g_ref[:, cols]
            h_ref[0, :, cols] = hn.astype(BF16)
            hn_ref[s] = hn
            for b in range(PERM_BASE):
                part = hn_ref[s, pl.ds(b, q, stride=PERM_BASE), :]
                hb_ref[s, b * q:(b + 1) * q, :] = part
                for v, dil in enumerate(ROW_ORDERS):
                    if dil == PERM_BASE:
                        h_ref[v, b * q:(b + 1) * q, cols] = part.astype(BF16)
            for v, dil in enumerate(ROW_ORDERS):
                if dil > PERM_BASE:
                    k, n = dil // PERM_BASE, tm // dil
                    for b in range(PERM_BASE):
                        for a in range(k):
                            blk = b * k + a
                            h_ref[v, blk * n:(blk + 1) * n, cols] = (
                                hb_ref[s, pl.ds(b * q + a, n, stride=k), :].astype(BF16))

    order = jnp.where(j < n_qkv, j % N_GROUPS, jnp.where(j < n_qkv + u_tiles, N_GROUPS, 0))
    o_ref[...] = jnp.dot(h_ref[order], w_ref[...], preferred_element_type=F32).astype(o_ref.dtype)


def _inproj(x, gain, w, ssm_width):
    t, d = x.shape
    tm = PROJ_TILE
    ncol, _, tn = w.shape
    assert t % tm == 0 and tn == GROUP_WIDTH and ssm_width % tn == 0 and d % LANES == 0
    assert ROW_ORDERS[0] == 1 and all(tm % (dil * 16) == 0 for dil in ROW_ORDERS)
    assert all(dil in (1, PERM_BASE) or (dil % PERM_BASE == 0 and dil // PERM_BASE <= PERM_BASE) for dil in ROW_ORDERS)
    slabs = 8
    return pl.pallas_call(
        functools.partial(_inproj_kernel, n_qkv=3 * N_GROUPS, u_tiles=ssm_width // tn),
        grid=(t // tm, ncol),
        in_specs=[
            pl.BlockSpec((tm, d), lambda i, j: (i, 0)),
            pl.BlockSpec((1, d), lambda i, j: (0, 0)),
            pl.BlockSpec((None, d, tn), lambda i, j: (j, 0, 0)),
        ],
        out_specs=pl.BlockSpec((None, None, tm, tn), lambda i, j: (i, j, 0, 0)),
        out_shape=jax.ShapeDtypeStruct((t // tm, ncol, tm, tn), BF16),
        scratch_shapes=[
            pltpu.VMEM((len(ROW_ORDERS), tm, d), BF16),
            pltpu.VMEM((slabs, tm, LANES), F32),
            pltpu.VMEM((slabs, tm, LANES), F32),
        ],
        compiler_params=_params("parallel", "arbitrary"),
        name="inproj",
    )(x, gain.reshape(1, d), w)


def _attn_bias_table():
    qi = np.arange(ATTN_BLK)[:, None]
    kj = np.arange(ATTN_BLK)[None, :]
    table = np.full((N_GROUPS, 2, HEADS_PER_GROUP * ATTN_BLK, 2 * ATTN_BLK), NEG, np.float32)
    for g, (_, dil) in enumerate(ATTN_GROUPS):
        for h in range(HEADS_PER_GROUP):
            slope = 2.0 ** (-8.0 * (g * HEADS_PER_GROUP + h + 1) / N_HEADS) * dil * LOG2E
            rows = slice(h * ATTN_BLK, (h + 1) * ATTN_BLK)
            cur = np.where(kj <= qi, -slope * (qi - kj), NEG)
            prev = np.where(kj >= qi, -slope * (ATTN_BLK + qi - kj), NEG)
            table[g, :, rows, ATTN_BLK:] = cur
            table[g, 0, rows, :ATTN_BLK] = prev
    return table


def _attn_kernel(bias_ref, *refs):
    qkv_refs = refs[:3 * N_GROUPS]
    o_ref, out_scr, lse_scr = refs[3 * N_GROUPS:3 * N_GROUPS + 3]
    hist_refs = refs[3 * N_GROUPS + 3:]
    tile = pl.program_id(1)
    u = pl.program_id(2)
    units = ATTN_TILE // ATTN_BLK
    nt = (((1,), (1,)), ((), ()))

    @pl.when(jnp.logical_and(tile == 0, u == 0))
    def _():
        for hist in hist_refs:
            hist[...] = jnp.zeros_like(hist)

    for g, (_, dil) in enumerate(ATTN_GROUPS):
        hist = hist_refs[g]
        nb = u // dil
        r = u % dil
        first = jnp.logical_and(tile == 0, nb == 0).astype(jnp.int32)
        start = nb * (ATTN_BLK * dil) + r
        rows = pl.ds(start, ATTN_BLK) if dil == 1 else pl.ds(start, ATTN_BLK, stride=dil)
        q, k, v = (ref[...].reshape(ATTN_BLK, GROUP_WIDTH) for ref in qkv_refs[3 * g:3 * g + 3])
        prev = hist[r]
        scores = []
        for h in range(HEADS_PER_GROUP):
            cs = slice(h * HEAD_DIM, (h + 1) * HEAD_DIM)
            keys = jnp.concatenate([prev[:, cs], k[:, cs]], axis=0)
            scores.append(lax.dot_general(q[:, cs], keys, nt, preferred_element_type=F32))
        s = jnp.concatenate(scores, axis=0) * (HEAD_DIM ** -0.5 * LOG2E) + bias_ref[g, first]
        m = jnp.max(s, axis=1, keepdims=True)
        p = jnp.exp2(s - m)
        l = jnp.sum(p, axis=1, keepdims=True)
        p = p.astype(BF16)
        inv = 1.0 / l
        lse = m + jnp.log2(l)
        for h in range(HEADS_PER_GROUP):
            cs = slice(h * HEAD_DIM, (h + 1) * HEAD_DIM)
            hr = slice(h * ATTN_BLK, (h + 1) * ATTN_BLK)
            vals = jnp.concatenate([prev[:, GROUP_WIDTH:][:, cs], v[:, cs]], axis=0)
            o = jnp.dot(p[hr], vals, preferred_element_type=F32)
            out_scr[g, h, rows, :] = o * inv[hr]
            lse_scr[g, h, rows, :] = jnp.broadcast_to(lse[hr], (ATTN_BLK, HEAD_DIM))
        hist[r, :, :GROUP_WIDTH] = k
        hist[r, :, GROUP_WIDTH:] = v

    @pl.when(u == units - 1)
    def _():
        for h in range(HEADS_PER_GROUP):
            lses = [lse_scr[g, h] for g in range(N_GROUPS)]
            top = functools.reduce(jnp.maximum, lses)
            ws = [jnp.exp2(x - top) for x in lses]
            num = sum(w * out_scr[g, h] for g, w in enumerate(ws))
            o_ref[:, h * HEAD_DIM:(h + 1) * HEAD_DIM] = (num / sum(ws)).astype(o_ref.dtype)


def _attention(proj, bsz, seq):
    assert seq % ATTN_TILE == 0 and ATTN_TILE % PROJ_TILE == 0
    tiles = seq // ATTN_TILE
    units = ATTN_TILE // ATTN_BLK
    bias = jnp.asarray(_attn_bias_table())
    in_specs = [pl.BlockSpec(bias.shape, lambda b, tile, u: (0, 0, 0, 0))]
    hist = []
    for g, (window, dil) in enumerate(ATTN_GROUPS):
        assert window // dil == ATTN_BLK and units % dil == 0
        span = ATTN_BLK * dil
        spans_per_tile = ATTN_TILE // span
        if span <= PROJ_TILE:
            block = (None, None, ATTN_BLK, GROUP_WIDTH)

            def index(b, tile, u, *, col, dil=dil, spt=spans_per_tile, sppt=PROJ_TILE // span,
                      per_seq=seq // PROJ_TILE):
                sp = tile * spt + u // dil
                return b * per_seq + sp // sppt, col, _residue_block(u % dil, dil) * sppt + sp % sppt, 0
        else:
            block = (span // PROJ_TILE, None, PROJ_TILE // dil, GROUP_WIDTH)

            def index(b, tile, u, *, col, dil=dil, spt=spans_per_tile, per_seq=seq // span):
                return b * per_seq + tile * spt + u // dil, col, _residue_block(u % dil, dil), 0

        for part in range(3):
            in_specs.append(pl.BlockSpec(block, functools.partial(index, col=part * N_GROUPS + g)))
        hist.append(pltpu.VMEM((dil, ATTN_BLK, 2 * GROUP_WIDTH), BF16))
    scratch = pltpu.VMEM((N_GROUPS, HEADS_PER_GROUP, ATTN_TILE, HEAD_DIM), F32)
    return pl.pallas_call(
        _attn_kernel,
        grid=(bsz, tiles, units),
        in_specs=in_specs,
        out_specs=pl.BlockSpec((ATTN_TILE, GROUP_WIDTH), lambda b, tile, u: (b * tiles + tile, 0)),
        out_shape=jax.ShapeDtypeStruct((bsz * seq, GROUP_WIDTH), BF16),
        scratch_shapes=[scratch, scratch] + hist,
        compiler_params=_params("arbitrary", "arbitrary", "arbitrary"),
        name="dilated_attention",
    )(bias, *([proj] * (3 * N_GROUPS)))


def _cmul(a, b):
    return a[0] * b[0] - a[1] * b[1], a[0] * b[1] + a[1] * b[0]


def _ssm_weights(a_re, a_im, log_dt, b_re, b_im, c_re, c_im):
    n_groups = a_re.shape[0]
    nblk = n_groups // GROUPS_PER_LANE_BLOCK
    gl = GROUPS_PER_LANE_BLOCK
    L = SSM_CHUNK
    a_re, a_im = a_re.astype(F32), a_im.astype(F32)
    dt = jnp.exp(log_dt.astype(F32))[:, None]
    steps = jnp.arange(L + 1, dtype=F32)[None, :, None]
    mag = jnp.exp((a_re * dt)[:, None, :] * steps)
    ang = (a_im * dt)[:, None, :] * steps
    powers = (mag * jnp.cos(ang), mag * jnp.sin(ang))
    lam_bar = (powers[0][:, 1], powers[1][:, 1])
    den = a_re * a_re + a_im * a_im
    num = (lam_bar[0] - 1.0, lam_bar[1])
    ratio = ((num[0] * a_re + num[1] * a_im) / den, (num[1] * a_re - num[0] * a_im) / den)
    b_bar = _cmul((ratio[0][..., None], ratio[1][..., None]), (b_re.astype(F32), b_im.astype(F32)))
    c_t = (c_re.astype(F32).transpose(0, 2, 1), c_im.astype(F32).transpose(0, 2, 1))
    eye = jnp.eye(gl, dtype=F32)

    def block_diag(m):
        rows, cols = m.shape[1:]
        m = m.reshape(nblk, gl, rows, 1, cols) * eye[None, :, None, :, None]
        return m.reshape(nblk, gl * rows, gl * cols)

    b_in = jnp.stack([block_diag(b.transpose(0, 2, 1)) for b in b_bar], axis=1)
    c_out = jnp.stack([block_diag(c) for c in c_t], axis=1)
    pw = jnp.stack(powers, axis=0).reshape(2, nblk, gl, L + 1, SSM_STATE)
    pw_row = pw.transpose(1, 0, 3, 2, 4).reshape(nblk, 2, L + 1, gl * SSM_STATE)
    pw_col = pw_row.transpose(0, 1, 3, 2)
    return b_in, c_out, pw_row, pw_col


def _ssm_kernel(*refs, tiles_per_seq):
    u_refs = refs[:SSM_CHUNK]
    (bin_ref, cout_ref, pwr_ref, pwc_ref, d_ref, y_ref,
     wt_scr, wb_scr, wc_scr, s_scr, xp_scr, carry_scr, y_scr) = refs[SSM_CHUNK:]
    L = SSM_CHUNK
    tc = s_scr.shape[0]
    half = s_scr.shape[1] // 2

    @pl.when(pl.program_id(1) == 0)
    def _():
        hi = lax.Precision.HIGHEST
        for j in range(L):
            rows = slice(j * LANES, (j + 1) * LANES)
            n = L - 1 - j
            pr = pwr_ref[0, 0, n:n + 1, :]
            pi = pwr_ref[0, 1, n:n + 1, :]
            qr = bin_ref[0, 0] * pr - bin_ref[0, 1] * pi
            qi = bin_ref[0, 0] * pi + bin_ref[0, 1] * pr
            wb_scr[rows, :half] = qr.astype(BF16)
            wb_scr[rows, half:] = qi.astype(BF16)
            kn = (jnp.dot(qr, cout_ref[0, 0], precision=hi, preferred_element_type=F32)
                  - jnp.dot(qi, cout_ref[0, 1], precision=hi, preferred_element_type=F32)).astype(BF16)
            for jj in range(L):
                ii = jj + n
                if ii < L:
                    wt_scr[jj * LANES:(jj + 1) * LANES, ii * LANES:(ii + 1) * LANES] = kn
                if j < jj:
                    wt_scr[jj * LANES:(jj + 1) * LANES, rows] = jnp.zeros((LANES, LANES), BF16)
            pr = pwc_ref[0, 0, :, j + 1:j + 2]
            pi = pwc_ref[0, 1, :, j + 1:j + 2]
            wc_scr[:half, rows] = (cout_ref[0, 0] * pr - cout_ref[0, 1] * pi).astype(BF16)
            wc_scr[half:, rows] = (-(cout_ref[0, 0] * pi + cout_ref[0, 1] * pr)).astype(BF16)

    @pl.when(pl.program_id(1) % tiles_per_seq == 0)
    def _():
        carry_scr[...] = jnp.zeros_like(carry_scr)

    us = [r[...].reshape(tc, LANES) for r in u_refs]
    ucat = jnp.concatenate(us, axis=1)
    s_scr[...] = jnp.dot(ucat, wb_scr[...], preferred_element_type=F32)
    ar = pwr_ref[0, 0, L:L + 1, :]
    ai = pwr_ref[0, 1, L:L + 1, :]

    def step(c, carry):
        xr, xi = carry
        xp_scr[pl.ds(c, 1), :half] = xr
        xp_scr[pl.ds(c, 1), half:] = xi
        s = s_scr[pl.ds(c, 1), :]
        return ar * xr - ai * xi + s[:, :half], ar * xi + ai * xr + s[:, half:]

    xr, xi = lax.fori_loop(0, tc, step, (carry_scr[:, :half], carry_scr[:, half:]))
    carry_scr[:, :half] = xr
    carry_scr[:, half:] = xi

    y = jnp.dot(ucat, wt_scr[...], preferred_element_type=F32)
    y = y + jnp.dot(xp_scr[...].astype(BF16), wc_scr[...], preferred_element_type=F32)
    for i in range(L):
        yi = y[:, i * LANES:(i + 1) * LANES] + d_ref[0] * us[i].astype(F32)
        y_scr[pl.ds(i, tc, stride=L), :] = jax.nn.gelu(yi)
    y_ref[...] = y_scr[...].astype(y_ref.dtype)


def _ssm(proj, u_tile0, ssm_w, d_skip, bsz, seq, *, ptiles):
    nt, _, tm, tn = proj.shape
    L = SSM_CHUNK
    per_tile = tn // LANES
    crows = tm // L
    nblk = ssm_w[0].shape[0]
    width = nblk * LANES
    assert crows == LANES
    tc = ptiles * crows
    rows_per_seq = seq // L
    assert rows_per_seq % tc == 0 and nt % ptiles == 0
    u_specs = [
        pl.BlockSpec((ptiles, None, crows, LANES), functools.partial(
            lambda blk, i, j: (i, u_tile0 + blk // per_tile, j, blk % per_tile), j=_residue_block(j, L)))
        for j in range(L)
    ]
    w_specs = [pl.BlockSpec((1,) + w.shape[1:], lambda blk, i: (blk, 0, 0, 0)) for w in ssm_w]
    wide = L * LANES
    states = 2 * GROUPS_PER_LANE_BLOCK * SSM_STATE
    return pl.pallas_call(
        functools.partial(_ssm_kernel, tiles_per_seq=rows_per_seq // tc),
        grid=(nblk, nt // ptiles),
        in_specs=u_specs + w_specs + [pl.BlockSpec((1, 1, LANES), lambda blk, i: (blk, 0, 0))],
        out_specs=pl.BlockSpec((tc * L, LANES), lambda blk, i: (i, blk)),
        out_shape=jax.ShapeDtypeStruct((nt * tm, width), BF16),
        scratch_shapes=[
            pltpu.VMEM((wide, wide), BF16),
            pltpu.VMEM((wide, states), BF16),
            pltpu.VMEM((states, wide), BF16),
            pltpu.VMEM((tc, states), F32),
            pltpu.VMEM((tc, states), F32),
            pltpu.VMEM((1, states), F32),
            pltpu.VMEM((tc * L, LANES), F32),
        ],
        compiler_params=_params("parallel", "arbitrary"),
        name="s5_chunked",
    )(*([proj] * L), *ssm_w, d_skip.astype(F32).reshape(nblk, 1, LANES))


def _merge_kernel(attn_ref, y_ref, ga_ref, gs_ref, wup_ref, wv_ref, wg_ref, o_ref):
    y = y_ref[...]
    attn_branch = jnp.dot(attn_ref[...], wup_ref[...].astype(BF16), preferred_element_type=F32)
    val = jnp.dot(y, wv_ref[...].astype(BF16), preferred_element_type=F32)
    gate = jnp.dot(y, wg_ref[...].astype(BF16), preferred_element_type=F32)
    ssm_branch = val * jax.nn.sigmoid(gate)
    merged = (jax.nn.sigmoid(ga_ref[...].astype(F32)) * attn_branch
              + jax.nn.sigmoid(gs_ref[...].astype(F32)) * ssm_branch)
    o_ref[...] = merged.astype(o_ref.dtype)


def _merge(attn, y, proj, gate_tile0, w_up, w_v, w_g):
    t = attn.shape[0]
    n = w_up.shape[1]
    _, _, tm, tn = proj.shape
    assert n % tn == 0
    return pl.pallas_call(
        _merge_kernel,
        grid=(t // tm, n // tn),
        in_specs=[
            pl.BlockSpec((tm, attn.shape[1]), lambda i, j: (i, 0)),
            pl.BlockSpec((tm, y.shape[1]), lambda i, j: (i, 0)),
            pl.BlockSpec((None, None, tm, tn), lambda i, j: (i, gate_tile0 + j, 0, 0)),
            pl.BlockSpec((None, None, tm, tn), lambda i, j: (i, gate_tile0 + n // tn + j, 0, 0)),
            pl.BlockSpec((w_up.shape[0], tn), lambda i, j: (0, j)),
            pl.BlockSpec((w_v.shape[0], tn), lambda i, j: (0, j)),
            pl.BlockSpec((w_g.shape[0], tn), lambda i, j: (0, j)),
        ],
        out_specs=pl.BlockSpec((tm, tn), lambda i, j: (i, j)),
        out_shape=jax.ShapeDtypeStruct((t, n), BF16),
        compiler_params=_params("parallel", "arbitrary"),
        name="gated_merge",
    )(attn, y, proj, proj, w_up, w_v, w_g)


def _outproj_kernel(m_ref, w_ref, x_ref, g_ref, o_ref):
    z = jnp.dot(m_ref[...], w_ref[...], preferred_element_type=F32)
    o_ref[...] = x_ref[...] + _rms(z, g_ref[...])


def _outproj(merged, w, x, gain, *, tm):
    t, d = x.shape
    return pl.pallas_call(
        _outproj_kernel,
        grid=(t // tm,),
        in_specs=[
            pl.BlockSpec((tm, merged.shape[1]), lambda i: (i, 0)),
            pl.BlockSpec(w.shape, lambda i: (0, 0)),
            pl.BlockSpec((tm, d), lambda i: (i, 0)),
            pl.BlockSpec((1, d), lambda i: (0, 0)),
        ],
        out_specs=pl.BlockSpec((tm, d), lambda i: (i, 0)),
        out_shape=jax.ShapeDtypeStruct((t, d), F32),
        compiler_params=_params("parallel"),
        name="outproj_norm_residual",
    )(merged, w, x, gain.reshape(1, d))


def _ffn_kernel(x_ref, gpre_ref, gpost_ref, wg_ref, wu_ref, wd_ref, o_ref, h_ref, acc_ref):
    k = pl.program_id(1)

    @pl.when(k == 0)
    def _():
        h_ref[...] = _rms(x_ref[...], gpre_ref[...]).astype(BF16)
        acc_ref[...] = jnp.zeros_like(acc_ref)

    h = h_ref[...]
    gate = jnp.dot(h, wg_ref[...], preferred_element_type=F32)
    up = jnp.dot(h, wu_ref[...], preferred_element_type=F32)
    f = (jax.nn.silu(gate) * up).astype(BF16)
    acc_ref[...] += jnp.dot(f, wd_ref[...], preferred_element_type=F32)

    @pl.when(k == pl.num_programs(1) - 1)
    def _():
        o_ref[...] = x_ref[...] + _rms(acc_ref[...], gpost_ref[...])


def _ffn(x, gain_pre, gain_post, w_gate, w_up, w_down, *, tm):
    t, d = x.shape
    ntile, _, tf = w_gate.shape
    dff = ntile * tf
    return pl.pallas_call(
        _ffn_kernel,
        grid=(t // tm, dff // tf),
        in_specs=[
            pl.BlockSpec((tm, d), lambda i, k: (i, 0)),
            pl.BlockSpec((1, d), lambda i, k: (0, 0)),
            pl.BlockSpec((1, d), lambda i, k: (0, 0)),
            pl.BlockSpec((None, d, tf), lambda i, k: (k, 0, 0)),
            pl.BlockSpec((None, d, tf), lambda i, k: (k, 0, 0)),
            pl.BlockSpec((tf, d), lambda i, k: (k, 0)),
        ],
        out_specs=pl.BlockSpec((tm, d), lambda i, k: (i, 0)),
        out_shape=jax.ShapeDtypeStruct((t, d), F32),
        scratch_shapes=[pltpu.VMEM((tm, d), BF16), pltpu.VMEM((tm, d), F32)],
        compiler_params=_params("parallel", "arbitrary"),
        name="swiglu_ffn",
    )(x, gain_pre.reshape(1, d), gain_post.reshape(1, d), w_gate, w_up, w_down)


def _column_tiles(w, tn):
    k, n = w.shape
    return w.astype(BF16).reshape(k, n // tn, tn).transpose(1, 0, 2)


def _layer(x, norm_mix_pre, w_in, w_attn_up, ssm_a_re, ssm_a_im, ssm_log_dt, ssm_b_re, ssm_b_im,
           ssm_c_re, ssm_c_im, ssm_d, w_glu_v, w_glu_g, w_out, norm_mix_post, norm_ffn_pre,
           w_ffn_gate, w_ffn_up, w_ffn_down, norm_ffn_post):
    bsz, seq, d = x.shape
    t = bsz * seq
    ssm_width = ssm_d.shape[0]
    u_col0 = 3 * N_HEADS * HEAD_DIM
    assert w_in.shape[1] == u_col0 + ssm_width + 2 * d

    x2 = x.reshape(t, d)
    tn = GROUP_WIDTH
    proj = _inproj(x2, norm_mix_pre, _column_tiles(w_in, tn), ssm_width)
    attn = _attention(proj, bsz, seq)
    ssm_w = _ssm_weights(ssm_a_re, ssm_a_im, ssm_log_dt, ssm_b_re, ssm_b_im, ssm_c_re, ssm_c_im)
    y = _ssm(proj, u_col0 // tn, ssm_w, ssm_d, bsz, seq, ptiles=4)
    merged = _merge(attn, y, proj, (u_col0 + ssm_width) // tn, w_attn_up, w_glu_v, w_glu_g)
    x1 = _outproj(merged, w_out.astype(BF16), x2, norm_mix_post, tm=512)
    out = _ffn(x1, norm_ffn_pre, norm_ffn_post, _column_tiles(w_ffn_gate, 512), _column_tiles(w_ffn_up, 512),
               w_ffn_down.astype(BF16), tm=512)
    return out.reshape(bsz, seq, d)


def kernel(x, norm_mix_pre, w_in, w_attn_up, ssm_a_re, ssm_a_im, ssm_log_dt, ssm_b_re, ssm_b_im, ssm_c_re, ssm_c_im, ssm_d, w_glu_v, w_glu_g, w_out, norm_mix_post, norm_ffn_pre, w_ffn_gate, w_ffn_up, w_ffn_down, norm_ffn_post):
    stacked = (norm_mix_pre, w_in, w_attn_up, ssm_a_re, ssm_a_im, ssm_log_dt, ssm_b_re, ssm_b_im, ssm_c_re,
               ssm_c_im, ssm_d, w_glu_v, w_glu_g, w_out, norm_mix_post, norm_ffn_pre, w_ffn_gate, w_ffn_up,
               w_ffn_down, norm_ffn_post)
    for layer in range(norm_mix_pre.shape[0]):
        x = _layer(x, *(p[layer] for p in stacked))
    return x
```

```python
import functools

import jax
import jax.numpy as jnp
import numpy as np
from jax import lax
from jax.experimental import pallas as pl
from jax.experimental.pallas import tpu as pltpu

F32 = jnp.float32
BF16 = jnp.bfloat16

EPS = 1e-6
HEAD_DIM = 128
HEADS_PER_GROUP = 4
ATTN_GROUPS = ((128, 1), (512, 4), (2048, 16))
N_GROUPS = len(ATTN_GROUPS)
N_HEADS = HEADS_PER_GROUP * N_GROUPS
GROUP_WIDTH = HEADS_PER_GROUP * HEAD_DIM
ATTN_BLK = 128
ATTN_TILE = 2048
SSM_GROUP = 16
SSM_STATE = 64
SSM_CHUNK = 8
LANES = 128
GROUPS_PER_LANE_BLOCK = LANES // SSM_GROUP
PROJ_TILE = 1024
ROW_ORDERS = tuple(dil for _, dil in ATTN_GROUPS) + (SSM_CHUNK,)
PERM_BASE = 4
NEG = -1e30
LOG2E = 1.4426950408889634
VMEM_LIMIT = 56 * 1024 * 1024


def _params(*sem):
    return pltpu.CompilerParams(dimension_semantics=sem, vmem_limit_bytes=VMEM_LIMIT)


def _rms(x, gain):
    return x * lax.rsqrt(jnp.mean(x * x, axis=-1, keepdims=True) + EPS) * gain


def _residue_block(r, dil):
    if dil <= PERM_BASE:
        return r
    return (r % PERM_BASE) * (dil // PERM_BASE) + r // PERM_BASE


def _inproj_kernel(x_ref, g_ref, w_ref, o_ref, h_ref, hn_ref, hb_ref, inv_ref, *, n_qkv, u_tiles):
    j = pl.program_id(1)
    slabs, tm, _ = hn_ref.shape

    @pl.when(j == 0)
    def _():
        x = x_ref[...]
        inv_ref[...] = jnp.broadcast_to(lax.rsqrt(jnp.mean(x * x, axis=-1, keepdims=True) + EPS), inv_ref.shape)
        q = tm // PERM_BASE
        piece = 256
        for c in range(x_ref.shape[1] // LANES):
            cols = slice(c * LANES, (c + 1) * LANES)
            s = c % slabs
            for r0 in range(0, tm, piece):
                rows = slice(r0, r0 + piece)
                hn = x_ref[rows, cols] * inv_ref[rows, :] * g_ref[:, cols]
                h_ref[0, rows, cols] = hn.astype(BF16)
                hn_ref[s, rows, :] = hn
            for b in range(PERM_BASE):
                part = hn_ref[s, pl.ds(b, q, stride=PERM_BASE), :]
                hb_ref[s, b * q:(b + 1) * q, :] = part
                for v, dil in enumerate(ROW_ORDERS):
                    if dil == PERM_BASE:
                        h_ref[v, b * q:(b + 1) * q, cols] = part.astype(BF16)
            for v, dil in enumerate(ROW_ORDERS):
                if dil > PERM_BASE:
                    k, n = dil // PERM_BASE, tm // dil
                    for b in range(PERM_BASE):
                        for a in range(k):
                            blk = b * k + a
                            h_ref[v, blk * n:(blk + 1) * n, cols] = (
                                hb_ref[s, pl.ds(b * q + a, n, stride=k), :].astype(BF16))

    order = jnp.where(j < n_qkv, j % N_GROUPS, jnp.where(j < n_qkv + u_tiles, N_GROUPS, 0))
    o_ref[...] = jnp.dot(h_ref[order], w_ref[...], preferred_element_type=F32).astype(o_ref.dtype)


def _inproj(x, gain, w, ssm_width):
    t, d = x.shape
    tm = PROJ_TILE
    tn = GROUP_WIDTH
    ncol = w.shape[1] // tn
    assert t % tm == 0 and w.shape[1] % tn == 0 and ssm_width % tn == 0 and d % LANES == 0
    assert ROW_ORDERS[0] == 1 and all(tm % (dil * 16) == 0 for dil in ROW_ORDERS)
    assert all(dil in (1, PERM_BASE) or (dil % PERM_BASE == 0 and dil // PERM_BASE <= PERM_BASE) for dil in ROW_ORDERS)
    slabs = 8
    return pl.pallas_call(
        functools.partial(_inproj_kernel, n_qkv=3 * N_GROUPS, u_tiles=ssm_width // tn),
        grid=(t // tm, ncol),
        in_specs=[
            pl.BlockSpec((tm, d), lambda i, j: (i, 0)),
            pl.BlockSpec((1, d), lambda i, j: (0, 0)),
            pl.BlockSpec((d, tn), lambda i, j: (0, j)),
        ],
        out_specs=pl.BlockSpec((None, None, tm, tn), lambda i, j: (i, j, 0, 0)),
        out_shape=jax.ShapeDtypeStruct((t // tm, ncol, tm, tn), BF16),
        scratch_shapes=[
            pltpu.VMEM((len(ROW_ORDERS), tm, d), BF16),
            pltpu.VMEM((slabs, tm, LANES), F32),
            pltpu.VMEM((slabs, tm, LANES), F32),
            pltpu.VMEM((tm, LANES), F32),
        ],
        compiler_params=_params("parallel", "arbitrary"),
        name="inproj",
    )(x, gain.reshape(1, d), w)


def _attn_bias_table():
    qi = np.arange(ATTN_BLK)[:, None]
    kj = np.arange(ATTN_BLK)[None, :]
    table = np.full((N_GROUPS, 2, HEADS_PER_GROUP * ATTN_BLK, 2 * ATTN_BLK), NEG, np.float32)
    for g, (_, dil) in enumerate(ATTN_GROUPS):
        for h in range(HEADS_PER_GROUP):
            slope = 2.0 ** (-8.0 * (g * HEADS_PER_GROUP + h + 1) / N_HEADS) * dil * LOG2E
            rows = slice(h * ATTN_BLK, (h + 1) * ATTN_BLK)
            cur = np.where(kj <= qi, -slope * (qi - kj), NEG)
            prev = np.where(kj >= qi, -slope * (ATTN_BLK + qi - kj), NEG)
            table[g, :, rows, ATTN_BLK:] = cur
            table[g, 0, rows, :ATTN_BLK] = prev
    return table


def _attn_kernel(bias_ref, *refs):
    qkv_refs = refs[:3 * N_GROUPS]
    o_ref, out_scr, lse_scr = refs[3 * N_GROUPS:3 * N_GROUPS + 3]
    hist_refs = refs[3 * N_GROUPS + 3:]
    tile = pl.program_id(1)
    u = pl.program_id(2)
    units = ATTN_TILE // ATTN_BLK
    nt = (((1,), (1,)), ((), ()))

    @pl.when(jnp.logical_and(tile == 0, u == 0))
    def _():
        for hist in hist_refs:
            hist[...] = jnp.zeros_like(hist)

    for g, (_, dil) in enumerate(ATTN_GROUPS):
        hist = hist_refs[g]
        nb = u // dil
        r = u % dil
        first = jnp.logical_and(tile == 0, nb == 0).astype(jnp.int32)
        start = nb * (ATTN_BLK * dil) + r
        rows = pl.ds(start, ATTN_BLK) if dil == 1 else pl.ds(start, ATTN_BLK, stride=dil)
        q, k, v = (ref[...].reshape(ATTN_BLK, GROUP_WIDTH) for ref in qkv_refs[3 * g:3 * g + 3])
        prev = hist[r]
        scores = []
        for h in range(HEADS_PER_GROUP):
            cs = slice(h * HEAD_DIM, (h + 1) * HEAD_DIM)
            keys = jnp.concatenate([prev[:, cs], k[:, cs]], axis=0)
            scores.append(lax.dot_general(q[:, cs], keys, nt, preferred_element_type=F32))
        s = jnp.concatenate(scores, axis=0) * (HEAD_DIM ** -0.5 * LOG2E) + bias_ref[g, first]
        m = jnp.max(s, axis=1, keepdims=True)
        p = jnp.exp2(s - m)
        l = jnp.sum(p, axis=1, keepdims=True)
        p = p.astype(BF16)
        inv = 1.0 / l
        lse = m + jnp.log2(l)
        for h in range(HEADS_PER_GROUP):
            cs = slice(h * HEAD_DIM, (h + 1) * HEAD_DIM)
            hr = slice(h * ATTN_BLK, (h + 1) * ATTN_BLK)
            vals = jnp.concatenate([prev[:, GROUP_WIDTH:][:, cs], v[:, cs]], axis=0)
            o = jnp.dot(p[hr], vals, preferred_element_type=F32)
            out_scr[g, h, rows, :] = o * inv[hr]
            lse_scr[g, h, rows, :] = jnp.broadcast_to(lse[hr], (ATTN_BLK, HEAD_DIM))
        hist[r, :, :GROUP_WIDTH] = k
        hist[r, :, GROUP_WIDTH:] = v

    @pl.when(u == units - 1)
    def _():
        for h in range(HEADS_PER_GROUP):
            lses = [lse_scr[g, h] for g in range(N_GROUPS)]
            top = functools.reduce(jnp.maximum, lses)
            ws = [jnp.exp2(x - top) for x in lses]
            num = sum(w * out_scr[g, h] for g, w in enumerate(ws))
            o_ref[:, h * HEAD_DIM:(h + 1) * HEAD_DIM] = (num / sum(ws)).astype(o_ref.dtype)


def _attention(proj, bsz, seq):
    assert seq % ATTN_TILE == 0 and ATTN_TILE % PROJ_TILE == 0
    tiles = seq // ATTN_TILE
    units = ATTN_TILE // ATTN_BLK
    bias = jnp.asarray(_attn_bias_table())
    in_specs = [pl.BlockSpec(bias.shape, lambda b, tile, u: (0, 0, 0, 0))]
    hist = []
    for g, (window, dil) in enumerate(ATTN_GROUPS):
        assert window // dil == ATTN_BLK and units % dil == 0
        span = ATTN_BLK * dil
        spans_per_tile = ATTN_TILE // span
        if span <= PROJ_TILE:
            block = (None, None, ATTN_BLK, GROUP_WIDTH)

            def index(b, tile, u, *, col, dil=dil, spt=spans_per_tile, sppt=PROJ_TILE // span,
                      per_seq=seq // PROJ_TILE):
                sp = tile * spt + u // dil
                return b * per_seq + sp // sppt, col, _residue_block(u % dil, dil) * sppt + sp % sppt, 0
        else:
            block = (span // PROJ_TILE, None, PROJ_TILE // dil, GROUP_WIDTH)

            def index(b, tile, u, *, col, dil=dil, spt=spans_per_tile, per_seq=seq // span):
                return b * per_seq + tile * spt + u // dil, col, _residue_block(u % dil, dil), 0

        for part in range(3):
            in_specs.append(pl.BlockSpec(block, functools.partial(index, col=part * N_GROUPS + g)))
        hist.append(pltpu.VMEM((dil, ATTN_BLK, 2 * GROUP_WIDTH), BF16))
    scratch = pltpu.VMEM((N_GROUPS, HEADS_PER_GROUP, ATTN_TILE, HEAD_DIM), F32)
    return pl.pallas_call(
        _attn_kernel,
        grid=(bsz, tiles, units),
        in_specs=in_specs,
        out_specs=pl.BlockSpec((ATTN_TILE, GROUP_WIDTH), lambda b, tile, u: (b * tiles + tile, 0)),
        out_shape=jax.ShapeDtypeStruct((bsz * seq, GROUP_WIDTH), BF16),
        scratch_shapes=[scratch, scratch] + hist,
        compiler_params=_params("arbitrary", "arbitrary", "arbitrary"),
        name="dilated_attention",
    )(bias, *([proj] * (3 * N_GROUPS)))


def _cmul(a, b):
    return a[0] * b[0] - a[1] * b[1], a[0] * b[1] + a[1] * b[0]


def _ssm_weights(a_re, a_im, log_dt, b_re, b_im, c_re, c_im):
    n_groups = a_re.shape[0]
    nblk = n_groups // GROUPS_PER_LANE_BLOCK
    gl = GROUPS_PER_LANE_BLOCK
    L = SSM_CHUNK
    a_re, a_im = a_re.astype(F32), a_im.astype(F32)
    dt = jnp.exp(log_dt.astype(F32))[:, None]
    steps = jnp.arange(L + 1, dtype=F32)[None, :, None]
    mag = jnp.exp((a_re * dt)[:, None, :] * steps)
    ang = (a_im * dt)[:, None, :] * steps
    powers = (mag * jnp.cos(ang), mag * jnp.sin(ang))
    lam_bar = (powers[0][:, 1], powers[1][:, 1])
    den = a_re * a_re + a_im * a_im
    num = (lam_bar[0] - 1.0, lam_bar[1])
    ratio = ((num[0] * a_re + num[1] * a_im) / den, (num[1] * a_re - num[0] * a_im) / den)
    b_bar = _cmul((ratio[0][..., None], ratio[1][..., None]), (b_re.astype(F32), b_im.astype(F32)))
    c_t = (c_re.astype(F32).transpose(0, 2, 1), c_im.astype(F32).transpose(0, 2, 1))
    eye = jnp.eye(gl, dtype=F32)

    def block_diag(m):
        rows, cols = m.shape[1:]
        m = m.reshape(nblk, gl, rows, 1, cols) * eye[None, :, None, :, None]
        return m.reshape(nblk, gl * rows, gl * cols)

    b_in = jnp.stack([block_diag(b.transpose(0, 2, 1)) for b in b_bar], axis=1)
    c_out = jnp.stack([block_diag(c) for c in c_t], axis=1)
    pw = jnp.stack(powers, axis=0).reshape(2, nblk, gl, L + 1, SSM_STATE)
    pw_row = pw.transpose(1, 0, 3, 2, 4).reshape(nblk, 2, L + 1, gl * SSM_STATE)
    pw_col = pw_row.transpose(0, 1, 3, 2)
    return b_in, c_out, pw_row, pw_col


def _ssm_kernel(*refs, tiles_per_seq):
    u_refs = refs[:SSM_CHUNK]
    (bin_ref, cout_ref, pwr_ref, pwc_ref, d_ref, y_ref,
     wt_scr, wb_scr, wc_scr, s_scr, xp_scr, carry_scr, y_scr) = refs[SSM_CHUNK:]
    L = SSM_CHUNK
    nb = wt_scr.shape[0]
    tc = s_scr.shape[0]
    half = s_scr.shape[1] // 2
    sw = half // nb

    @pl.when(pl.program_id(1) == 0)
    def _():
        hi = lax.Precision.HIGHEST
        for q in range(nb):
            for j in range(L):
                rows = slice(j * LANES, (j + 1) * LANES)
                n = L - 1 - j
                pr = pwr_ref[q, 0, n:n + 1, :]
                pi = pwr_ref[q, 1, n:n + 1, :]
                qr = bin_ref[q, 0] * pr - bin_ref[q, 1] * pi
                qi = bin_ref[q, 0] * pi + bin_ref[q, 1] * pr
                wb_scr[q, rows, :sw] = qr.astype(BF16)
                wb_scr[q, rows, sw:] = qi.astype(BF16)
                kn = (jnp.dot(qr, cout_ref[q, 0], precision=hi, preferred_element_type=F32)
                      - jnp.dot(qi, cout_ref[q, 1], precision=hi, preferred_element_type=F32)).astype(BF16)
                for jj in range(L):
                    ii = jj + n
                    if ii < L:
                        wt_scr[q, jj * LANES:(jj + 1) * LANES, ii * LANES:(ii + 1) * LANES] = kn
                    if j < jj:
                        wt_scr[q, jj * LANES:(jj + 1) * LANES, rows] = jnp.zeros((LANES, LANES), BF16)
                pr = pwc_ref[q, 0, :, j + 1:j + 2]
                pi = pwc_ref[q, 1, :, j + 1:j + 2]
                wc_scr[q, :sw, rows] = (cout_ref[q, 0] * pr - cout_ref[q, 1] * pi).astype(BF16)
                wc_scr[q, sw:, rows] = (-(cout_ref[q, 0] * pi + cout_ref[q, 1] * pr)).astype(BF16)

    @pl.when(pl.program_id(1) % tiles_per_seq == 0)
    def _():
        carry_scr[...] = jnp.zeros_like(carry_scr)

    us = [r[...].reshape(tc, nb * LANES) for r in u_refs]
    ucat = [jnp.concatenate([u[:, q * LANES:(q + 1) * LANES] for u in us], axis=1)
            for q in range(nb)]
    for q in range(nb):
        s = jnp.dot(ucat[q], wb_scr[q], preferred_element_type=F32)
        s_scr[:, q * sw:(q + 1) * sw] = s[:, :sw]
        s_scr[:, half + q * sw:half + (q + 1) * sw] = s[:, sw:]
    for q in range(nb):
        re = slice(q * sw, (q + 1) * sw)
        im = slice(half + q * sw, half + (q + 1) * sw)
        ar = pwr_ref[q, 0, L:L + 1, :]
        ai = pwr_ref[q, 1, L:L + 1, :]

        def step(c, carry, re=re, im=im, ar=ar, ai=ai):
            xr, xi = carry
            xp_scr[pl.ds(c, 1), re] = xr
            xp_scr[pl.ds(c, 1), im] = xi
            return (ar * xr - ai * xi + s_scr[pl.ds(c, 1), re], ar * xi + ai * xr + s_scr[pl.ds(c, 1), im])

        xr, xi = lax.fori_loop(0, tc, step, (carry_scr[:, re], carry_scr[:, im]))
        carry_scr[:, re] = xr
        carry_scr[:, im] = xi

    for q in range(nb):
        lanes = slice(q * LANES, (q + 1) * LANES)
        xq = jnp.concatenate([xp_scr[:, q * sw:(q + 1) * sw], xp_scr[:, half + q * sw:half + (q + 1) * sw]], axis=1)
        y = jnp.dot(ucat[q], wt_scr[q], preferred_element_type=F32)
        y = y + jnp.dot(xq.astype(BF16), wc_scr[q], preferred_element_type=F32)
        for i in range(L):
            yi = y[:, i * LANES:(i + 1) * LANES] + d_ref[0, :, lanes] * us[i][:, lanes].astype(F32)
            y_scr[q, pl.ds(i, tc, stride=L), :] = jax.nn.gelu(yi)
        y_ref[:, lanes] = y_scr[q].astype(y_ref.dtype)


def _ssm(proj, u_tile0, ssm_w, d_skip, bsz, seq, *, ptiles, nb):
    nt, _, tm, tn = proj.shape
    L = SSM_CHUNK
    per_tile = tn // (nb * LANES)
    crows = tm // L
    nblk = ssm_w[0].shape[0]
    width = nblk * LANES
    assert crows == LANES and nblk % nb == 0 and tn % (nb * LANES) == 0
    tc = ptiles * crows
    rows_per_seq = seq // L
    assert rows_per_seq % tc == 0 and nt % ptiles == 0
    u_specs = [
        pl.BlockSpec((ptiles, None, crows, nb * LANES), functools.partial(
            lambda blk, i, j: (i, u_tile0 + blk // per_tile, j, blk % per_tile), j=_residue_block(j, L)))
        for j in range(L)
    ]
    w_specs = [pl.BlockSpec((nb,) + w.shape[1:], lambda blk, i: (blk, 0, 0, 0)) for w in ssm_w]
    wide = L * LANES
    states = 2 * GROUPS_PER_LANE_BLOCK * SSM_STATE
    return pl.pallas_call(
        functools.partial(_ssm_kernel, tiles_per_seq=rows_per_seq // tc),
        grid=(nblk // nb, nt // ptiles),
        in_specs=u_specs + w_specs + [pl.BlockSpec((1, 1, nb * LANES), lambda blk, i: (blk, 0, 0))],
        out_specs=pl.BlockSpec((tc * L, nb * LANES), lambda blk, i: (i, blk)),
        out_shape=jax.ShapeDtypeStruct((nt * tm, width), BF16),
        scratch_shapes=[
            pltpu.VMEM((nb, wide, wide), BF16),
            pltpu.VMEM((nb, wide, states), BF16),
            pltpu.VMEM((nb, states, wide), BF16),
            pltpu.VMEM((tc, nb * states), F32),
            pltpu.VMEM((tc, nb * states), F32),
            pltpu.VMEM((1, nb * states), F32),
            pltpu.VMEM((nb, tc * L, LANES), F32),
        ],
        compiler_params=_params("parallel", "arbitrary"),
        name="s5_chunked",
    )(*([proj] * L), *ssm_w, d_skip.astype(F32).reshape(nblk // nb, 1, nb * LANES))


def _merge_kernel(attn_ref, y_ref, ga_ref, gs_ref, wup_ref, wv_ref, wg_ref, o_ref):
    y = y_ref[...]
    attn_branch = jnp.dot(attn_ref[...], wup_ref[...].astype(BF16), preferred_element_type=F32)
    val = jnp.dot(y, wv_ref[...].astype(BF16), preferred_element_type=F32)
    gate = jnp.dot(y, wg_ref[...].astype(BF16), preferred_element_type=F32)
    ssm_branch = val * jax.nn.sigmoid(gate)
    merged = (jax.nn.sigmoid(ga_ref[...].astype(F32)) * attn_branch
              + jax.nn.sigmoid(gs_ref[...].astype(F32)) * ssm_branch)
    o_ref[...] = merged.astype(o_ref.dtype)


def _merge(attn, y, proj, gate_tile0, w_up, w_v, w_g):
    t = attn.shape[0]
    n = w_up.shape[1]
    _, _, tm, tn = proj.shape
    assert n % tn == 0
    return pl.pallas_call(
        _merge_kernel,
        grid=(t // tm, n // tn),
        in_specs=[
            pl.BlockSpec((tm, attn.shape[1]), lambda i, j: (i, 0)),
            pl.BlockSpec((tm, y.shape[1]), lambda i, j: (i, 0)),
            pl.BlockSpec((None, None, tm, tn), lambda i, j: (i, gate_tile0 + j, 0, 0)),
            pl.BlockSpec((None, None, tm, tn), lambda i, j: (i, gate_tile0 + n // tn + j, 0, 0)),
            pl.BlockSpec((w_up.shape[0], tn), lambda i, j: (0, j)),
            pl.BlockSpec((w_v.shape[0], tn), lambda i, j: (0, j)),
            pl.BlockSpec((w_g.shape[0], tn), lambda i, j: (0, j)),
        ],
        out_specs=pl.BlockSpec((tm, tn), lambda i, j: (i, j)),
        out_shape=jax.ShapeDtypeStruct((t, n), BF16),
        compiler_params=_params("parallel", "arbitrary"),
        name="gated_merge",
    )(attn, y, proj, proj, w_up, w_v, w_g)


def _outproj_kernel(m_ref, w_ref, x_ref, g_ref, o_ref):
    z = jnp.dot(m_ref[...], w_ref[...], preferred_element_type=F32)
    o_ref[...] = x_ref[...] + _rms(z, g_ref[...])


def _outproj(merged, w, x, gain, *, tm):
    t, d = x.shape
    return pl.pallas_call(
        _outproj_kernel,
        grid=(t // tm,),
        in_specs=[
            pl.BlockSpec((tm, merged.shape[1]), lambda i: (i, 0)),
            pl.BlockSpec(w.shape, lambda i: (0, 0)),
            pl.BlockSpec((tm, d), lambda i: (i, 0)),
            pl.BlockSpec((1, d), lambda i: (0, 0)),
        ],
        out_specs=pl.BlockSpec((tm, d), lambda i: (i, 0)),
        out_shape=jax.ShapeDtypeStruct((t, d), F32),
        compiler_params=_params("parallel"),
        name="outproj_norm_residual",
    )(merged, w, x, gain.reshape(1, d))


def _ffn_kernel(x_ref, gpre_ref, gpost_ref, wg_ref, wu_ref, wd_ref, o_ref, h_ref, acc_ref):
    k = pl.program_id(1)

    @pl.when(k == 0)
    def _():
        h_ref[...] = _rms(x_ref[...], gpre_ref[...]).astype(BF16)
        acc_ref[...] = jnp.zeros_like(acc_ref)

    h = h_ref[...]
    gate = jnp.dot(h, wg_ref[...], preferred_element_type=F32)
    up = jnp.dot(h, wu_ref[...], preferred_element_type=F32)
    f = (jax.nn.silu(gate) * up).astype(BF16)
    acc_ref[...] += jnp.dot(f, wd_ref[...], preferred_element_type=F32)

    @pl.when(k == pl.num_programs(1) - 1)
    def _():
        o_ref[...] = x_ref[...] + _rms(acc_ref[...], gpost_ref[...])


def _ffn(x, gain_pre, gain_post, w_gate, w_up, w_down, *, tm, tf):
    t, d = x.shape
    dff = w_gate.shape[1]
    return pl.pallas_call(
        _ffn_kernel,
        grid=(t // tm, dff // tf),
        in_specs=[
            pl.BlockSpec((tm, d), lambda i, k: (i, 0)),
            pl.BlockSpec((1, d), lambda i, k: (0, 0)),
            pl.BlockSpec((1, d), lambda i, k: (0, 0)),
            pl.BlockSpec((d, tf), lambda i, k: (0, k)),
            pl.BlockSpec((d, tf), lambda i, k: (0, k)),
            pl.BlockSpec((tf, d), lambda i, k: (k, 0)),
        ],
        out_specs=pl.BlockSpec((tm, d), lambda i, k: (i, 0)),
        out_shape=jax.ShapeDtypeStruct((t, d), F32),
        scratch_shapes=[pltpu.VMEM((tm, d), BF16), pltpu.VMEM((tm, d), F32)],
        compiler_params=_params("parallel", "arbitrary"),
        name="swiglu_ffn",
    )(x, gain_pre.reshape(1, d), gain_post.reshape(1, d), w_gate, w_up, w_down)


def _layer(x, norm_mix_pre, w_in, w_attn_up, ssm_a_re, ssm_a_im, ssm_log_dt, ssm_b_re, ssm_b_im,
           ssm_c_re, ssm_c_im, ssm_d, w_glu_v, w_glu_g, w_out, norm_mix_post, norm_ffn_pre,
           w_ffn_gate, w_ffn_up, w_ffn_down, norm_ffn_post):
    bsz, seq, d = x.shape
    t = bsz * seq
    ssm_width = ssm_d.shape[0]
    u_col0 = 3 * N_HEADS * HEAD_DIM
    assert w_in.shape[1] == u_col0 + ssm_width + 2 * d

    x2 = x.reshape(t, d)
    tn = GROUP_WIDTH
    proj = _inproj(x2, norm_mix_pre, w_in.astype(BF16), ssm_width)
    attn = _attention(proj, bsz, seq)
    ssm_w = _ssm_weights(ssm_a_re, ssm_a_im, ssm_log_dt, ssm_b_re, ssm_b_im, ssm_c_re, ssm_c_im)
    y = _ssm(proj, u_col0 // tn, ssm_w, ssm_d, bsz, seq, ptiles=4, nb=2)
    merged = _merge(attn, y, proj, (u_col0 + ssm_width) // tn, w_attn_up, w_glu_v, w_glu_g)
    x1 = _outproj(merged, w_out.astype(BF16), x2, norm_mix_post, tm=512)
    out = _ffn(x1, norm_ffn_pre, norm_ffn_post, w_ffn_gate.astype(BF16), w_ffn_up.astype(BF16),
               w_ffn_down.astype(BF16), tm=512, tf=512)
    return out.reshape(bsz, seq, d)


def kernel(x, norm_mix_pre, w_in, w_attn_up, ssm_a_re, ssm_a_im, ssm_log_dt, ssm_b_re, ssm_b_im, ssm_c_re, ssm_c_im, ssm_d, w_glu_v, w_glu_g, w_out, norm_mix_post, norm_ffn_pre, w_ffn_gate, w_ffn_up, w_ffn_down, norm_ffn_post):
    stacked = (norm_mix_pre, w_in, w_attn_up, ssm_a_re, ssm_a_im, ssm_log_dt, ssm_b_re, ssm_b_im, ssm_c_re,
               ssm_c_im, ssm_d, w_glu_v, w_glu_g, w_out, norm_mix_post, norm_ffn_pre, w_ffn_gate, w_ffn_up,
               w_ffn_down, norm_ffn_post)
    for layer in range(norm_mix_pre.shape[0]):
        x = _layer(x, *(p[layer] for p in stacked))
    return x
```

```python
import functools

import jax
import jax.numpy as jnp
import numpy as np
from jax import lax
from jax.experimental import pallas as pl
from jax.experimental.pallas import tpu as pltpu

F32 = jnp.float32
BF16 = jnp.bfloat16

EPS = 1e-6
HEAD_DIM = 128
HEADS_PER_GROUP = 4
ATTN_GROUPS = ((128, 1), (512, 4), (2048, 16))
N_GROUPS = len(ATTN_GROUPS)
N_HEADS = HEADS_PER_GROUP * N_GROUPS
GROUP_WIDTH = HEADS_PER_GROUP * HEAD_DIM
ATTN_BLK = 128
ATTN_TILE = 2048
SSM_GROUP = 16
SSM_STATE = 64
SSM_CHUNK = 8
LANES = 128
GROUPS_PER_LANE_BLOCK = LANES // SSM_GROUP
PROJ_TILE = 1024
ROW_ORDERS = tuple(dil for _, dil in ATTN_GROUPS) + (SSM_CHUNK,)
PERM_BASE = 4
NEG = -1e30
LOG2E = 1.4426950408889634
VMEM_LIMIT = 56 * 1024 * 1024


def _params(*sem):
    return pltpu.CompilerParams(dimension_semantics=sem, vmem_limit_bytes=VMEM_LIMIT)


def _sigmoid(x):
    return 0.5 * jnp.tanh(0.5 * x) + 0.5


def _rms(x, gain):
    return x * lax.rsqrt(jnp.mean(x * x, axis=-1, keepdims=True) + EPS) * gain


def _residue_block(r, dil):
    if dil <= PERM_BASE:
        return r
    return (r % PERM_BASE) * (dil // PERM_BASE) + r // PERM_BASE


def _inproj_kernel(x_ref, g_ref, w_ref, o_ref, h_ref, hn_ref, hb_ref, inv_ref, *, n_qkv, u_tiles):
    j = pl.program_id(1)
    slabs, tm, _ = hn_ref.shape

    @pl.when(j == 0)
    def _():
        x = x_ref[...]
        inv_ref[...] = jnp.broadcast_to(lax.rsqrt(jnp.mean(x * x, axis=-1, keepdims=True) + EPS), inv_ref.shape)
        q = tm // PERM_BASE
        piece = 256
        for c in range(x_ref.shape[1] // LANES):
            cols = slice(c * LANES, (c + 1) * LANES)
            s = c % slabs
            for r0 in range(0, tm, piece):
                rows = slice(r0, r0 + piece)
                hn = x_ref[rows, cols] * inv_ref[rows, :] * g_ref[:, cols]
                h_ref[0, rows, cols] = hn.astype(BF16)
                hn_ref[s, rows, :] = hn
            for b in range(PERM_BASE):
                part = hn_ref[s, pl.ds(b, q, stride=PERM_BASE), :]
                hb_ref[s, b * q:(b + 1) * q, :] = part
                for v, dil in enumerate(ROW_ORDERS):
                    if dil == PERM_BASE:
                        h_ref[v, b * q:(b + 1) * q, cols] = part.astype(BF16)
            for v, dil in enumerate(ROW_ORDERS):
                if dil > PERM_BASE:
                    k, n = dil // PERM_BASE, tm // dil
                    for b in range(PERM_BASE):
                        for a in range(k):
                            blk = b * k + a
                            h_ref[v, blk * n:(blk + 1) * n, cols] = (
                                hb_ref[s, pl.ds(b * q + a, n, stride=k), :].astype(BF16))

    order = jnp.where(j < n_qkv, j % N_GROUPS, jnp.where(j < n_qkv + u_tiles, N_GROUPS, 0))
    o_ref[...] = jnp.dot(h_ref[order], w_ref[...], preferred_element_type=F32).astype(o_ref.dtype)


def _inproj(x, gain, w, ssm_width):
    t, d = x.shape
    tm = PROJ_TILE
    tn = GROUP_WIDTH
    ncol = w.shape[1] // tn
    assert t % tm == 0 and w.shape[1] % tn == 0 and ssm_width % tn == 0 and d % LANES == 0
    assert ROW_ORDERS[0] == 1 and all(tm % (dil * 16) == 0 for dil in ROW_ORDERS)
    assert all(dil in (1, PERM_BASE) or (dil % PERM_BASE == 0 and dil // PERM_BASE <= PERM_BASE) for dil in ROW_ORDERS)
    slabs = 8
    return pl.pallas_call(
        functools.partial(_inproj_kernel, n_qkv=3 * N_GROUPS, u_tiles=ssm_width // tn),
        grid=(t // tm, ncol),
        in_specs=[
            pl.BlockSpec((tm, d), lambda i, j: (i, 0)),
            pl.BlockSpec((1, d), lambda i, j: (0, 0)),
            pl.BlockSpec((d, tn), lambda i, j: (0, j)),
        ],
        out_specs=pl.BlockSpec((None, None, tm, tn), lambda i, j: (i, j, 0, 0)),
        out_shape=jax.ShapeDtypeStruct((t // tm, ncol, tm, tn), BF16),
        scratch_shapes=[
            pltpu.VMEM((len(ROW_ORDERS), tm, d), BF16),
            pltpu.VMEM((slabs, tm, LANES), F32),
            pltpu.VMEM((slabs, tm, LANES), F32),
            pltpu.VMEM((tm, LANES), F32),
        ],
        compiler_params=_params("parallel", "arbitrary"),
        name="inproj",
    )(x, gain.reshape(1, d), w)


def _attn_bias_table():
    qi = np.arange(ATTN_BLK)[:, None]
    kj = np.arange(ATTN_BLK)[None, :]
    table = np.full((N_GROUPS, 2, HEADS_PER_GROUP * ATTN_BLK, 2 * ATTN_BLK), NEG, np.float32)
    for g, (_, dil) in enumerate(ATTN_GROUPS):
        for h in range(HEADS_PER_GROUP):
            slope = 2.0 ** (-8.0 * (g * HEADS_PER_GROUP + h + 1) / N_HEADS) * dil * LOG2E
            rows = slice(h * ATTN_BLK, (h + 1) * ATTN_BLK)
            cur = np.where(kj <= qi, -slope * (qi - kj), NEG)
            prev = np.where(kj >= qi, -slope * (ATTN_BLK + qi - kj), NEG)
            table[g, :, rows, ATTN_BLK:] = cur
            table[g, 0, rows, :ATTN_BLK] = prev
    return table


def _attn_kernel(bias_ref, *refs):
    qkv_refs = refs[:3 * N_GROUPS]
    o_ref, out_scr, lse_scr = refs[3 * N_GROUPS:3 * N_GROUPS + 3]
    hist_refs = refs[3 * N_GROUPS + 3:]
    tile = pl.program_id(1)
    u = pl.program_id(2)
    units = ATTN_TILE // ATTN_BLK
    nt = (((1,), (1,)), ((), ()))

    @pl.when(jnp.logical_and(tile == 0, u == 0))
    def _():
        for hist in hist_refs:
            hist[...] = jnp.zeros_like(hist)

    for g, (_, dil) in enumerate(ATTN_GROUPS):
        hist = hist_refs[g]
        nb = u // dil
        r = u % dil
        first = jnp.logical_and(tile == 0, nb == 0).astype(jnp.int32)
        start = nb * (ATTN_BLK * dil) + r
        rows = pl.ds(start, ATTN_BLK) if dil == 1 else pl.ds(start, ATTN_BLK, stride=dil)
        q, k, v = (ref[...].reshape(ATTN_BLK, GROUP_WIDTH) for ref in qkv_refs[3 * g:3 * g + 3])
        prev = hist[r]
        scores = []
        for h in range(HEADS_PER_GROUP):
            cs = slice(h * HEAD_DIM, (h + 1) * HEAD_DIM)
            keys = jnp.concatenate([prev[:, cs], k[:, cs]], axis=0)
            scores.append(lax.dot_general(q[:, cs], keys, nt, preferred_element_type=F32))
        s = jnp.concatenate(scores, axis=0) * (HEAD_DIM ** -0.5 * LOG2E) + bias_ref[g, first]
        m = jnp.max(s, axis=1, keepdims=True)
        p = jnp.exp2(s - m)
        l = jnp.sum(p, axis=1, keepdims=True)
        p = p.astype(BF16)
        inv = 1.0 / l
        lse = m + jnp.log2(l)
        for h in range(HEADS_PER_GROUP):
            cs = slice(h * HEAD_DIM, (h + 1) * HEAD_DIM)
            hr = slice(h * ATTN_BLK, (h + 1) * ATTN_BLK)
            vals = jnp.concatenate([prev[:, GROUP_WIDTH:][:, cs], v[:, cs]], axis=0)
            o = jnp.dot(p[hr], vals, preferred_element_type=F32)
            out_scr[g, h, rows, :] = o * inv[hr]
            lse_scr[g, h, rows, :] = jnp.broadcast_to(lse[hr], (ATTN_BLK, HEAD_DIM))
        hist[r, :, :GROUP_WIDTH] = k
        hist[r, :, GROUP_WIDTH:] = v

    @pl.when(u == units - 1)
    def _():
        for h in range(HEADS_PER_GROUP):
            lses = [lse_scr[g, h] for g in range(N_GROUPS)]
            top = functools.reduce(jnp.maximum, lses)
            ws = [jnp.exp2(x - top) for x in lses]
            num = sum(w * out_scr[g, h] for g, w in enumerate(ws))
            o_ref[:, h * HEAD_DIM:(h + 1) * HEAD_DIM] = (num / sum(ws)).astype(o_ref.dtype)


def _attention(proj, bsz, seq):
    assert seq % ATTN_TILE == 0 and ATTN_TILE % PROJ_TILE == 0
    tiles = seq // ATTN_TILE
    units = ATTN_TILE // ATTN_BLK
    bias = jnp.asarray(_attn_bias_table())
    in_specs = [pl.BlockSpec(bias.shape, lambda b, tile, u: (0, 0, 0, 0))]
    hist = []
    for g, (window, dil) in enumerate(ATTN_GROUPS):
        assert window // dil == ATTN_BLK and units % dil == 0
        span = ATTN_BLK * dil
        spans_per_tile = ATTN_TILE // span
        if span <= PROJ_TILE:
            block = (None, None, ATTN_BLK, GROUP_WIDTH)

            def index(b, tile, u, *, col, dil=dil, spt=spans_per_tile, sppt=PROJ_TILE // span,
                      per_seq=seq // PROJ_TILE):
                sp = tile * spt + u // dil
                return b * per_seq + sp // sppt, col, _residue_block(u % dil, dil) * sppt + sp % sppt, 0
        else:
            block = (span // PROJ_TILE, None, PROJ_TILE // dil, GROUP_WIDTH)

            def index(b, tile, u, *, col, dil=dil, spt=spans_per_tile, per_seq=seq // span):
                return b * per_seq + tile * spt + u // dil, col, _residue_block(u % dil, dil), 0

        for part in range(3):
            in_specs.append(pl.BlockSpec(block, functools.partial(index, col=part * N_GROUPS + g)))
        hist.append(pltpu.VMEM((dil, ATTN_BLK, 2 * GROUP_WIDTH), BF16))
    scratch = pltpu.VMEM((N_GROUPS, HEADS_PER_GROUP, ATTN_TILE, HEAD_DIM), F32)
    return pl.pallas_call(
        _attn_kernel,
        grid=(bsz, tiles, units),
        in_specs=in_specs,
        out_specs=pl.BlockSpec((ATTN_TILE, GROUP_WIDTH), lambda b, tile, u: (b * tiles + tile, 0)),
        out_shape=jax.ShapeDtypeStruct((bsz * seq, GROUP_WIDTH), BF16),
        scratch_shapes=[scratch, scratch] + hist,
        compiler_params=_params("arbitrary", "arbitrary", "arbitrary"),
        name="dilated_attention",
    )(bias, *([proj] * (3 * N_GROUPS)))


def _cmul(a, b):
    return a[0] * b[0] - a[1] * b[1], a[0] * b[1] + a[1] * b[0]


def _ssm_weights(a_re, a_im, log_dt, b_re, b_im, c_re, c_im):
    n_groups = a_re.shape[0]
    nblk = n_groups // GROUPS_PER_LANE_BLOCK
    gl = GROUPS_PER_LANE_BLOCK
    L = SSM_CHUNK
    a_re, a_im = a_re.astype(F32), a_im.astype(F32)
    dt = jnp.exp(log_dt.astype(F32))[:, None]
    steps = jnp.arange(L + 1, dtype=F32)[None, :, None]
    mag = jnp.exp((a_re * dt)[:, None, :] * steps)
    ang = (a_im * dt)[:, None, :] * steps
    powers = (mag * jnp.cos(ang), mag * jnp.sin(ang))
    lam_bar = (powers[0][:, 1], powers[1][:, 1])
    den = a_re * a_re + a_im * a_im
    num = (lam_bar[0] - 1.0, lam_bar[1])
    ratio = ((num[0] * a_re + num[1] * a_im) / den, (num[1] * a_re - num[0] * a_im) / den)
    b_bar = _cmul((ratio[0][..., None], ratio[1][..., None]), (b_re.astype(F32), b_im.astype(F32)))
    c_t = (c_re.astype(F32).transpose(0, 2, 1), c_im.astype(F32).transpose(0, 2, 1))
    eye = jnp.eye(gl, dtype=F32)

    def block_diag(m):
        rows, cols = m.shape[1:]
        m = m.reshape(nblk, gl, rows, 1, cols) * eye[None, :, None, :, None]
        return m.reshape(nblk, gl * rows, gl * cols)

    b_in = jnp.stack([block_diag(b.transpose(0, 2, 1)) for b in b_bar], axis=1)
    c_out = jnp.stack([block_diag(c) for c in c_t], axis=1)
    pw = jnp.stack(powers, axis=0).reshape(2, nblk, gl, L + 1, SSM_STATE)
    pw_row = pw.transpose(1, 0, 3, 2, 4).reshape(nblk, 2, L + 1, gl * SSM_STATE)
    pw_col = pw_row.transpose(0, 1, 3, 2)
    return b_in, c_out, pw_row, pw_col


def _ssm_kernel(*refs, tiles_per_seq):
    u_refs = refs[:SSM_CHUNK]
    (bin_ref, cout_ref, pwr_ref, pwc_ref, d_ref, y_ref,
     wt_scr, wb_scr, wc_scr, s_scr, xp_scr, carry_scr, y_scr) = refs[SSM_CHUNK:]
    L = SSM_CHUNK
    nb = wt_scr.shape[0]
    tc = s_scr.shape[0]
    half = s_scr.shape[1] // 2
    sw = half // nb

    @pl.when(pl.program_id(1) == 0)
    def _():
        hi = lax.Precision.HIGHEST
        for q in range(nb):
            for j in range(L):
                rows = slice(j * LANES, (j + 1) * LANES)
                n = L - 1 - j
                pr = pwr_ref[q, 0, n:n + 1, :]
                pi = pwr_ref[q, 1, n:n + 1, :]
                qr = bin_ref[q, 0] * pr - bin_ref[q, 1] * pi
                qi = bin_ref[q, 0] * pi + bin_ref[q, 1] * pr
                wb_scr[q, rows, :sw] = qr.astype(BF16)
                wb_scr[q, rows, sw:] = qi.astype(BF16)
                kn = (jnp.dot(qr, cout_ref[q, 0], precision=hi, preferred_element_type=F32)
                      - jnp.dot(qi, cout_ref[q, 1], precision=hi, preferred_element_type=F32)).astype(BF16)
                for jj in range(L):
                    ii = jj + n
                    if ii < L:
                        wt_scr[q, jj * LANES:(jj + 1) * LANES, ii * LANES:(ii + 1) * LANES] = kn
                    if j < jj:
                        wt_scr[q, jj * LANES:(jj + 1) * LANES, rows] = jnp.zeros((LANES, LANES), BF16)
                pr = pwc_ref[q, 0, :, j + 1:j + 2]
                pi = pwc_ref[q, 1, :, j + 1:j + 2]
                wc_scr[q, :sw, rows] = (cout_ref[q, 0] * pr - cout_ref[q, 1] * pi).astype(BF16)
                wc_scr[q, sw:, rows] = (-(cout_ref[q, 0] * pi + cout_ref[q, 1] * pr)).astype(BF16)

    @pl.when(pl.program_id(1) % tiles_per_seq == 0)
    def _():
        carry_scr[...] = jnp.zeros_like(carry_scr)

    us = [r[...].reshape(tc, nb * LANES) for r in u_refs]
    ucat = [jnp.concatenate([u[:, q * LANES:(q + 1) * LANES] for u in us], axis=1)
            for q in range(nb)]
    for q in range(nb):
        s = jnp.dot(ucat[q], wb_scr[q], preferred_element_type=F32)
        s_scr[:, q * sw:(q + 1) * sw] = s[:, :sw]
        s_scr[:, half + q * sw:half + (q + 1) * sw] = s[:, sw:]
    for q in range(nb):
        re = slice(q * sw, (q + 1) * sw)
        im = slice(half + q * sw, half + (q + 1) * sw)
        ar = pwr_ref[q, 0, L:L + 1, :]
        ai = pwr_ref[q, 1, L:L + 1, :]

        def step(c, carry, re=re, im=im, ar=ar, ai=ai):
            xr, xi = carry
            xp_scr[pl.ds(c, 1), re] = xr
            xp_scr[pl.ds(c, 1), im] = xi
            return (ar * xr - ai * xi + s_scr[pl.ds(c, 1), re], ar * xi + ai * xr + s_scr[pl.ds(c, 1), im])

        xr, xi = lax.fori_loop(0, tc, step, (carry_scr[:, re], carry_scr[:, im]))
        carry_scr[:, re] = xr
        carry_scr[:, im] = xi

    for q in range(nb):
        lanes = slice(q * LANES, (q + 1) * LANES)
        xq = jnp.concatenate([xp_scr[:, q * sw:(q + 1) * sw], xp_scr[:, half + q * sw:half + (q + 1) * sw]], axis=1)
        y = jnp.dot(ucat[q], wt_scr[q], preferred_element_type=F32)
        y = y + jnp.dot(xq.astype(BF16), wc_scr[q], preferred_element_type=F32)
        for i in range(L):
            yi = y[:, i * LANES:(i + 1) * LANES] + d_ref[0, :, lanes] * us[i][:, lanes].astype(F32)
            y_scr[q, pl.ds(i, tc, stride=L), :] = jax.nn.gelu(yi)
        y_ref[:, lanes] = y_scr[q].astype(y_ref.dtype)


def _ssm(proj, u_tile0, ssm_w, d_skip, bsz, seq, *, ptiles, nb):
    nt, _, tm, tn = proj.shape
    L = SSM_CHUNK
    per_tile = tn // (nb * LANES)
    crows = tm // L
    nblk = ssm_w[0].shape[0]
    width = nblk * LANES
    assert crows == LANES and nblk % nb == 0 and tn % (nb * LANES) == 0
    tc = ptiles * crows
    rows_per_seq = seq // L
    assert rows_per_seq % tc == 0 and nt % ptiles == 0
    u_specs = [
        pl.BlockSpec((ptiles, None, crows, nb * LANES), functools.partial(
            lambda blk, i, j: (i, u_tile0 + blk // per_tile, j, blk % per_tile), j=_residue_block(j, L)))
        for j in range(L)
    ]
    w_specs = [pl.BlockSpec((nb,) + w.shape[1:], lambda blk, i: (blk, 0, 0, 0)) for w in ssm_w]
    wide = L * LANES
    states = 2 * GROUPS_PER_LANE_BLOCK * SSM_STATE
    return pl.pallas_call(
        functools.partial(_ssm_kernel, tiles_per_seq=rows_per_seq // tc),
        grid=(nblk // nb, nt // ptiles),
        in_specs=u_specs + w_specs + [pl.BlockSpec((1, 1, nb * LANES), lambda blk, i: (blk, 0, 0))],
        out_specs=pl.BlockSpec((tc * L, nb * LANES), lambda blk, i: (i, blk)),
        out_shape=jax.ShapeDtypeStruct((nt * tm, width), BF16),
        scratch_shapes=[
            pltpu.VMEM((nb, wide, wide), BF16),
            pltpu.VMEM((nb, wide, states), BF16),
            pltpu.VMEM((nb, states, wide), BF16),
            pltpu.VMEM((tc, nb * states), F32),
            pltpu.VMEM((tc, nb * states), F32),
            pltpu.VMEM((1, nb * states), F32),
            pltpu.VMEM((nb, tc * L, LANES), F32),
        ],
        compiler_params=_params("parallel", "arbitrary"),
        name="s5_chunked",
    )(*([proj] * L), *ssm_w, d_skip.astype(F32).reshape(nblk // nb, 1, nb * LANES))


def _merge_kernel(attn_ref, y_ref, ga_ref, gs_ref, wup_ref, wv_ref, wg_ref, o_ref):
    y = y_ref[...]
    gate = _sigmoid(jnp.dot(y, wg_ref[...].astype(BF16), preferred_element_type=F32))
    gate_s = _sigmoid(gs_ref[...].astype(F32))
    gate_a = _sigmoid(ga_ref[...].astype(F32))
    val = jnp.dot(y, wv_ref[...].astype(BF16), preferred_element_type=F32)
    attn_branch = jnp.dot(attn_ref[...], wup_ref[...].astype(BF16), preferred_element_type=F32)
    merged = gate_a * attn_branch + gate_s * (val * gate)
    o_ref[...] = merged.astype(o_ref.dtype)


def _merge(attn, y, proj, gate_tile0, w_up, w_v, w_g):
    t = attn.shape[0]
    n = w_up.shape[1]
    _, _, tm, tn = proj.shape
    assert n % tn == 0
    return pl.pallas_call(
        _merge_kernel,
        grid=(t // tm, n // tn),
        in_specs=[
            pl.BlockSpec((tm, attn.shape[1]), lambda i, j: (i, 0)),
            pl.BlockSpec((tm, y.shape[1]), lambda i, j: (i, 0)),
            pl.BlockSpec((None, None, tm, tn), lambda i, j: (i, gate_tile0 + j, 0, 0)),
            pl.BlockSpec((None, None, tm, tn), lambda i, j: (i, gate_tile0 + n // tn + j, 0, 0)),
            pl.BlockSpec((w_up.shape[0], tn), lambda i, j: (0, j)),
            pl.BlockSpec((w_v.shape[0], tn), lambda i, j: (0, j)),
            pl.BlockSpec((w_g.shape[0], tn), lambda i, j: (0, j)),
        ],
        out_specs=pl.BlockSpec((tm, tn), lambda i, j: (i, j)),
        out_shape=jax.ShapeDtypeStruct((t, n), BF16),
        compiler_params=_params("parallel", "arbitrary"),
        name="gated_merge",
    )(attn, y, proj, proj, w_up, w_v, w_g)


def _outproj_kernel(m_ref, w_ref, x_ref, g_ref, o_ref):
    z = jnp.dot(m_ref[...], w_ref[...], preferred_element_type=F32)
    o_ref[...] = x_ref[...] + _rms(z, g_ref[...])


def _outproj(merged, w, x, gain, *, tm):
    t, d = x.shape
    return pl.pallas_call(
        _outproj_kernel,
        grid=(t // tm,),
        in_specs=[
            pl.BlockSpec((tm, merged.shape[1]), lambda i: (i, 0)),
            pl.BlockSpec(w.shape, lambda i: (0, 0)),
            pl.BlockSpec((tm, d), lambda i: (i, 0)),
            pl.BlockSpec((1, d), lambda i: (0, 0)),
        ],
        out_specs=pl.BlockSpec((tm, d), lambda i: (i, 0)),
        out_shape=jax.ShapeDtypeStruct((t, d), F32),
        compiler_params=_params("parallel"),
        name="outproj_norm_residual",
    )(merged, w, x, gain.reshape(1, d))


def _ffn_kernel(x_ref, gpre_ref, gpost_ref, wg_ref, wu_ref, wd_ref, o_ref, h_ref, acc_ref):
    k = pl.program_id(1)

    @pl.when(k == 0)
    def _():
        h_ref[...] = _rms(x_ref[...], gpre_ref[...]).astype(BF16)
        acc_ref[...] = jnp.zeros_like(acc_ref)

    h = h_ref[...]
    gate = jnp.dot(h, wg_ref[...], preferred_element_type=F32)
    up = jnp.dot(h, wu_ref[...], preferred_element_type=F32)
    f = (jax.nn.silu(gate) * up).astype(BF16)
    acc_ref[...] += jnp.dot(f, wd_ref[...], preferred_element_type=F32)

    @pl.when(k == pl.num_programs(1) - 1)
    def _():
        o_ref[...] = x_ref[...] + _rms(acc_ref[...], gpost_ref[...])


def _ffn(x, gain_pre, gain_post, w_gate, w_up, w_down, *, tm, tf):
    t, d = x.shape
    dff = w_gate.shape[1]
    return pl.pallas_call(
        _ffn_kernel,
        grid=(t // tm, dff // tf),
        in_specs=[
            pl.BlockSpec((tm, d), lambda i, k: (i, 0)),
            pl.BlockSpec((1, d), lambda i, k: (0, 0)),
            pl.BlockSpec((1, d), lambda i, k: (0, 0)),
            pl.BlockSpec((d, tf), lambda i, k: (0, k)),
            pl.BlockSpec((d, tf), lambda i, k: (0, k)),
            pl.BlockSpec((tf, d), lambda i, k: (k, 0)),
        ],
        out_specs=pl.BlockSpec((tm, d), lambda i, k: (i, 0)),
        out_shape=jax.ShapeDtypeStruct((t, d), F32),
        scratch_shapes=[pltpu.VMEM((tm, d), BF16), pltpu.VMEM((tm, d), F32)],
        compiler_params=_params("parallel", "arbitrary"),
        name="swiglu_ffn",
    )(x, gain_pre.reshape(1, d), gain_post.reshape(1, d), w_gate, w_up, w_down)


def _layer(x, norm_mix_pre, w_in, w_attn_up, ssm_a_re, ssm_a_im, ssm_log_dt, ssm_b_re, ssm_b_im,
           ssm_c_re, ssm_c_im, ssm_d, w_glu_v, w_glu_g, w_out, norm_mix_post, norm_ffn_pre,
           w_ffn_gate, w_ffn_up, w_ffn_down, norm_ffn_post):
    bsz, seq, d = x.shape
    t = bsz * seq
    ssm_width = ssm_d.shape[0]
    u_col0 = 3 * N_HEADS * HEAD_DIM
    assert w_in.shape[1] == u_col0 + ssm_width + 2 * d

    x2 = x.reshape(t, d)
    tn = GROUP_WIDTH
    proj = _inproj(x2, norm_mix_pre, w_in.astype(BF16), ssm_width)
    attn = _attention(proj, bsz, seq)
    ssm_w = _ssm_weights(ssm_a_re, ssm_a_im, ssm_log_dt, ssm_b_re, ssm_b_im, ssm_c_re, ssm_c_im)
    y = _ssm(proj, u_col0 // tn, ssm_w, ssm_d, bsz, seq, ptiles=4, nb=2)
    merged = _merge(attn, y, proj, (u_col0 + ssm_width) // tn, w_attn_up, w_glu_v, w_glu_g)
    x1 = _outproj(merged, w_out.astype(BF16), x2, norm_mix_post, tm=512)
    out = _ffn(x1, norm_ffn_pre, norm_ffn_post, w_ffn_gate.astype(BF16), w_ffn_up.astype(BF16),
               w_ffn_down.astype(BF16), tm=512, tf=512)
    return out.reshape(bsz, seq, d)


def kernel(x, norm_mix_pre, w_in, w_attn_up, ssm_a_re, ssm_a_im, ssm_log_dt, ssm_b_re, ssm_b_im, ssm_c_re, ssm_c_im, ssm_d, w_glu_v, w_glu_g, w_out, norm_mix_post, norm_ffn_pre, w_ffn_gate, w_ffn_up, w_ffn_down, norm_ffn_post):
    stacked = (norm_mix_pre, w_in, w_attn_up, ssm_a_re, ssm_a_im, ssm_log_dt, ssm_b_re, ssm_b_im, ssm_c_re,
               ssm_c_im, ssm_d, w_glu_v, w_glu_g, w_out, norm_mix_post, norm_ffn_pre, w_ffn_gate, w_ffn_up,
               w_ffn_down, norm_ffn_post)
    for layer in range(norm_mix_pre.shape[0]):
        x = _layer(x, *(p[layer] for p in stacked))
    return x
```

```python
import functools

import jax
import jax.numpy as jnp
import numpy as np
from jax import lax
from jax.experimental import pallas as pl
from jax.experimental.pallas import tpu as pltpu

F32 = jnp.float32
BF16 = jnp.bfloat16

EPS = 1e-6
HEAD_DIM = 128
HEADS_PER_GROUP = 4
ATTN_GROUPS = ((128, 1), (512, 4), (2048, 16))
N_GROUPS = len(ATTN_GROUPS)
N_HEADS = HEADS_PER_GROUP * N_GROUPS
GROUP_WIDTH = HEADS_PER_GROUP * HEAD_DIM
ATTN_BLK = 128
ATTN_TILE = 2048
SSM_GROUP = 16
SSM_STATE = 64
SSM_CHUNK = 8
LANES = 128
GROUPS_PER_LANE_BLOCK = LANES // SSM_GROUP
PROJ_TILE = 1024
ROW_ORDERS = tuple(dil for _, dil in ATTN_GROUPS) + (SSM_CHUNK,)
PERM_BASE = 4
NEG = -1e30
LOG2E = 1.4426950408889634
VMEM_LIMIT = 56 * 1024 * 1024


def _params(*sem):
    return pltpu.CompilerParams(dimension_semantics=sem, vmem_limit_bytes=VMEM_LIMIT)


def _sigmoid(x):
    return 0.5 * jnp.tanh(0.5 * x) + 0.5


def _rms(x, gain):
    return x * lax.rsqrt(jnp.mean(x * x, axis=-1, keepdims=True) + EPS) * gain


def _residue_block(r, dil):
    if dil <= PERM_BASE:
        return r
    return (r % PERM_BASE) * (dil // PERM_BASE) + r // PERM_BASE


def _inproj_kernel(x_ref, g_ref, w_ref, o_ref, h_ref, hn_ref, hb_ref, inv_ref, *, n_qkv, u_tiles):
    j = pl.program_id(1)
    slabs, tm, _ = hn_ref.shape

    @pl.when(j == 0)
    def _():
        x = x_ref[...]
        inv_ref[...] = jnp.broadcast_to(lax.rsqrt(jnp.mean(x * x, axis=-1, keepdims=True) + EPS), inv_ref.shape)
        q = tm // PERM_BASE
        piece = 256
        for c in range(x_ref.shape[1] // LANES):
            cols = slice(c * LANES, (c + 1) * LANES)
            s = c % slabs
            for r0 in range(0, tm, piece):
                rows = slice(r0, r0 + piece)
                hn = x_ref[rows, cols] * inv_ref[rows, :] * g_ref[:, cols]
                h_ref[0, rows, cols] = hn.astype(BF16)
                hn_ref[s, rows, :] = hn
            for b in range(PERM_BASE):
                part = hn_ref[s, pl.ds(b, q, stride=PERM_BASE), :]
                hb_ref[s, b * q:(b + 1) * q, :] = part
                for v, dil in enumerate(ROW_ORDERS):
                    if dil == PERM_BASE:
                        h_ref[v, b * q:(b + 1) * q, cols] = part.astype(BF16)
            for v, dil in enumerate(ROW_ORDERS):
                if dil > PERM_BASE:
                    k, n = dil // PERM_BASE, tm // dil
                    for b in range(PERM_BASE):
                        for a in range(k):
                            blk = b * k + a
                            h_ref[v, blk * n:(blk + 1) * n, cols] = (
                                hb_ref[s, pl.ds(b * q + a, n, stride=k), :].astype(BF16))

    order = jnp.where(j < n_qkv, j % N_GROUPS, jnp.where(j < n_qkv + u_tiles, N_GROUPS, 0))
    o_ref[...] = jnp.dot(h_ref[order], w_ref[...], preferred_element_type=F32).astype(o_ref.dtype)


def _inproj(x, gain, w, ssm_width):
    t, d = x.shape
    tm = PROJ_TILE
    tn = GROUP_WIDTH
    ncol = w.shape[1] // tn
    assert t % tm == 0 and w.shape[1] % tn == 0 and ssm_width % tn == 0 and d % LANES == 0
    assert ROW_ORDERS[0] == 1 and all(tm % (dil * 16) == 0 for dil in ROW_ORDERS)
    assert all(dil in (1, PERM_BASE) or (dil % PERM_BASE == 0 and dil // PERM_BASE <= PERM_BASE) for dil in ROW_ORDERS)
    slabs = 8
    return pl.pallas_call(
        functools.partial(_inproj_kernel, n_qkv=3 * N_GROUPS, u_tiles=ssm_width // tn),
        grid=(t // tm, ncol),
        in_specs=[
            pl.BlockSpec((tm, d), lambda i, j: (i, 0)),
            pl.BlockSpec((1, d), lambda i, j: (0, 0)),
            pl.BlockSpec((d, tn), lambda i, j: (0, j)),
        ],
        out_specs=pl.BlockSpec((None, None, tm, tn), lambda i, j: (i, j, 0, 0)),
        out_shape=jax.ShapeDtypeStruct((t // tm, ncol, tm, tn), BF16),
        scratch_shapes=[
            pltpu.VMEM((len(ROW_ORDERS), tm, d), BF16),
            pltpu.VMEM((slabs, tm, LANES), F32),
            pltpu.VMEM((slabs, tm, LANES), F32),
            pltpu.VMEM((tm, LANES), F32),
        ],
        compiler_params=_params("parallel", "arbitrary"),
        name="inproj",
    )(x, gain.reshape(1, d), w)


def _attn_bias_table():
    qi = np.arange(ATTN_BLK)[:, None]
    kj = np.arange(ATTN_BLK)[None, :]
    table = np.full((N_GROUPS, 2, HEADS_PER_GROUP * ATTN_BLK, 2 * ATTN_BLK), NEG, np.float32)
    for g, (_, dil) in enumerate(ATTN_GROUPS):
        for h in range(HEADS_PER_GROUP):
            slope = 2.0 ** (-8.0 * (g * HEADS_PER_GROUP + h + 1) / N_HEADS) * dil * LOG2E
            rows = slice(h * ATTN_BLK, (h + 1) * ATTN_BLK)
            cur = np.where(kj <= qi, -slope * (qi - kj), NEG)
            prev = np.where(kj >= qi, -slope * (ATTN_BLK + qi - kj), NEG)
            table[g, :, rows, ATTN_BLK:] = cur
            table[g, 0, rows, :ATTN_BLK] = prev
    return table


def _attn_kernel(bias_ref, *refs):
    qkv_refs = refs[:3 * N_GROUPS]
    o_ref, out_scr, lse_scr = refs[3 * N_GROUPS:3 * N_GROUPS + 3]
    hist_refs = refs[3 * N_GROUPS + 3:]
    tile = pl.program_id(1)
    u = pl.program_id(2)
    units = ATTN_TILE // ATTN_BLK
    nt = (((1,), (1,)), ((), ()))

    @pl.when(jnp.logical_and(tile == 0, u == 0))
    def _():
        for hist in hist_refs:
            hist[...] = jnp.zeros_like(hist)

    for g, (_, dil) in enumerate(ATTN_GROUPS):
        hist = hist_refs[g]
        nb = u // dil
        r = u % dil
        first = jnp.logical_and(tile == 0, nb == 0).astype(jnp.int32)
        start = nb * (ATTN_BLK * dil) + r
        rows = pl.ds(start, ATTN_BLK) if dil == 1 else pl.ds(start, ATTN_BLK, stride=dil)
        q, k, v = (ref[...].reshape(ATTN_BLK, GROUP_WIDTH) for ref in qkv_refs[3 * g:3 * g + 3])
        prev = hist[r]
        scores = []
        for h in range(HEADS_PER_GROUP):
            cs = slice(h * HEAD_DIM, (h + 1) * HEAD_DIM)
            keys = jnp.concatenate([prev[:, cs], k[:, cs]], axis=0)
            scores.append(lax.dot_general(q[:, cs], keys, nt, preferred_element_type=F32))
        s = jnp.concatenate(scores, axis=0) * (HEAD_DIM ** -0.5 * LOG2E) + bias_ref[g, first]
        m = jnp.max(s, axis=1, keepdims=True)
        p = jnp.exp2(s - m)
        l = jnp.sum(p, axis=1, keepdims=True)
        p = p.astype(BF16)
        inv = 1.0 / l
        lse = m + jnp.log2(l)
        for h in range(HEADS_PER_GROUP):
            cs = slice(h * HEAD_DIM, (h + 1) * HEAD_DIM)
            hr = slice(h * ATTN_BLK, (h + 1) * ATTN_BLK)
            vals = jnp.concatenate([prev[:, GROUP_WIDTH:][:, cs], v[:, cs]], axis=0)
            o = jnp.dot(p[hr], vals, preferred_element_type=F32)
            out_scr[g, h, rows, :] = o * inv[hr]
            lse_scr[g, h, rows, :] = jnp.broadcast_to(lse[hr], (ATTN_BLK, HEAD_DIM))
        hist[r, :, :GROUP_WIDTH] = k
        hist[r, :, GROUP_WIDTH:] = v

    @pl.when(u == units - 1)
    def _():
        for h in range(HEADS_PER_GROUP):
            lses = [lse_scr[g, h] for g in range(N_GROUPS)]
            top = functools.reduce(jnp.maximum, lses)
            ws = [jnp.exp2(x - top) for x in lses]
            num = sum(w * out_scr[g, h] for g, w in enumerate(ws))
            o_ref[:, h * HEAD_DIM:(h + 1) * HEAD_DIM] = (num / sum(ws)).astype(o_ref.dtype)


def _attention(proj, bsz, seq):
    assert seq % ATTN_TILE == 0 and ATTN_TILE % PROJ_TILE == 0
    tiles = seq // ATTN_TILE
    units = ATTN_TILE // ATTN_BLK
    bias = jnp.asarray(_attn_bias_table())
    in_specs = [pl.BlockSpec(bias.shape, lambda b, tile, u: (0, 0, 0, 0))]
    hist = []
    for g, (window, dil) in enumerate(ATTN_GROUPS):
        assert window // dil == ATTN_BLK and units % dil == 0
        span = ATTN_BLK * dil
        spans_per_tile = ATTN_TILE // span
        if span <= PROJ_TILE:
            block = (None, None, ATTN_BLK, GROUP_WIDTH)

            def index(b, tile, u, *, col, dil=dil, spt=spans_per_tile, sppt=PROJ_TILE // span,
                      per_seq=seq // PROJ_TILE):
                sp = tile * spt + u // dil
                return b * per_seq + sp // sppt, col, _residue_block(u % dil, dil) * sppt + sp % sppt, 0
        else:
            block = (span // PROJ_TILE, None, PROJ_TILE // dil, GROUP_WIDTH)

            def index(b, tile, u, *, col, dil=dil, spt=spans_per_tile, per_seq=seq // span):
                return b * per_seq + tile * spt + u // dil, col, _residue_block(u % dil, dil), 0

        for part in range(3):
            in_specs.append(pl.BlockSpec(block, functools.partial(index, col=part * N_GROUPS + g)))
        hist.append(pltpu.VMEM((dil, ATTN_BLK, 2 * GROUP_WIDTH), BF16))
    scratch = pltpu.VMEM((N_GROUPS, HEADS_PER_GROUP, ATTN_TILE, HEAD_DIM), F32)
    return pl.pallas_call(
        _attn_kernel,
        grid=(bsz, tiles, units),
        in_specs=in_specs,
        out_specs=pl.BlockSpec((ATTN_TILE, GROUP_WIDTH), lambda b, tile, u: (b * tiles + tile, 0)),
        out_shape=jax.ShapeDtypeStruct((bsz * seq, GROUP_WIDTH), BF16),
        scratch_shapes=[scratch, scratch] + hist,
        compiler_params=_params("arbitrary", "arbitrary", "arbitrary"),
        name="dilated_attention",
    )(bias, *([proj] * (3 * N_GROUPS)))


def _cmul(a, b):
    return a[0] * b[0] - a[1] * b[1], a[0] * b[1] + a[1] * b[0]


def _ssm_weights(a_re, a_im, log_dt, b_re, b_im, c_re, c_im):
    n_groups = a_re.shape[0]
    nblk = n_groups // GROUPS_PER_LANE_BLOCK
    gl = GROUPS_PER_LANE_BLOCK
    L = SSM_CHUNK
    a_re, a_im = a_re.astype(F32), a_im.astype(F32)
    dt = jnp.exp(log_dt.astype(F32))[:, None]
    steps = jnp.arange(L + 1, dtype=F32)[None, :, None]
    mag = jnp.exp((a_re * dt)[:, None, :] * steps)
    ang = (a_im * dt)[:, None, :] * steps
    powers = (mag * jnp.cos(ang), mag * jnp.sin(ang))
    lam_bar = (powers[0][:, 1], powers[1][:, 1])
    den = a_re * a_re + a_im * a_im
    num = (lam_bar[0] - 1.0, lam_bar[1])
    ratio = ((num[0] * a_re + num[1] * a_im) / den, (num[1] * a_re - num[0] * a_im) / den)
    b_bar = _cmul((ratio[0][..., None], ratio[1][..., None]), (b_re.astype(F32), b_im.astype(F32)))
    c_t = (c_re.astype(F32).transpose(0, 2, 1), c_im.astype(F32).transpose(0, 2, 1))
    eye = jnp.eye(gl, dtype=F32)

    def block_diag(m):
        rows, cols = m.shape[1:]
        m = m.reshape(nblk, gl, rows, 1, cols) * eye[None, :, None, :, None]
        return m.reshape(nblk, gl * rows, gl * cols)

    b_in = jnp.stack([block_diag(b.transpose(0, 2, 1)) for b in b_bar], axis=1)
    c_out = jnp.stack([block_diag(c) for c in c_t], axis=1)
    pw = jnp.stack(powers, axis=0).reshape(2, nblk, gl, L + 1, SSM_STATE)
    pw_row = pw.transpose(1, 0, 3, 2, 4).reshape(nblk, 2, L + 1, gl * SSM_STATE)
    pw_col = pw_row.transpose(0, 1, 3, 2)
    return b_in, c_out, pw_row, pw_col


def _ssm_kernel(*refs, tiles_per_seq):
    u_refs = refs[:SSM_CHUNK]
    (bin_ref, cout_ref, pwr_ref, pwc_ref, d_ref, y_ref,
     wt_scr, wb_scr, wl_scr, wc_scr, s_scr, xp_scr, carry_scr, y_scr) = refs[SSM_CHUNK:]
    L = SSM_CHUNK
    nb = wt_scr.shape[0]
    tc = s_scr.shape[0]
    half = s_scr.shape[1] // 2
    sw = half // nb

    @pl.when(pl.program_id(1) == 0)
    def _():
        for q in range(nb):
            c_hi = [cout_ref[q, ri].astype(BF16) for ri in range(2)]
            c_lo = [(cout_ref[q, ri] - c_hi[ri].astype(F32)).astype(BF16) for ri in range(2)]
            for j in range(L):
                rows = slice(j * LANES, (j + 1) * LANES)
                n = L - 1 - j
                pr = pwr_ref[q, 0, n:n + 1, :]
                pi = pwr_ref[q, 1, n:n + 1, :]
                for ri, val in enumerate((bin_ref[q, 0] * pr - bin_ref[q, 1] * pi,
                                          bin_ref[q, 0] * pi + bin_ref[q, 1] * pr)):
                    cols = slice(ri * sw, (ri + 1) * sw)
                    top = val.astype(BF16)
                    wb_scr[q, rows, cols] = top
                    wl_scr[rows, cols] = (val - top.astype(F32)).astype(BF16)
                pr = pwc_ref[q, 0, :, j + 1:j + 2]
                pi = pwc_ref[q, 1, :, j + 1:j + 2]
                wc_scr[q, :sw, rows] = (cout_ref[q, 0] * pr - cout_ref[q, 1] * pi).astype(BF16)
                wc_scr[q, sw:, rows] = (-(cout_ref[q, 0] * pi + cout_ref[q, 1] * pr)).astype(BF16)
            prods = []
            for ri in range(2):
                cols = slice(ri * sw, (ri + 1) * sw)
                top, low = wb_scr[q, :, cols], wl_scr[:, cols]
                prods.append(jnp.dot(top, c_hi[ri], preferred_element_type=F32)
                             + jnp.dot(top, c_lo[ri], preferred_element_type=F32)
                             + jnp.dot(low, c_hi[ri], preferred_element_type=F32))
            k_all = (prods[0] - prods[1]).astype(BF16)
            for j in range(L):
                n = L - 1 - j
                kn = k_all[j * LANES:(j + 1) * LANES]
                for jj in range(L):
                    ii = jj + n
                    if ii < L:
                        wt_scr[q, jj * LANES:(jj + 1) * LANES, ii * LANES:(ii + 1) * LANES] = kn
                    if j < jj:
                        wt_scr[q, jj * LANES:(jj + 1) * LANES, j * LANES:(j + 1) * LANES] = jnp.zeros((LANES, LANES), BF16)

    @pl.when(pl.program_id(1) % tiles_per_seq == 0)
    def _():
        carry_scr[...] = jnp.zeros_like(carry_scr)

    us = [r[...].reshape(tc, nb * LANES) for r in u_refs]
    ucat = [jnp.concatenate([u[:, q * LANES:(q + 1) * LANES] for u in us], axis=1)
            for q in range(nb)]
    for q in range(nb):
        s = jnp.dot(ucat[q], wb_scr[q], preferred_element_type=F32)
        s_scr[:, q * sw:(q + 1) * sw] = s[:, :sw]
        s_scr[:, half + q * sw:half + (q + 1) * sw] = s[:, sw:]
    for q in range(nb):
        re = slice(q * sw, (q + 1) * sw)
        im = slice(half + q * sw, half + (q + 1) * sw)
        ar = pwr_ref[q, 0, L:L + 1, :]
        ai = pwr_ref[q, 1, L:L + 1, :]

        def step(c, carry, re=re, im=im, ar=ar, ai=ai):
            xr, xi = carry
            xp_scr[pl.ds(c, 1), re] = xr
            xp_scr[pl.ds(c, 1), im] = xi
            return (ar * xr - ai * xi + s_scr[pl.ds(c, 1), re], ar * xi + ai * xr + s_scr[pl.ds(c, 1), im])

        xr, xi = lax.fori_loop(0, tc, step, (carry_scr[:, re], carry_scr[:, im]))
        carry_scr[:, re] = xr
        carry_scr[:, im] = xi

    for q in range(nb):
        lanes = slice(q * LANES, (q + 1) * LANES)
        xq = jnp.concatenate([xp_scr[:, q * sw:(q + 1) * sw], xp_scr[:, half + q * sw:half + (q + 1) * sw]], axis=1)
        y = jnp.dot(ucat[q], wt_scr[q], preferred_element_type=F32)
        y = y + jnp.dot(xq.astype(BF16), wc_scr[q], preferred_element_type=F32)
        for i in range(L):
            yi = y[:, i * LANES:(i + 1) * LANES] + d_ref[0, :, lanes] * us[i][:, lanes].astype(F32)
            y_scr[q, pl.ds(i, tc, stride=L), :] = jax.nn.gelu(yi)
        y_ref[:, lanes] = y_scr[q].astype(y_ref.dtype)


def _ssm(proj, u_tile0, ssm_w, d_skip, bsz, seq, *, ptiles, nb):
    nt, _, tm, tn = proj.shape
    L = SSM_CHUNK
    per_tile = tn // (nb * LANES)
    crows = tm // L
    nblk = ssm_w[0].shape[0]
    width = nblk * LANES
    assert crows == LANES and nblk % nb == 0 and tn % (nb * LANES) == 0
    tc = ptiles * crows
    rows_per_seq = seq // L
    assert rows_per_seq % tc == 0 and nt % ptiles == 0
    u_specs = [
        pl.BlockSpec((ptiles, None, crows, nb * LANES), functools.partial(
            lambda blk, i, j: (i, u_tile0 + blk // per_tile, j, blk % per_tile), j=_residue_block(j, L)))
        for j in range(L)
    ]
    w_specs = [pl.BlockSpec((nb,) + w.shape[1:], lambda blk, i: (blk, 0, 0, 0)) for w in ssm_w]
    wide = L * LANES
    states = 2 * GROUPS_PER_LANE_BLOCK * SSM_STATE
    return pl.pallas_call(
        functools.partial(_ssm_kernel, tiles_per_seq=rows_per_seq // tc),
        grid=(nblk // nb, nt // ptiles),
        in_specs=u_specs + w_specs + [pl.BlockSpec((1, 1, nb * LANES), lambda blk, i: (blk, 0, 0))],
        out_specs=pl.BlockSpec((tc * L, nb * LANES), lambda blk, i: (i, blk)),
        out_shape=jax.ShapeDtypeStruct((nt * tm, width), BF16),
        scratch_shapes=[
            pltpu.VMEM((nb, wide, wide), BF16),
            pltpu.VMEM((nb, wide, states), BF16),
            pltpu.VMEM((wide, states), BF16),
            pltpu.VMEM((nb, states, wide), BF16),
            pltpu.VMEM((tc, nb * states), F32),
            pltpu.VMEM((tc, nb * states), F32),
            pltpu.VMEM((1, nb * states), F32),
            pltpu.VMEM((nb, tc * L, LANES), F32),
        ],
        compiler_params=_params("parallel", "arbitrary"),
        name="s5_chunked",
    )(*([proj] * L), *ssm_w, d_skip.astype(F32).reshape(nblk // nb, 1, nb * LANES))


def _merge_kernel(attn_ref, y_ref, ga_ref, gs_ref, wup_ref, wv_ref, wg_ref, o_ref):
    y = y_ref[...]
    gate = _sigmoid(jnp.dot(y, wg_ref[...].astype(BF16), preferred_element_type=F32))
    gate_s = _sigmoid(gs_ref[...].astype(F32))
    gate_a = _sigmoid(ga_ref[...].astype(F32))
    val = jnp.dot(y, wv_ref[...].astype(BF16), preferred_element_type=F32)
    attn_branch = jnp.dot(attn_ref[...], wup_ref[...].astype(BF16), preferred_element_type=F32)
    merged = gate_a * attn_branch + gate_s * (val * gate)
    o_ref[...] = merged.astype(o_ref.dtype)


def _merge(attn, y, proj, gate_tile0, w_up, w_v, w_g):
    t = attn.shape[0]
    n = w_up.shape[1]
    _, _, tm, tn = proj.shape
    assert n % tn == 0
    return pl.pallas_call(
        _merge_kernel,
        grid=(t // tm, n // tn),
        in_specs=[
            pl.BlockSpec((tm, attn.shape[1]), lambda i, j: (i, 0)),
            pl.BlockSpec((tm, y.shape[1]), lambda i, j: (i, 0)),
            pl.BlockSpec((None, None, tm, tn), lambda i, j: (i, gate_tile0 + j, 0, 0)),
            pl.BlockSpec((None, None, tm, tn), lambda i, j: (i, gate_tile0 + n // tn + j, 0, 0)),
            pl.BlockSpec((w_up.shape[0], tn), lambda i, j: (0, j)),
            pl.BlockSpec((w_v.shape[0], tn), lambda i, j: (0, j)),
            pl.BlockSpec((w_g.shape[0], tn), lambda i, j: (0, j)),
        ],
        out_specs=pl.BlockSpec((tm, tn), lambda i, j: (i, j)),
        out_shape=jax.ShapeDtypeStruct((t, n), BF16),
        compiler_params=_params("parallel", "arbitrary"),
        name="gated_merge",
    )(attn, y, proj, proj, w_up, w_v, w_g)


def _outproj_kernel(m_ref, w_ref, x_ref, g_ref, o_ref):
    z = jnp.dot(m_ref[...], w_ref[...], preferred_element_type=F32)
    o_ref[...] = x_ref[...] + _rms(z, g_ref[...])


def _outproj(merged, w, x, gain, *, tm):
    t, d = x.shape
    return pl.pallas_call(
        _outproj_kernel,
        grid=(t // tm,),
        in_specs=[
            pl.BlockSpec((tm, merged.shape[1]), lambda i: (i, 0)),
            pl.BlockSpec(w.shape, lambda i: (0, 0)),
            pl.BlockSpec((tm, d), lambda i: (i, 0)),
            pl.BlockSpec((1, d), lambda i: (0, 0)),
        ],
        out_specs=pl.BlockSpec((tm, d), lambda i: (i, 0)),
        out_shape=jax.ShapeDtypeStruct((t, d), F32),
        compiler_params=_params("parallel"),
        name="outproj_norm_residual",
    )(merged, w, x, gain.reshape(1, d))


def _ffn_kernel(x_ref, gpre_ref, gpost_ref, wg_ref, wu_ref, wd_ref, o_ref, h_ref, acc_ref):
    k = pl.program_id(1)

    @pl.when(k == 0)
    def _():
        h_ref[...] = _rms(x_ref[...], gpre_ref[...]).astype(BF16)
        acc_ref[...] = jnp.zeros_like(acc_ref)

    h = h_ref[...]
    gate = jnp.dot(h, wg_ref[...], preferred_element_type=F32)
    up = jnp.dot(h, wu_ref[...], preferred_element_type=F32)
    f = (jax.nn.silu(gate) * up).astype(BF16)
    acc_ref[...] += jnp.dot(f, wd_ref[...], preferred_element_type=F32)

    @pl.when(k == pl.num_programs(1) - 1)
    def _():
        o_ref[...] = x_ref[...] + _rms(acc_ref[...], gpost_ref[...])


def _ffn(x, gain_pre, gain_post, w_gate, w_up, w_down, *, tm, tf):
    t, d = x.shape
    dff = w_gate.shape[1]
    return pl.pallas_call(
        _ffn_kernel,
        grid=(t // tm, dff // tf),
        in_specs=[
            pl.BlockSpec((tm, d), lambda i, k: (i, 0)),
            pl.BlockSpec((1, d), lambda i, k: (0, 0)),
            pl.BlockSpec((1, d), lambda i, k: (0, 0)),
            pl.BlockSpec((d, tf), lambda i, k: (0, k)),
            pl.BlockSpec((d, tf), lambda i, k: (0, k)),
            pl.BlockSpec((tf, d), lambda i, k: (k, 0)),
        ],
        out_specs=pl.BlockSpec((tm, d), lambda i, k: (i, 0)),
        out_shape=jax.ShapeDtypeStruct((t, d), F32),
        scratch_shapes=[pltpu.VMEM((tm, d), BF16), pltpu.VMEM((tm, d), F32)],
        compiler_params=_params("parallel", "arbitrary"),
        name="swiglu_ffn",
    )(x, gain_pre.reshape(1, d), gain_post.reshape(1, d), w_gate, w_up, w_down)


def _layer(x, norm_mix_pre, w_in, w_attn_up, ssm_a_re, ssm_a_im, ssm_log_dt, ssm_b_re, ssm_b_im,
           ssm_c_re, ssm_c_im, ssm_d, w_glu_v, w_glu_g, w_out, norm_mix_post, norm_ffn_pre,
           w_ffn_gate, w_ffn_up, w_ffn_down, norm_ffn_post):
    bsz, seq, d = x.shape
    t = bsz * seq
    ssm_width = ssm_d.shape[0]
    u_col0 = 3 * N_HEADS * HEAD_DIM
    assert w_in.shape[1] == u_col0 + ssm_width + 2 * d

    x2 = x.reshape(t, d)
    tn = GROUP_WIDTH
    proj = _inproj(x2, norm_mix_pre, w_in.astype(BF16), ssm_width)
    attn = _attention(proj, bsz, seq)
    ssm_w = _ssm_weights(ssm_a_re, ssm_a_im, ssm_log_dt, ssm_b_re, ssm_b_im, ssm_c_re, ssm_c_im)
    y = _ssm(proj, u_col0 // tn, ssm_w, ssm_d, bsz, seq, ptiles=4, nb=2)
    merged = _merge(attn, y, proj, (u_col0 + ssm_width) // tn, w_attn_up, w_glu_v, w_glu_g)
    x1 = _outproj(merged, w_out.astype(BF16), x2, norm_mix_post, tm=512)
    out = _ffn(x1, norm_ffn_pre, norm_ffn_post, w_ffn_gate.astype(BF16), w_ffn_up.astype(BF16),
               w_ffn_down.astype(BF16), tm=512, tf=512)
    return out.reshape(bsz, seq, d)


def kernel(x, norm_mix_pre, w_in, w_attn_up, ssm_a_re, ssm_a_im, ssm_log_dt, ssm_b_re, ssm_b_im, ssm_c_re, ssm_c_im, ssm_d, w_glu_v, w_glu_g, w_out, norm_mix_post, norm_ffn_pre, w_ffn_gate, w_ffn_up, w_ffn_down, norm_ffn_post):
    stacked = (norm_mix_pre, w_in, w_attn_up, ssm_a_re, ssm_a_im, ssm_log_dt, ssm_b_re, ssm_b_im, ssm_c_re,
               ssm_c_im, ssm_d, w_glu_v, w_glu_g, w_out, norm_mix_post, norm_ffn_pre, w_ffn_gate, w_ffn_up,
               w_ffn_down, norm_ffn_post)
    for layer in range(norm_mix_pre.shape[0]):
        x = _layer(x, *(p[layer] for p in stacked))
    return x
```

```python
import functools

import jax
import jax.numpy as jnp
import numpy as np
from jax import lax
from jax.experimental import pallas as pl
from jax.experimental.pallas import tpu as pltpu

F32 = jnp.float32
BF16 = jnp.bfloat16

EPS = 1e-6
HEAD_DIM = 128
HEADS_PER_GROUP = 4
ATTN_GROUPS = ((128, 1), (512, 4), (2048, 16))
N_GROUPS = len(ATTN_GROUPS)
N_HEADS = HEADS_PER_GROUP * N_GROUPS
GROUP_WIDTH = HEADS_PER_GROUP * HEAD_DIM
ATTN_BLK = 128
ATTN_TILE = 2048
SSM_GROUP = 16
SSM_STATE = 64
SSM_CHUNK = 8
LANES = 128
GROUPS_PER_LANE_BLOCK = LANES // SSM_GROUP
PROJ_TILE = 1024
ROW_ORDERS = tuple(dil for _, dil in ATTN_GROUPS) + (SSM_CHUNK,)
PERM_BASE = 4
NEG = -1e30
LOG2E = 1.4426950408889634
VMEM_LIMIT = 56 * 1024 * 1024


def _params(*sem):
    return pltpu.CompilerParams(dimension_semantics=sem, vmem_limit_bytes=VMEM_LIMIT)


def _sigmoid(x):
    return 0.5 * jnp.tanh(0.5 * x) + 0.5


def _rms(x, gain):
    return x * lax.rsqrt(jnp.mean(x * x, axis=-1, keepdims=True) + EPS) * gain


def _residue_block(r, dil):
    if dil <= PERM_BASE:
        return r
    return (r % PERM_BASE) * (dil // PERM_BASE) + r // PERM_BASE


def _inproj_kernel(x_ref, g_ref, w_ref, o_ref, h_ref, hn_ref, hb_ref, inv_ref, *, n_qkv, u_tiles):
    j = pl.program_id(1)
    slabs, tm, _ = hn_ref.shape

    @pl.when(j == 0)
    def _():
        x = x_ref[...]
        inv_ref[...] = jnp.broadcast_to(lax.rsqrt(jnp.mean(x * x, axis=-1, keepdims=True) + EPS), inv_ref.shape)
        q = tm // PERM_BASE
        piece = 256
        for c in range(x_ref.shape[1] // LANES):
            cols = slice(c * LANES, (c + 1) * LANES)
            s = c % slabs
            for r0 in range(0, tm, piece):
                rows = slice(r0, r0 + piece)
                hn = x_ref[rows, cols] * inv_ref[rows, :] * g_ref[:, cols]
                h_ref[0, rows, cols] = hn.astype(BF16)
                hn_ref[s, rows, :] = hn
            for b in range(PERM_BASE):
                part = hn_ref[s, pl.ds(b, q, stride=PERM_BASE), :]
                hb_ref[s, b * q:(b + 1) * q, :] = part
                for v, dil in enumerate(ROW_ORDERS):
                    if dil == PERM_BASE:
                        h_ref[v, b * q:(b + 1) * q, cols] = part.astype(BF16)
            for v, dil in enumerate(ROW_ORDERS):
                if dil > PERM_BASE:
                    k, n = dil // PERM_BASE, tm // dil
                    for b in range(PERM_BASE):
                        for a in range(k):
                            blk = b * k + a
                            h_ref[v, blk * n:(blk + 1) * n, cols] = (
                                hb_ref[s, pl.ds(b * q + a, n, stride=k), :].astype(BF16))

    order = jnp.where(j < n_qkv, j % N_GROUPS, jnp.where(j < n_qkv + u_tiles, N_GROUPS, 0))
    o_ref[...] = jnp.dot(h_ref[order], w_ref[...], preferred_element_type=F32).astype(o_ref.dtype)


def _inproj(x, gain, w, ssm_width):
    t, d = x.shape
    tm = PROJ_TILE
    tn = GROUP_WIDTH
    ncol = w.shape[1] // tn
    assert t % tm == 0 and w.shape[1] % tn == 0 and ssm_width % tn == 0 and d % LANES == 0
    assert ROW_ORDERS[0] == 1 and all(tm % (dil * 16) == 0 for dil in ROW_ORDERS)
    assert all(dil in (1, PERM_BASE) or (dil % PERM_BASE == 0 and dil // PERM_BASE <= PERM_BASE) for dil in ROW_ORDERS)
    slabs = 8
    return pl.pallas_call(
        functools.partial(_inproj_kernel, n_qkv=3 * N_GROUPS, u_tiles=ssm_width // tn),
        grid=(t // tm, ncol),
        in_specs=[
            pl.BlockSpec((tm, d), lambda i, j: (i, 0)),
            pl.BlockSpec((1, d), lambda i, j: (0, 0)),
            pl.BlockSpec((d, tn), lambda i, j: (0, j)),
        ],
        out_specs=pl.BlockSpec((None, None, tm, tn), lambda i, j: (i, j, 0, 0)),
        out_shape=jax.ShapeDtypeStruct((t // tm, ncol, tm, tn), BF16),
        scratch_shapes=[
            pltpu.VMEM((len(ROW_ORDERS), tm, d), BF16),
            pltpu.VMEM((slabs, tm, LANES), F32),
            pltpu.VMEM((slabs, tm, LANES), F32),
            pltpu.VMEM((tm, LANES), F32),
        ],
        compiler_params=_params("parallel", "arbitrary"),
        name="inproj",
    )(x, gain.reshape(1, d), w)


def _attn_bias_table():
    qi = np.arange(ATTN_BLK)[:, None]
    kj = np.arange(ATTN_BLK)[None, :]
    table = np.full((N_GROUPS, 2, HEADS_PER_GROUP * ATTN_BLK, 2 * ATTN_BLK), NEG, np.float32)
    for g, (_, dil) in enumerate(ATTN_GROUPS):
        for h in range(HEADS_PER_GROUP):
            slope = 2.0 ** (-8.0 * (g * HEADS_PER_GROUP + h + 1) / N_HEADS) * dil * LOG2E
            rows = slice(h * ATTN_BLK, (h + 1) * ATTN_BLK)
            cur = np.where(kj <= qi, -slope * (qi - kj), NEG)
            prev = np.where(kj >= qi, -slope * (ATTN_BLK + qi - kj), NEG)
            table[g, :, rows, ATTN_BLK:] = cur
            table[g, 0, rows, :ATTN_BLK] = prev
    return table


def _attn_kernel(bias_ref, *refs):
    qkv_refs = refs[:3 * N_GROUPS]
    o_ref, out_scr, lse_scr = refs[3 * N_GROUPS:3 * N_GROUPS + 3]
    hist_refs = refs[3 * N_GROUPS + 3:]
    tile = pl.program_id(1)
    u = pl.program_id(2)
    units = ATTN_TILE // ATTN_BLK
    nt = (((1,), (1,)), ((), ()))

    @pl.when(jnp.logical_and(tile == 0, u == 0))
    def _():
        for hist in hist_refs:
            hist[...] = jnp.zeros_like(hist)

    blocks, logits, stats = [], [], []
    for g, (_, dil) in enumerate(ATTN_GROUPS):
        nb = u // dil
        r = u % dil
        first = jnp.logical_and(tile == 0, nb == 0).astype(jnp.int32)
        start = nb * (ATTN_BLK * dil) + r
        rows = pl.ds(start, ATTN_BLK) if dil == 1 else pl.ds(start, ATTN_BLK, stride=dil)
        q, k, v = (ref[...].reshape(ATTN_BLK, GROUP_WIDTH) for ref in qkv_refs[3 * g:3 * g + 3])
        prev = hist_refs[g][r]
        scores = []
        for h in range(HEADS_PER_GROUP):
            cs = slice(h * HEAD_DIM, (h + 1) * HEAD_DIM)
            keys = jnp.concatenate([prev[:, cs], k[:, cs]], axis=0)
            scores.append(lax.dot_general(q[:, cs], keys, nt, preferred_element_type=F32))
        logits.append(jnp.concatenate(scores, axis=0) * (HEAD_DIM ** -0.5 * LOG2E) + bias_ref[g, first])
        blocks.append((r, rows, k, v, prev))
    for s in logits:
        m = jnp.max(s, axis=1, keepdims=True)
        p = jnp.exp2(s - m)
        l = jnp.sum(p, axis=1, keepdims=True)
        stats.append((p.astype(BF16), 1.0 / l, m + jnp.log2(l)))
    for g, ((r, rows, k, v, prev), (p, inv, lse)) in enumerate(zip(blocks, stats)):
        for h in range(HEADS_PER_GROUP):
            cs = slice(h * HEAD_DIM, (h + 1) * HEAD_DIM)
            hr = slice(h * ATTN_BLK, (h + 1) * ATTN_BLK)
            vals = jnp.concatenate([prev[:, GROUP_WIDTH:][:, cs], v[:, cs]], axis=0)
            o = jnp.dot(p[hr], vals, preferred_element_type=F32)
            out_scr[g, h, rows, :] = o * inv[hr]
            lse_scr[g, h, rows, :] = jnp.broadcast_to(lse[hr], (ATTN_BLK, HEAD_DIM))
        hist_refs[g][r, :, :GROUP_WIDTH] = k
        hist_refs[g][r, :, GROUP_WIDTH:] = v

    @pl.when(u == units - 1)
    def _():
        for h in range(HEADS_PER_GROUP):
            lses = [lse_scr[g, h] for g in range(N_GROUPS)]
            top = functools.reduce(jnp.maximum, lses)
            ws = [jnp.exp2(x - top) for x in lses]
            num = sum(w * out_scr[g, h] for g, w in enumerate(ws))
            o_ref[:, h * HEAD_DIM:(h + 1) * HEAD_DIM] = (num / sum(ws)).astype(o_ref.dtype)


def _attention(proj, bsz, seq):
    assert seq % ATTN_TILE == 0 and ATTN_TILE % PROJ_TILE == 0
    tiles = seq // ATTN_TILE
    units = ATTN_TILE // ATTN_BLK
    bias = jnp.asarray(_attn_bias_table())
    in_specs = [pl.BlockSpec(bias.shape, lambda b, tile, u: (0, 0, 0, 0))]
    hist = []
    for g, (window, dil) in enumerate(ATTN_GROUPS):
        assert window // dil == ATTN_BLK and units % dil == 0
        span = ATTN_BLK * dil
        spans_per_tile = ATTN_TILE // span
        if span <= PROJ_TILE:
            block = (None, None, ATTN_BLK, GROUP_WIDTH)

            def index(b, tile, u, *, col, dil=dil, spt=spans_per_tile, sppt=PROJ_TILE // span,
                      per_seq=seq // PROJ_TILE):
                sp = tile * spt + u // dil
                return b * per_seq + sp // sppt, col, _residue_block(u % dil, dil) * sppt + sp % sppt, 0
        else:
            block = (span // PROJ_TILE, None, PROJ_TILE // dil, GROUP_WIDTH)

            def index(b, tile, u, *, col, dil=dil, spt=spans_per_tile, per_seq=seq // span):
                return b * per_seq + tile * spt + u // dil, col, _residue_block(u % dil, dil), 0

        for part in range(3):
            in_specs.append(pl.BlockSpec(block, functools.partial(index, col=part * N_GROUPS + g)))
        hist.append(pltpu.VMEM((dil, ATTN_BLK, 2 * GROUP_WIDTH), BF16))
    scratch = pltpu.VMEM((N_GROUPS, HEADS_PER_GROUP, ATTN_TILE, HEAD_DIM), F32)
    return pl.pallas_call(
        _attn_kernel,
        grid=(bsz, tiles, units),
        in_specs=in_specs,
        out_specs=pl.BlockSpec((ATTN_TILE, GROUP_WIDTH), lambda b, tile, u: (b * tiles + tile, 0)),
        out_shape=jax.ShapeDtypeStruct((bsz * seq, GROUP_WIDTH), BF16),
        scratch_shapes=[scratch, scratch] + hist,
        compiler_params=_params("arbitrary", "arbitrary", "arbitrary"),
        name="dilated_attention",
    )(bias, *([proj] * (3 * N_GROUPS)))


def _cmul(a, b):
    return a[0] * b[0] - a[1] * b[1], a[0] * b[1] + a[1] * b[0]


def _ssm_weights(a_re, a_im, log_dt, b_re, b_im, c_re, c_im):
    n_groups = a_re.shape[0]
    nblk = n_groups // GROUPS_PER_LANE_BLOCK
    gl = GROUPS_PER_LANE_BLOCK
    L = SSM_CHUNK
    a_re, a_im = a_re.astype(F32), a_im.astype(F32)
    dt = jnp.exp(log_dt.astype(F32))[:, None]
    steps = jnp.arange(L + 1, dtype=F32)[None, :, None]
    mag = jnp.exp((a_re * dt)[:, None, :] * steps)
    ang = (a_im * dt)[:, None, :] * steps
    powers = (mag * jnp.cos(ang), mag * jnp.sin(ang))
    lam_bar = (powers[0][:, 1], powers[1][:, 1])
    den = a_re * a_re + a_im * a_im
    num = (lam_bar[0] - 1.0, lam_bar[1])
    ratio = ((num[0] * a_re + num[1] * a_im) / den, (num[1] * a_re - num[0] * a_im) / den)
    b_bar = _cmul((ratio[0][..., None], ratio[1][..., None]), (b_re.astype(F32), b_im.astype(F32)))
    c_t = (c_re.astype(F32).transpose(0, 2, 1), c_im.astype(F32).transpose(0, 2, 1))
    eye = jnp.eye(gl, dtype=F32)

    def block_diag(m):
        rows, cols = m.shape[1:]
        m = m.reshape(nblk, gl, rows, 1, cols) * eye[None, :, None, :, None]
        return m.reshape(nblk, gl * rows, gl * cols)

    b_in = jnp.stack([block_diag(b.transpose(0, 2, 1)) for b in b_bar], axis=1)
    c_out = jnp.stack([block_diag(c) for c in c_t], axis=1)
    pw = jnp.stack(powers, axis=0).reshape(2, nblk, gl, L + 1, SSM_STATE)
    pw_row = pw.transpose(1, 0, 3, 2, 4).reshape(nblk, 2, L + 1, gl * SSM_STATE)
    pw_col = pw_row.transpose(0, 1, 3, 2)
    return b_in, c_out, pw_row, pw_col


def _ssm_kernel(*refs, tiles_per_seq):
    u_refs = refs[:SSM_CHUNK]
    (bin_ref, cout_ref, pwr_ref, pwc_ref, d_ref, y_ref,
     wt_scr, wb_scr, wl_scr, wc_scr, s_scr, xp_scr, carry_scr, y_scr) = refs[SSM_CHUNK:]
    L = SSM_CHUNK
    nb = wt_scr.shape[0]
    tc = s_scr.shape[0]
    half = s_scr.shape[1] // 2
    sw = half // nb

    @pl.when(pl.program_id(1) == 0)
    def _():
        for q in range(nb):
            c_hi = [cout_ref[q, ri].astype(BF16) for ri in range(2)]
            c_lo = [(cout_ref[q, ri] - c_hi[ri].astype(F32)).astype(BF16) for ri in range(2)]
            for j in range(L):
                rows = slice(j * LANES, (j + 1) * LANES)
                n = L - 1 - j
                pr = pwr_ref[q, 0, n:n + 1, :]
                pi = pwr_ref[q, 1, n:n + 1, :]
                for ri, val in enumerate((bin_ref[q, 0] * pr - bin_ref[q, 1] * pi,
                                          bin_ref[q, 0] * pi + bin_ref[q, 1] * pr)):
                    cols = slice(ri * sw, (ri + 1) * sw)
                    top = val.astype(BF16)
                    wb_scr[q, rows, cols] = top
                    wl_scr[rows, cols] = (val - top.astype(F32)).astype(BF16)
                pr = pwc_ref[q, 0, :, j + 1:j + 2]
                pi = pwc_ref[q, 1, :, j + 1:j + 2]
                wc_scr[q, :sw, rows] = (cout_ref[q, 0] * pr - cout_ref[q, 1] * pi).astype(BF16)
                wc_scr[q, sw:, rows] = (-(cout_ref[q, 0] * pi + cout_ref[q, 1] * pr)).astype(BF16)
            prods = []
            for ri in range(2):
                cols = slice(ri * sw, (ri + 1) * sw)
                top, low = wb_scr[q, :, cols], wl_scr[:, cols]
                prods.append(jnp.dot(top, c_hi[ri], preferred_element_type=F32)
                             + jnp.dot(top, c_lo[ri], preferred_element_type=F32)
                             + jnp.dot(low, c_hi[ri], preferred_element_type=F32))
            k_all = (prods[0] - prods[1]).astype(BF16)
            for j in range(L):
                n = L - 1 - j
                kn = k_all[j * LANES:(j + 1) * LANES]
                for jj in range(L):
                    ii = jj + n
                    if ii < L:
                        wt_scr[q, jj * LANES:(jj + 1) * LANES, ii * LANES:(ii + 1) * LANES] = kn
                    if j < jj:
                        wt_scr[q, jj * LANES:(jj + 1) * LANES, j * LANES:(j + 1) * LANES] = jnp.zeros((LANES, LANES), BF16)

    @pl.when(pl.program_id(1) % tiles_per_seq == 0)
    def _():
        carry_scr[...] = jnp.zeros_like(carry_scr)

    us = [r[...].reshape(tc, nb * LANES) for r in u_refs]
    ucat = [jnp.concatenate([u[:, q * LANES:(q + 1) * LANES] for u in us], axis=1)
            for q in range(nb)]
    for q in range(nb):
        s = jnp.dot(ucat[q], wb_scr[q], preferred_element_type=F32)
        s_scr[:, q * sw:(q + 1) * sw] = s[:, :sw]
        s_scr[:, half + q * sw:half + (q + 1) * sw] = s[:, sw:]
    for q in range(nb):
        re = slice(q * sw, (q + 1) * sw)
        im = slice(half + q * sw, half + (q + 1) * sw)
        ar = pwr_ref[q, 0, L:L + 1, :]
        ai = pwr_ref[q, 1, L:L + 1, :]

        def step(c, carry, re=re, im=im, ar=ar, ai=ai):
            xr, xi = carry
            xp_scr[pl.ds(c, 1), re] = xr
            xp_scr[pl.ds(c, 1), im] = xi
            return (ar * xr - ai * xi + s_scr[pl.ds(c, 1), re], ar * xi + ai * xr + s_scr[pl.ds(c, 1), im])

        xr, xi = lax.fori_loop(0, tc, step, (carry_scr[:, re], carry_scr[:, im]))
        carry_scr[:, re] = xr
        carry_scr[:, im] = xi

    for q in range(nb):
        lanes = slice(q * LANES, (q + 1) * LANES)
        xq = jnp.concatenate([xp_scr[:, q * sw:(q + 1) * sw], xp_scr[:, half + q * sw:half + (q + 1) * sw]], axis=1)
        y = jnp.dot(ucat[q], wt_scr[q], preferred_element_type=F32)
        y = y + jnp.dot(xq.astype(BF16), wc_scr[q], preferred_element_type=F32)
        for i in range(L):
            yi = y[:, i * LANES:(i + 1) * LANES] + d_ref[0, :, lanes] * us[i][:, lanes].astype(F32)
            y_scr[q, pl.ds(i, tc, stride=L), :] = jax.nn.gelu(yi)
        y_ref[:, lanes] = y_scr[q].astype(y_ref.dtype)


def _ssm(proj, u_tile0, ssm_w, d_skip, bsz, seq, *, ptiles, nb):
    nt, _, tm, tn = proj.shape
    L = SSM_CHUNK
    per_tile = tn // (nb * LANES)
    crows = tm // L
    nblk = ssm_w[0].shape[0]
    width = nblk * LANES
    assert crows == LANES and nblk % nb == 0 and tn % (nb * LANES) == 0
    tc = ptiles * crows
    rows_per_seq = seq // L
    assert rows_per_seq % tc == 0 and nt % ptiles == 0
    u_specs = [
        pl.BlockSpec((ptiles, None, crows, nb * LANES), functools.partial(
            lambda blk, i, j: (i, u_tile0 + blk // per_tile, j, blk % per_tile), j=_residue_block(j, L)))
        for j in range(L)
    ]
    w_specs = [pl.BlockSpec((nb,) + w.shape[1:], lambda blk, i: (blk, 0, 0, 0)) for w in ssm_w]
    wide = L * LANES
    states = 2 * GROUPS_PER_LANE_BLOCK * SSM_STATE
    return pl.pallas_call(
        functools.partial(_ssm_kernel, tiles_per_seq=rows_per_seq // tc),
        grid=(nblk // nb, nt // ptiles),
        in_specs=u_specs + w_specs + [pl.BlockSpec((1, 1, nb * LANES), lambda blk, i: (blk, 0, 0))],
        out_specs=pl.BlockSpec((tc * L, nb * LANES), lambda blk, i: (i, blk)),
        out_shape=jax.ShapeDtypeStruct((nt * tm, width), BF16),
        scratch_shapes=[
            pltpu.VMEM((nb, wide, wide), BF16),
            pltpu.VMEM((nb, wide, states), BF16),
            pltpu.VMEM((wide, states), BF16),
            pltpu.VMEM((nb, states, wide), BF16),
            pltpu.VMEM((tc, nb * states), F32),
            pltpu.VMEM((tc, nb * states), F32),
            pltpu.VMEM((1, nb * states), F32),
            pltpu.VMEM((nb, tc * L, LANES), F32),
        ],
        compiler_params=_params("parallel", "arbitrary"),
        name="s5_chunked",
    )(*([proj] * L), *ssm_w, d_skip.astype(F32).reshape(nblk // nb, 1, nb * LANES))


def _merge_kernel(attn_ref, y_ref, ga_ref, gs_ref, wup_ref, wv_ref, wg_ref, o_ref):
    y = y_ref[...]
    gate = _sigmoid(jnp.dot(y, wg_ref[...].astype(BF16), preferred_element_type=F32))
    gate_s = _sigmoid(gs_ref[...].astype(F32))
    gate_a = _sigmoid(ga_ref[...].astype(F32))
    val = jnp.dot(y, wv_ref[...].astype(BF16), preferred_element_type=F32)
    attn_branch = jnp.dot(attn_ref[...], wup_ref[...].astype(BF16), preferred_element_type=F32)
    merged = gate_a * attn_branch + gate_s * (val * gate)
    o_ref[...] = merged.astype(o_ref.dtype)


def _merge(attn, y, proj, gate_tile0, w_up, w_v, w_g):
    t = attn.shape[0]
    n = w_up.shape[1]
    _, _, tm, tn = proj.shape
    assert n % tn == 0
    return pl.pallas_call(
        _merge_kernel,
        grid=(t // tm, n // tn),
        in_specs=[
            pl.BlockSpec((tm, attn.shape[1]), lambda i, j: (i, 0)),
            pl.BlockSpec((tm, y.shape[1]), lambda i, j: (i, 0)),
            pl.BlockSpec((None, None, tm, tn), lambda i, j: (i, gate_tile0 + j, 0, 0)),
            pl.BlockSpec((None, None, tm, tn), lambda i, j: (i, gate_tile0 + n // tn + j, 0, 0)),
            pl.BlockSpec((w_up.shape[0], tn), lambda i, j: (0, j)),
            pl.BlockSpec((w_v.shape[0], tn), lambda i, j: (0, j)),
            pl.BlockSpec((w_g.shape[0], tn), lambda i, j: (0, j)),
        ],
        out_specs=pl.BlockSpec((tm, tn), lambda i, j: (i, j)),
        out_shape=jax.ShapeDtypeStruct((t, n), BF16),
        compiler_params=_params("parallel", "arbitrary"),
        name="gated_merge",
    )(attn, y, proj, proj, w_up, w_v, w_g)


def _outproj_kernel(m_ref, w_ref, x_ref, g_ref, o_ref):
    z = jnp.dot(m_ref[...], w_ref[...], preferred_element_type=F32)
    o_ref[...] = x_ref[...] + _rms(z, g_ref[...])


def _outproj(merged, w, x, gain, *, tm):
    t, d = x.shape
    return pl.pallas_call(
        _outproj_kernel,
        grid=(t // tm,),
        in_specs=[
            pl.BlockSpec((tm, merged.shape[1]), lambda i: (i, 0)),
            pl.BlockSpec(w.shape, lambda i: (0, 0)),
            pl.BlockSpec((tm, d), lambda i: (i, 0)),
            pl.BlockSpec((1, d), lambda i: (0, 0)),
        ],
        out_specs=pl.BlockSpec((tm, d), lambda i: (i, 0)),
        out_shape=jax.ShapeDtypeStruct((t, d), F32),
        compiler_params=_params("parallel"),
        name="outproj_norm_residual",
    )(merged, w, x, gain.reshape(1, d))


def _ffn_kernel(x_ref, gpre_ref, gpost_ref, wg_ref, wu_ref, wd_ref, o_ref, h_ref, acc_ref):
    k = pl.program_id(1)

    @pl.when(k == 0)
    def _():
        h_ref[...] = _rms(x_ref[...], gpre_ref[...]).astype(BF16)
        acc_ref[...] = jnp.zeros_like(acc_ref)

    h = h_ref[...]
    gate = jnp.dot(h, wg_ref[...], preferred_element_type=F32)
    up = jnp.dot(h, wu_ref[...], preferred_element_type=F32)
    f = (jax.nn.silu(gate) * up).astype(BF16)
    acc_ref[...] += jnp.dot(f, wd_ref[...].astype(BF16), preferred_element_type=F32)

    @pl.when(k == pl.num_programs(1) - 1)
    def _():
        o_ref[...] = x_ref[...] + _rms(acc_ref[...], gpost_ref[...])


def _ffn(x, gain_pre, gain_post, w_gate, w_up, w_down, *, tm, tf):
    t, d = x.shape
    dff = w_gate.shape[1]
    return pl.pallas_call(
        _ffn_kernel,
        grid=(t // tm, dff // tf),
        in_specs=[
            pl.BlockSpec((tm, d), lambda i, k: (i, 0)),
            pl.BlockSpec((1, d), lambda i, k: (0, 0)),
            pl.BlockSpec((1, d), lambda i, k: (0, 0)),
            pl.BlockSpec((d, tf), lambda i, k: (0, k)),
            pl.BlockSpec((d, tf), lambda i, k: (0, k)),
            pl.BlockSpec((tf, d), lambda i, k: (k, 0)),
        ],
        out_specs=pl.BlockSpec((tm, d), lambda i, k: (i, 0)),
        out_shape=jax.ShapeDtypeStruct((t, d), F32),
        scratch_shapes=[pltpu.VMEM((tm, d), BF16), pltpu.VMEM((tm, d), F32)],
        compiler_params=_params("parallel", "arbitrary"),
        name="swiglu_ffn",
    )(x, gain_pre.reshape(1, d), gain_post.reshape(1, d), w_gate, w_up, w_down)


def _layer(x, norm_mix_pre, w_in, w_attn_up, ssm_a_re, ssm_a_im, ssm_log_dt, ssm_b_re, ssm_b_im,
           ssm_c_re, ssm_c_im, ssm_d, w_glu_v, w_glu_g, w_out, norm_mix_post, norm_ffn_pre,
           w_ffn_gate, w_ffn_up, w_ffn_down, norm_ffn_post):
    bsz, seq, d = x.shape
    t = bsz * seq
    ssm_width = ssm_d.shape[0]
    u_col0 = 3 * N_HEADS * HEAD_DIM
    assert w_in.shape[1] == u_col0 + ssm_width + 2 * d

    x2 = x.reshape(t, d)
    tn = GROUP_WIDTH
    proj = _inproj(x2, norm_mix_pre, w_in.astype(BF16), ssm_width)
    attn = _attention(proj, bsz, seq)
    ssm_w = _ssm_weights(ssm_a_re, ssm_a_im, ssm_log_dt, ssm_b_re, ssm_b_im, ssm_c_re, ssm_c_im)
    y = _ssm(proj, u_col0 // tn, ssm_w, ssm_d, bsz, seq, ptiles=4, nb=2)
    merged = _merge(attn, y, proj, (u_col0 + ssm_width) // tn, w_attn_up, w_glu_v, w_glu_g)
    x1 = _outproj(merged, w_out.astype(BF16), x2, norm_mix_post, tm=512)
    out = _ffn(x1, norm_ffn_pre, norm_ffn_post, w_ffn_gate.astype(BF16), w_ffn_up.astype(BF16),
               w_ffn_down, tm=512, tf=512)
    return out.reshape(bsz, seq, d)


def kernel(x, norm_mix_pre, w_in, w_attn_up, ssm_a_re, ssm_a_im, ssm_log_dt, ssm_b_re, ssm_b_im, ssm_c_re, ssm_c_im, ssm_d, w_glu_v, w_glu_g, w_out, norm_mix_post, norm_ffn_pre, w_ffn_gate, w_ffn_up, w_ffn_down, norm_ffn_post):
    stacked = (norm_mix_pre, w_in, w_attn_up, ssm_a_re, ssm_a_im, ssm_log_dt, ssm_b_re, ssm_b_im, ssm_c_re,
               ssm_c_im, ssm_d, w_glu_v, w_glu_g, w_out, norm_mix_post, norm_ffn_pre, w_ffn_gate, w_ffn_up,
               w_ffn_down, norm_ffn_post)
    for layer in range(norm_mix_pre.shape[0]):
        x = _layer(x, *(p[layer] for p in stacked))
    return x
```

```python
import functools

import jax
import jax.numpy as jnp
import numpy as np
from jax import lax
from jax.experimental import pallas as pl
from jax.experimental.pallas import tpu as pltpu

F32 = jnp.float32
BF16 = jnp.bfloat16

EPS = 1e-6
HEAD_DIM = 128
HEADS_PER_GROUP = 4
ATTN_GROUPS = ((128, 1), (512, 4), (2048, 16))
N_GROUPS = len(ATTN_GROUPS)
N_HEADS = HEADS_PER_GROUP * N_GROUPS
GROUP_WIDTH = HEADS_PER_GROUP * HEAD_DIM
ATTN_BLK = 128
ATTN_TILE = 2048
SSM_GROUP = 16
SSM_STATE = 64
SSM_CHUNK = 8
LANES = 128
GROUPS_PER_LANE_BLOCK = LANES // SSM_GROUP
PROJ_TILE = 1024
ROW_ORDERS = tuple(dil for _, dil in ATTN_GROUPS) + (SSM_CHUNK,)
PERM_BASE = 4
NEG = -1e30
LOG2E = 1.4426950408889634
VMEM_LIMIT = 56 * 1024 * 1024


def _params(*sem):
    return pltpu.CompilerParams(dimension_semantics=sem, vmem_limit_bytes=VMEM_LIMIT)


def _sigmoid(x):
    return 0.5 * jnp.tanh(0.5 * x) + 0.5


def _rms(x, gain):
    return x * lax.rsqrt(jnp.mean(x * x, axis=-1, keepdims=True) + EPS) * gain


def _residue_block(r, dil):
    if dil <= PERM_BASE:
        return r
    return (r % PERM_BASE) * (dil // PERM_BASE) + r // PERM_BASE


def _inproj_kernel(x_ref, g_ref, w_ref, o_ref, h_ref, hn_ref, hb_ref, inv_ref, *, n_qkv, u_tiles):
    j = pl.program_id(1)
    slabs, tm, _ = hn_ref.shape

    @pl.when(j == 0)
    def _():
        x = x_ref[...]
        inv_ref[...] = jnp.broadcast_to(lax.rsqrt(jnp.mean(x * x, axis=-1, keepdims=True) + EPS), inv_ref.shape)
        q = tm // PERM_BASE
        piece = 256
        for c in range(x_ref.shape[1] // LANES):
            cols = slice(c * LANES, (c + 1) * LANES)
            s = c % slabs
            for r0 in range(0, tm, piece):
                rows = slice(r0, r0 + piece)
                hn = x_ref[rows, cols] * inv_ref[rows, :] * g_ref[:, cols]
                h_ref[0, rows, cols] = hn.astype(BF16)
                hn_ref[s, rows, :] = hn
            for b in range(PERM_BASE):
                part = hn_ref[s, pl.ds(b, q, stride=PERM_BASE), :]
                hb_ref[s, b * q:(b + 1) * q, :] = part
                for v, dil in enumerate(ROW_ORDERS):
                    if dil == PERM_BASE:
                        h_ref[v, b * q:(b + 1) * q, cols] = part.astype(BF16)
            for v, dil in enumerate(ROW_ORDERS):
                if dil > PERM_BASE:
                    k, n = dil // PERM_BASE, tm // dil
                    for b in range(PERM_BASE):
                        for a in range(k):
                            blk = b * k + a
                            h_ref[v, blk * n:(blk + 1) * n, cols] = (
                                hb_ref[s, pl.ds(b * q + a, n, stride=k), :].astype(BF16))

    order = jnp.where(j < n_qkv, j % N_GROUPS, jnp.where(j < n_qkv + u_tiles, N_GROUPS, 0))
    o_ref[...] = jnp.dot(h_ref[order], w_ref[...], preferred_element_type=F32).astype(o_ref.dtype)


def _inproj(x, gain, w, ssm_width):
    t, d = x.shape
    tm = PROJ_TILE
    tn = GROUP_WIDTH
    ncol = w.shape[1] // tn
    assert t % tm == 0 and w.shape[1] % tn == 0 and ssm_width % tn == 0 and d % LANES == 0
    assert ROW_ORDERS[0] == 1 and all(tm % (dil * 16) == 0 for dil in ROW_ORDERS)
    assert all(dil in (1, PERM_BASE) or (dil % PERM_BASE == 0 and dil // PERM_BASE <= PERM_BASE) for dil in ROW_ORDERS)
    slabs = 8
    return pl.pallas_call(
        functools.partial(_inproj_kernel, n_qkv=3 * N_GROUPS, u_tiles=ssm_width // tn),
        grid=(t // tm, ncol),
        in_specs=[
            pl.BlockSpec((tm, d), lambda i, j: (i, 0)),
            pl.BlockSpec((1, d), lambda i, j: (0, 0)),
            pl.BlockSpec((d, tn), lambda i, j: (0, j)),
        ],
        out_specs=pl.BlockSpec((None, None, tm, tn), lambda i, j: (i, j, 0, 0)),
        out_shape=jax.ShapeDtypeStruct((t // tm, ncol, tm, tn), BF16),
        scratch_shapes=[
            pltpu.VMEM((len(ROW_ORDERS), tm, d), BF16),
            pltpu.VMEM((slabs, tm, LANES), F32),
            pltpu.VMEM((slabs, tm, LANES), F32),
            pltpu.VMEM((tm, LANES), F32),
        ],
        compiler_params=_params("parallel", "arbitrary"),
        name="inproj",
    )(x, gain.reshape(1, d), w)


def _attn_bias_table():
    qi = np.arange(ATTN_BLK)[:, None]
    kj = np.arange(ATTN_BLK)[None, :]
    table = np.full((N_GROUPS, 2, HEADS_PER_GROUP * ATTN_BLK, 2 * ATTN_BLK), NEG, np.float32)
    for g, (_, dil) in enumerate(ATTN_GROUPS):
        for h in range(HEADS_PER_GROUP):
            slope = 2.0 ** (-8.0 * (g * HEADS_PER_GROUP + h + 1) / N_HEADS) * dil * LOG2E
            rows = slice(h * ATTN_BLK, (h + 1) * ATTN_BLK)
            cur = np.where(kj <= qi, -slope * (qi - kj), NEG)
            prev = np.where(kj >= qi, -slope * (ATTN_BLK + qi - kj), NEG)
            table[g, :, rows, ATTN_BLK:] = cur
            table[g, 0, rows, :ATTN_BLK] = prev
    return table


def _attn_kernel(bias_ref, *refs):
    qkv_refs = refs[:3 * N_GROUPS]
    o_ref, out_scr, lse_scr = refs[3 * N_GROUPS:3 * N_GROUPS + 3]
    hist_refs = refs[3 * N_GROUPS + 3:]
    tile = pl.program_id(1)
    u = pl.program_id(2)
    units = ATTN_TILE // ATTN_BLK
    nt = (((1,), (1,)), ((), ()))

    @pl.when(jnp.logical_and(tile == 0, u == 0))
    def _():
        for hist in hist_refs:
            hist[...] = jnp.zeros_like(hist)

    blocks, logits, stats = [], [], []
    for g, (_, dil) in enumerate(ATTN_GROUPS):
        nb = u // dil
        r = u % dil
        first = jnp.logical_and(tile == 0, nb == 0).astype(jnp.int32)
        start = nb * (ATTN_BLK * dil) + r
        rows = pl.ds(start, ATTN_BLK) if dil == 1 else pl.ds(start, ATTN_BLK, stride=dil)
        q, k, v = (ref[...].reshape(ATTN_BLK, GROUP_WIDTH) for ref in qkv_refs[3 * g:3 * g + 3])
        prev = hist_refs[g][r]
        scores = []
        for h in range(HEADS_PER_GROUP):
            cs = slice(h * HEAD_DIM, (h + 1) * HEAD_DIM)
            keys = jnp.concatenate([prev[:, cs], k[:, cs]], axis=0)
            scores.append(lax.dot_general(q[:, cs], keys, nt, preferred_element_type=F32))
        logits.append(jnp.concatenate(scores, axis=0) * (HEAD_DIM ** -0.5 * LOG2E) + bias_ref[g, first])
        blocks.append((r, rows, k, v, prev))
    for s in logits:
        m = jnp.max(s, axis=1, keepdims=True)
        p = jnp.exp2(s - m)
        l = jnp.sum(p, axis=1, keepdims=True)
        stats.append((p.astype(BF16), 1.0 / l, m + jnp.log2(l)))
    for g, ((r, rows, k, v, prev), (p, inv, lse)) in enumerate(zip(blocks, stats)):
        for h in range(HEADS_PER_GROUP):
            cs = slice(h * HEAD_DIM, (h + 1) * HEAD_DIM)
            hr = slice(h * ATTN_BLK, (h + 1) * ATTN_BLK)
            vals = jnp.concatenate([prev[:, GROUP_WIDTH:][:, cs], v[:, cs]], axis=0)
            o = jnp.dot(p[hr], vals, preferred_element_type=F32)
            out_scr[g, h, rows, :] = o * inv[hr]
            lse_scr[g, h, rows, :] = jnp.broadcast_to(lse[hr], (ATTN_BLK, HEAD_DIM))
        hist_refs[g][r, :, :GROUP_WIDTH] = k
        hist_refs[g][r, :, GROUP_WIDTH:] = v

    @pl.when(u == units - 1)
    def _():
        for h in range(HEADS_PER_GROUP):
            lses = [lse_scr[g, h] for g in range(N_GROUPS)]
            top = functools.reduce(jnp.maximum, lses)
            ws = [jnp.exp2(x - top) for x in lses]
            num = sum(w * out_scr[g, h] for g, w in enumerate(ws))
            o_ref[:, h * HEAD_DIM:(h + 1) * HEAD_DIM] = (num / sum(ws)).astype(o_ref.dtype)


def _attention(proj, bsz, seq):
    assert seq % ATTN_TILE == 0 and ATTN_TILE % PROJ_TILE == 0
    tiles = seq // ATTN_TILE
    units = ATTN_TILE // ATTN_BLK
    bias = jnp.asarray(_attn_bias_table())
    in_specs = [pl.BlockSpec(bias.shape, lambda b, tile, u: (0, 0, 0, 0))]
    hist = []
    for g, (window, dil) in enumerate(ATTN_GROUPS):
        assert window // dil == ATTN_BLK and units % dil == 0
        span = ATTN_BLK * dil
        spans_per_tile = ATTN_TILE // span
        if span <= PROJ_TILE:
            block = (None, None, ATTN_BLK, GROUP_WIDTH)

            def index(b, tile, u, *, col, dil=dil, spt=spans_per_tile, sppt=PROJ_TILE // span,
                      per_seq=seq // PROJ_TILE):
                sp = tile * spt + u // dil
                return b * per_seq + sp // sppt, col, _residue_block(u % dil, dil) * sppt + sp % sppt, 0
        else:
            block = (span // PROJ_TILE, None, PROJ_TILE // dil, GROUP_WIDTH)

            def index(b, tile, u, *, col, dil=dil, spt=spans_per_tile, per_seq=seq // span):
                return b * per_seq + tile * spt + u // dil, col, _residue_block(u % dil, dil), 0

        for part in range(3):
            in_specs.append(pl.BlockSpec(block, functools.partial(index, col=part * N_GROUPS + g)))
        hist.append(pltpu.VMEM((dil, ATTN_BLK, 2 * GROUP_WIDTH), BF16))
    scratch = pltpu.VMEM((N_GROUPS, HEADS_PER_GROUP, ATTN_TILE, HEAD_DIM), F32)
    return pl.pallas_call(
        _attn_kernel,
        grid=(bsz, tiles, units),
        in_specs=in_specs,
        out_specs=pl.BlockSpec((ATTN_TILE, GROUP_WIDTH), lambda b, tile, u: (b * tiles + tile, 0)),
        out_shape=jax.ShapeDtypeStruct((bsz * seq, GROUP_WIDTH), BF16),
        scratch_shapes=[scratch, scratch] + hist,
        compiler_params=_params("arbitrary", "arbitrary", "arbitrary"),
        name="dilated_attention",
    )(bias, *([proj] * (3 * N_GROUPS)))


def _cmul(a, b):
    return a[0] * b[0] - a[1] * b[1], a[0] * b[1] + a[1] * b[0]


def _ssm_weights(a_re, a_im, log_dt, b_re, b_im, c_re, c_im):
    n_groups = a_re.shape[0]
    nblk = n_groups // GROUPS_PER_LANE_BLOCK
    gl = GROUPS_PER_LANE_BLOCK
    L = SSM_CHUNK
    a_re, a_im = a_re.astype(F32), a_im.astype(F32)
    dt = jnp.exp(log_dt.astype(F32))[:, None]
    steps = jnp.arange(L + 1, dtype=F32)[None, :, None]
    mag = jnp.exp((a_re * dt)[:, None, :] * steps)
    ang = (a_im * dt)[:, None, :] * steps
    powers = (mag * jnp.cos(ang), mag * jnp.sin(ang))
    lam_bar = (powers[0][:, 1], powers[1][:, 1])
    den = a_re * a_re + a_im * a_im
    num = (lam_bar[0] - 1.0, lam_bar[1])
    ratio = ((num[0] * a_re + num[1] * a_im) / den, (num[1] * a_re - num[0] * a_im) / den)
    b_bar = _cmul((ratio[0][..., None], ratio[1][..., None]), (b_re.astype(F32), b_im.astype(F32)))
    c_t = (c_re.astype(F32).transpose(0, 2, 1), c_im.astype(F32).transpose(0, 2, 1))
    eye = jnp.eye(gl, dtype=F32)

    def block_diag(m):
        rows, cols = m.shape[1:]
        m = m.reshape(nblk, gl, rows, 1, cols) * eye[None, :, None, :, None]
        return m.reshape(nblk, gl * rows, gl * cols)

    b_in = jnp.stack([block_diag(b.transpose(0, 2, 1)) for b in b_bar], axis=1)
    c_out = jnp.stack([block_diag(c) for c in c_t], axis=1)
    pw = jnp.stack(powers, axis=0).reshape(2, nblk, gl, L + 1, SSM_STATE)
    pw_row = pw.transpose(1, 0, 3, 2, 4).reshape(nblk, 2, L + 1, gl * SSM_STATE)
    pw_col = pw_row.transpose(0, 1, 3, 2)
    return b_in, c_out, pw_row, pw_col


def _ssm_kernel(*refs, tiles_per_seq):
    u_refs = refs[:SSM_CHUNK]
    (bin_ref, cout_ref, pwr_ref, pwc_ref, d_ref, y_ref,
     wt_scr, wb_scr, wl_scr, wc_scr, s_scr, xp_scr, carry_scr, y_scr) = refs[SSM_CHUNK:]
    L = SSM_CHUNK
    nb = wt_scr.shape[0]
    tc = s_scr.shape[0]
    half = s_scr.shape[1] // 2
    sw = half // nb

    @pl.when(pl.program_id(1) == 0)
    def _():
        for q in range(nb):
            c_hi = [cout_ref[q, ri].astype(BF16) for ri in range(2)]
            c_lo = [(cout_ref[q, ri] - c_hi[ri].astype(F32)).astype(BF16) for ri in range(2)]
            for j in range(L):
                rows = slice(j * LANES, (j + 1) * LANES)
                n = L - 1 - j
                pr = pwr_ref[q, 0, n:n + 1, :]
                pi = pwr_ref[q, 1, n:n + 1, :]
                for ri, val in enumerate((bin_ref[q, 0] * pr - bin_ref[q, 1] * pi,
                                          bin_ref[q, 0] * pi + bin_ref[q, 1] * pr)):
                    cols = slice(ri * sw, (ri + 1) * sw)
                    top = val.astype(BF16)
                    wb_scr[q, rows, cols] = top
                    wl_scr[rows, cols] = (val - top.astype(F32)).astype(BF16)
                pr = pwc_ref[q, 0, :, j + 1:j + 2]
                pi = pwc_ref[q, 1, :, j + 1:j + 2]
                wc_scr[q, :sw, rows] = (cout_ref[q, 0] * pr - cout_ref[q, 1] * pi).astype(BF16)
                wc_scr[q, sw:, rows] = (-(cout_ref[q, 0] * pi + cout_ref[q, 1] * pr)).astype(BF16)
            prods = []
            for ri in range(2):
                cols = slice(ri * sw, (ri + 1) * sw)
                top, low = wb_scr[q, :, cols], wl_scr[:, cols]
                prods.append(jnp.dot(top, c_hi[ri], preferred_element_type=F32)
                             + jnp.dot(top, c_lo[ri], preferred_element_type=F32)
                             + jnp.dot(low, c_hi[ri], preferred_element_type=F32))
            k_all = (prods[0] - prods[1]).astype(BF16)
            for j in range(L):
                n = L - 1 - j
                kn = k_all[j * LANES:(j + 1) * LANES]
                for jj in range(L):
                    ii = jj + n
                    if ii < L:
                        wt_scr[q, jj * LANES:(jj + 1) * LANES, ii * LANES:(ii + 1) * LANES] = kn
                    if j < jj:
                        wt_scr[q, jj * LANES:(jj + 1) * LANES, j * LANES:(j + 1) * LANES] = jnp.zeros((LANES, LANES), BF16)

    @pl.when(pl.program_id(1) % tiles_per_seq == 0)
    def _():
        carry_scr[...] = jnp.zeros_like(carry_scr)

    us = [r[...].reshape(tc, nb * LANES) for r in u_refs]
    ucat = [jnp.concatenate([u[:, q * LANES:(q + 1) * LANES] for u in us], axis=1)
            for q in range(nb)]
    for q in range(nb):
        s = jnp.dot(ucat[q], wb_scr[q], preferred_element_type=F32)
        s_scr[:, q * sw:(q + 1) * sw] = s[:, :sw]
        s_scr[:, half + q * sw:half + (q + 1) * sw] = s[:, sw:]
    ys = [jnp.dot(ucat[q], wt_scr[q], preferred_element_type=F32) for q in range(nb)]
    for q in range(nb):
        re = slice(q * sw, (q + 1) * sw)
        im = slice(half + q * sw, half + (q + 1) * sw)
        ar = pwr_ref[q, 0, L:L + 1, :]
        ai = pwr_ref[q, 1, L:L + 1, :]

        xr, xi = carry_scr[:, re], carry_scr[:, im]
        for c in range(tc):
            xp_scr[c:c + 1, re] = xr
            xp_scr[c:c + 1, im] = xi
            xr, xi = (ar * xr - ai * xi + s_scr[c:c + 1, re], ar * xi + ai * xr + s_scr[c:c + 1, im])
        carry_scr[:, re] = xr
        carry_scr[:, im] = xi

    for q in range(nb):
        lanes = slice(q * LANES, (q + 1) * LANES)
        xq = jnp.concatenate([xp_scr[:, q * sw:(q + 1) * sw], xp_scr[:, half + q * sw:half + (q + 1) * sw]], axis=1)
        y = ys[q] + jnp.dot(xq.astype(BF16), wc_scr[q], preferred_element_type=F32)
        for i in range(L):
            yi = y[:, i * LANES:(i + 1) * LANES] + d_ref[0, :, lanes] * us[i][:, lanes].astype(F32)
            y_scr[q, pl.ds(i, tc, stride=L), :] = jax.nn.gelu(yi)
        y_ref[:, lanes] = y_scr[q].astype(y_ref.dtype)


def _ssm(proj, u_tile0, ssm_w, d_skip, bsz, seq, *, ptiles, nb):
    nt, _, tm, tn = proj.shape
    L = SSM_CHUNK
    per_tile = tn // (nb * LANES)
    crows = tm // L
    nblk = ssm_w[0].shape[0]
    width = nblk * LANES
    assert crows == LANES and nblk % nb == 0 and tn % (nb * LANES) == 0
    tc = ptiles * crows
    rows_per_seq = seq // L
    assert rows_per_seq % tc == 0 and nt % ptiles == 0
    u_specs = [
        pl.BlockSpec((ptiles, None, crows, nb * LANES), functools.partial(
            lambda blk, i, j: (i, u_tile0 + blk // per_tile, j, blk % per_tile), j=_residue_block(j, L)))
        for j in range(L)
    ]
    w_specs = [pl.BlockSpec((nb,) + w.shape[1:], lambda blk, i: (blk, 0, 0, 0)) for w in ssm_w]
    wide = L * LANES
    states = 2 * GROUPS_PER_LANE_BLOCK * SSM_STATE
    return pl.pallas_call(
        functools.partial(_ssm_kernel, tiles_per_seq=rows_per_seq // tc),
        grid=(nblk // nb, nt // ptiles),
        in_specs=u_specs + w_specs + [pl.BlockSpec((1, 1, nb * LANES), lambda blk, i: (blk, 0, 0))],
        out_specs=pl.BlockSpec((tc * L, nb * LANES), lambda blk, i: (i, blk)),
        out_shape=jax.ShapeDtypeStruct((nt * tm, width), BF16),
        scratch_shapes=[
            pltpu.VMEM((nb, wide, wide), BF16),
            pltpu.VMEM((nb, wide, states), BF16),
            pltpu.VMEM((wide, states), BF16),
            pltpu.VMEM((nb, states, wide), BF16),
            pltpu.VMEM((tc, nb * states), F32),
            pltpu.VMEM((tc, nb * states), F32),
            pltpu.VMEM((1, nb * states), F32),
            pltpu.VMEM((nb, tc * L, LANES), F32),
        ],
        compiler_params=_params("parallel", "arbitrary"),
        name="s5_chunked",
    )(*([proj] * L), *ssm_w, d_skip.astype(F32).reshape(nblk // nb, 1, nb * LANES))


def _merge_kernel(attn_ref, y_ref, ga_ref, gs_ref, wup_ref, wv_ref, wg_ref, o_ref):
    y = y_ref[...]
    gate = _sigmoid(jnp.dot(y, wg_ref[...].astype(BF16), preferred_element_type=F32))
    gate_s = _sigmoid(gs_ref[...].astype(F32))
    gate_a = _sigmoid(ga_ref[...].astype(F32))
    val = jnp.dot(y, wv_ref[...].astype(BF16), preferred_element_type=F32)
    attn_branch = jnp.dot(attn_ref[...], wup_ref[...].astype(BF16), preferred_element_type=F32)
    merged = gate_a * attn_branch + gate_s * (val * gate)
    o_ref[...] = merged.astype(o_ref.dtype)


def _merge(attn, y, proj, gate_tile0, w_up, w_v, w_g):
    t = attn.shape[0]
    n = w_up.shape[1]
    _, _, tm, tn = proj.shape
    assert n % tn == 0
    return pl.pallas_call(
        _merge_kernel,
        grid=(t // tm, n // tn),
        in_specs=[
            pl.BlockSpec((tm, attn.shape[1]), lambda i, j: (i, 0)),
            pl.BlockSpec((tm, y.shape[1]), lambda i, j: (i, 0)),
            pl.BlockSpec((None, None, tm, tn), lambda i, j: (i, gate_tile0 + j, 0, 0)),
            pl.BlockSpec((None, None, tm, tn), lambda i, j: (i, gate_tile0 + n // tn + j, 0, 0)),
            pl.BlockSpec((w_up.shape[0], tn), lambda i, j: (0, j)),
            pl.BlockSpec((w_v.shape[0], tn), lambda i, j: (0, j)),
            pl.BlockSpec((w_g.shape[0], tn), lambda i, j: (0, j)),
        ],
        out_specs=pl.BlockSpec((tm, tn), lambda i, j: (i, j)),
        out_shape=jax.ShapeDtypeStruct((t, n), BF16),
        compiler_params=_params("parallel", "arbitrary"),
        name="gated_merge",
    )(attn, y, proj, proj, w_up, w_v, w_g)


def _outproj_kernel(m_ref, w_ref, x_ref, g_ref, o_ref):
    z = jnp.dot(m_ref[...], w_ref[...], preferred_element_type=F32)
    o_ref[...] = x_ref[...] + _rms(z, g_ref[...])


def _outproj(merged, w, x, gain, *, tm):
    t, d = x.shape
    return pl.pallas_call(
        _outproj_kernel,
        grid=(t // tm,),
        in_specs=[
            pl.BlockSpec((tm, merged.shape[1]), lambda i: (i, 0)),
            pl.BlockSpec(w.shape, lambda i: (0, 0)),
            pl.BlockSpec((tm, d), lambda i: (i, 0)),
            pl.BlockSpec((1, d), lambda i: (0, 0)),
        ],
        out_specs=pl.BlockSpec((tm, d), lambda i: (i, 0)),
        out_shape=jax.ShapeDtypeStruct((t, d), F32),
        compiler_params=_params("parallel"),
        name="outproj_norm_residual",
    )(merged, w, x, gain.reshape(1, d))


def _ffn_kernel(x_ref, gpre_ref, gpost_ref, wg_ref, wu_ref, wd_ref, o_ref, h_ref, acc_ref):
    k = pl.program_id(1)

    @pl.when(k == 0)
    def _():
        h_ref[...] = _rms(x_ref[...], gpre_ref[...]).astype(BF16)
        acc_ref[...] = jnp.zeros_like(acc_ref)

    h = h_ref[...]
    gate = jnp.dot(h, wg_ref[...], preferred_element_type=F32)
    up = jnp.dot(h, wu_ref[...], preferred_element_type=F32)
    f = (jax.nn.silu(gate) * up).astype(BF16)
    acc_ref[...] += jnp.dot(f, wd_ref[...], preferred_element_type=F32)

    @pl.when(k == pl.num_programs(1) - 1)
    def _():
        o_ref[...] = x_ref[...] + _rms(acc_ref[...], gpost_ref[...])


def _ffn(x, gain_pre, gain_post, w_gate, w_up, w_down, *, tm, tf):
    t, d = x.shape
    dff = w_gate.shape[1]
    return pl.pallas_call(
        _ffn_kernel,
        grid=(t // tm, dff // tf),
        in_specs=[
            pl.BlockSpec((tm, d), lambda i, k: (i, 0)),
            pl.BlockSpec((1, d), lambda i, k: (0, 0)),
            pl.BlockSpec((1, d), lambda i, k: (0, 0)),
            pl.BlockSpec((d, tf), lambda i, k: (0, k)),
            pl.BlockSpec((d, tf), lambda i, k: (0, k)),
            pl.BlockSpec((tf, d), lambda i, k: (k, 0)),
        ],
        out_specs=pl.BlockSpec((tm, d), lambda i, k: (i, 0)),
        out_shape=jax.ShapeDtypeStruct((t, d), F32),
        scratch_shapes=[pltpu.VMEM((tm, d), BF16), pltpu.VMEM((tm, d), F32)],
        compiler_params=_params("parallel", "arbitrary"),
        name="swiglu_ffn",
    )(x, gain_pre.reshape(1, d), gain_post.reshape(1, d), w_gate, w_up, w_down)


def _layer(x, norm_mix_pre, w_in, w_attn_up, ssm_a_re, ssm_a_im, ssm_log_dt, ssm_b_re, ssm_b_im,
           ssm_c_re, ssm_c_im, ssm_d, w_glu_v, w_glu_g, w_out, norm_mix_post, norm_ffn_pre,
           w_ffn_gate, w_ffn_up, w_ffn_down, norm_ffn_post):
    bsz, seq, d = x.shape
    t = bsz * seq
    ssm_width = ssm_d.shape[0]
    u_col0 = 3 * N_HEADS * HEAD_DIM
    assert w_in.shape[1] == u_col0 + ssm_width + 2 * d

    x2 = x.reshape(t, d)
    tn = GROUP_WIDTH
    proj = _inproj(x2, norm_mix_pre, w_in.astype(BF16), ssm_width)
    attn = _attention(proj, bsz, seq)
    ssm_w = _ssm_weights(ssm_a_re, ssm_a_im, ssm_log_dt, ssm_b_re, ssm_b_im, ssm_c_re, ssm_c_im)
    y = _ssm(proj, u_col0 // tn, ssm_w, ssm_d, bsz, seq, ptiles=4, nb=2)
    merged = _merge(attn, y, proj, (u_col0 + ssm_width) // tn, w_attn_up, w_glu_v, w_glu_g)
    x1 = _outproj(merged, w_out.astype(BF16), x2, norm_mix_post, tm=512)
    out = _ffn(x1, norm_ffn_pre, norm_ffn_post, w_ffn_gate.astype(BF16), w_ffn_up.astype(BF16),
               w_ffn_down.astype(BF16), tm=512, tf=512)
    return out.reshape(bsz, seq, d)


def kernel(x, norm_mix_pre, w_in, w_attn_up, ssm_a_re, ssm_a_im, ssm_log_dt, ssm_b_re, ssm_b_im, ssm_c_re, ssm_c_im, ssm_d, w_glu_v, w_glu_g, w_out, norm_mix_post, norm_ffn_pre, w_ffn_gate, w_ffn_up, w_ffn_down, norm_ffn_post):
    stacked = (norm_mix_pre, w_in, w_attn_up, ssm_a_re, ssm_a_im, ssm_log_dt, ssm_b_re, ssm_b_im, ssm_c_re,
               ssm_c_im, ssm_d, w_glu_v, w_glu_g, w_out, norm_mix_post, norm_ffn_pre, w_ffn_gate, w_ffn_up,
               w_ffn_down, norm_ffn_post)
    for layer in range(norm_mix_pre.shape[0]):
        x = _layer(x, *(p[layer] for p in stacked))
    return x
```

```python
import functools

import jax
import jax.numpy as jnp
import numpy as np
from jax import lax
from jax.experimental import pallas as pl
from jax.experimental.pallas import tpu as pltpu

F32 = jnp.float32
BF16 = jnp.bfloat16

EPS = 1e-6
HEAD_DIM = 128
HEADS_PER_GROUP = 4
ATTN_GROUPS = ((128, 1), (512, 4), (2048, 16))
N_GROUPS = len(ATTN_GROUPS)
N_HEADS = HEADS_PER_GROUP * N_GROUPS
GROUP_WIDTH = HEADS_PER_GROUP * HEAD_DIM
ATTN_BLK = 128
ATTN_TILE = 2048
SSM_GROUP = 16
SSM_STATE = 64
SSM_CHUNK = 8
LANES = 128
GROUPS_PER_LANE_BLOCK = LANES // SSM_GROUP
PROJ_TILE = 1024
ROW_ORDERS = tuple(dil for _, dil in ATTN_GROUPS) + (SSM_CHUNK,)
PERM_BASE = 4
NEG = -1e30
LOG2E = 1.4426950408889634
VMEM_LIMIT = 56 * 1024 * 1024


def _params(*sem):
    return pltpu.CompilerParams(dimension_semantics=sem, vmem_limit_bytes=VMEM_LIMIT)


def _sigmoid(x):
    return 0.5 * jnp.tanh(0.5 * x) + 0.5


def _rms(x, gain):
    return x * lax.rsqrt(jnp.mean(x * x, axis=-1, keepdims=True) + EPS) * gain


def _cast_riders(weights, grid):
    ni, nj = grid
    in_specs, out_specs, out_shapes = [], [], []
    for w, ncols in weights:
        k, n = w.shape
        assert k % (ni * 16) == 0 and n % (ncols * LANES) == 0 and ncols <= nj
        spec = pl.BlockSpec((k // ni, n // ncols), functools.partial(
            lambda i, j, last: (i, jnp.minimum(j, last)), last=ncols - 1))
        in_specs.append(spec)
        out_specs.append(spec)
        out_shapes.append(jax.ShapeDtypeStruct(w.shape, BF16))
    return in_specs, out_specs, out_shapes


def _residue_block(r, dil):
    if dil <= PERM_BASE:
        return r
    return (r % PERM_BASE) * (dil // PERM_BASE) + r // PERM_BASE


def _inproj_kernel(x_ref, g_ref, w_ref, *refs, n_qkv, u_tiles, riders):
    ride_in, (o_ref, *ride_out) = refs[:riders], refs[riders:2 * riders + 1]
    h_ref, hn_ref, hb_ref, inv_ref = refs[2 * riders + 1:]
    j = pl.program_id(1)
    slabs, tm, _ = hn_ref.shape

    @pl.when(j == 0)
    def _():
        x = x_ref[...]
        inv_ref[...] = jnp.broadcast_to(lax.rsqrt(jnp.mean(x * x, axis=-1, keepdims=True) + EPS), inv_ref.shape)
        q = tm // PERM_BASE
        piece = 256
        for c in range(x_ref.shape[1] // LANES):
            cols = slice(c * LANES, (c + 1) * LANES)
            s = c % slabs
            for r0 in range(0, tm, piece):
                rows = slice(r0, r0 + piece)
                hn = x_ref[rows, cols] * inv_ref[rows, :] * g_ref[:, cols]
                h_ref[0, rows, cols] = hn.astype(BF16)
                hn_ref[s, rows, :] = hn
            for b in range(PERM_BASE):
                part = hn_ref[s, pl.ds(b, q, stride=PERM_BASE), :]
                hb_ref[s, b * q:(b + 1) * q, :] = part
                for v, dil in enumerate(ROW_ORDERS):
                    if dil == PERM_BASE:
                        h_ref[v, b * q:(b + 1) * q, cols] = part.astype(BF16)
            for v, dil in enumerate(ROW_ORDERS):
                if dil > PERM_BASE:
                    k, n = dil // PERM_BASE, tm // dil
                    for b in range(PERM_BASE):
                        for a in range(k):
                            blk = b * k + a
                            h_ref[v, blk * n:(blk + 1) * n, cols] = (
                                hb_ref[s, pl.ds(b * q + a, n, stride=k), :].astype(BF16))

    order = jnp.where(j < n_qkv, j % N_GROUPS, jnp.where(j < n_qkv + u_tiles, N_GROUPS, 0))
    o_ref[...] = jnp.dot(h_ref[order], w_ref[...], preferred_element_type=F32).astype(o_ref.dtype)
    for src, dst in zip(ride_in, ride_out):
        dst[...] = src[...].astype(dst.dtype)


def _inproj(x, gain, w, ssm_width, ride):
    t, d = x.shape
    tm = PROJ_TILE
    tn = GROUP_WIDTH
    ncol = w.shape[1] // tn
    assert t % tm == 0 and w.shape[1] % tn == 0 and ssm_width % tn == 0 and d % LANES == 0
    assert ROW_ORDERS[0] == 1 and all(tm % (dil * 16) == 0 for dil in ROW_ORDERS)
    assert all(dil in (1, PERM_BASE) or (dil % PERM_BASE == 0 and dil // PERM_BASE <= PERM_BASE) for dil in ROW_ORDERS)
    slabs = 8
    grid = (t // tm, ncol)
    ride_in, ride_out, ride_shapes = _cast_riders(ride, grid)
    return pl.pallas_call(
        functools.partial(_inproj_kernel, n_qkv=3 * N_GROUPS, u_tiles=ssm_width // tn, riders=len(ride)),
        grid=grid,
        in_specs=[
            pl.BlockSpec((tm, d), lambda i, j: (i, 0)),
            pl.BlockSpec((1, d), lambda i, j: (0, 0)),
            pl.BlockSpec((d, tn), lambda i, j: (0, j)),
        ] + ride_in,
        out_specs=[pl.BlockSpec((None, None, tm, tn), lambda i, j: (i, j, 0, 0))] + ride_out,
        out_shape=[jax.ShapeDtypeStruct((t // tm, ncol, tm, tn), BF16)] + ride_shapes,
        scratch_shapes=[
            pltpu.VMEM((len(ROW_ORDERS), tm, d), BF16),
            pltpu.VMEM((slabs, tm, LANES), F32),
            pltpu.VMEM((slabs, tm, LANES), F32),
            pltpu.VMEM((tm, LANES), F32),
        ],
        compiler_params=_params("parallel", "arbitrary"),
        name="inproj",
    )(x, gain.reshape(1, d), w, *(w for w, _ in ride))


def _attn_bias_table():
    qi = np.arange(ATTN_BLK)[:, None]
    kj = np.arange(ATTN_BLK)[None, :]
    table = np.full((N_GROUPS, 2, HEADS_PER_GROUP * ATTN_BLK, 2 * ATTN_BLK), NEG, np.float32)
    for g, (_, dil) in enumerate(ATTN_GROUPS):
        for h in range(HEADS_PER_GROUP):
            slope = 2.0 ** (-8.0 * (g * HEADS_PER_GROUP + h + 1) / N_HEADS) * dil * LOG2E
            rows = slice(h * ATTN_BLK, (h + 1) * ATTN_BLK)
            cur = np.where(kj <= qi, -slope * (qi - kj), NEG)
            prev = np.where(kj >= qi, -slope * (ATTN_BLK + qi - kj), NEG)
            table[g, :, rows, ATTN_BLK:] = cur
            table[g, 0, rows, :ATTN_BLK] = prev
    return table


def _attn_kernel(bias_ref, *refs):
    qkv_refs = refs[:3 * N_GROUPS]
    o_ref, out_scr, lse_scr = refs[3 * N_GROUPS:3 * N_GROUPS + 3]
    hist_refs = refs[3 * N_GROUPS + 3:]
    tile = pl.program_id(1)
    u = pl.program_id(2)
    units = ATTN_TILE // ATTN_BLK
    nt = (((1,), (1,)), ((), ()))

    @pl.when(jnp.logical_and(tile == 0, u == 0))
    def _():
        for hist in hist_refs:
            hist[...] = jnp.zeros_like(hist)

    blocks, logits, stats = [], [], []
    for g, (_, dil) in enumerate(ATTN_GROUPS):
        nb = u // dil
        r = u % dil
        first = jnp.logical_and(tile == 0, nb == 0).astype(jnp.int32)
        start = nb * (ATTN_BLK * dil) + r
        rows = pl.ds(start, ATTN_BLK) if dil == 1 else pl.ds(start, ATTN_BLK, stride=dil)
        q, k, v = (ref[...].reshape(ATTN_BLK, GROUP_WIDTH) for ref in qkv_refs[3 * g:3 * g + 3])
        prev = hist_refs[g][r]
        scores = []
        for h in range(HEADS_PER_GROUP):
            cs = slice(h * HEAD_DIM, (h + 1) * HEAD_DIM)
            keys = jnp.concatenate([prev[:, cs], k[:, cs]], axis=0)
            scores.append(lax.dot_general(q[:, cs], keys, nt, preferred_element_type=F32))
        logits.append(jnp.concatenate(scores, axis=0) * (HEAD_DIM ** -0.5 * LOG2E) + bias_ref[g, first])
        blocks.append((r, rows, k, v, prev))
    for s in logits:
        m = jnp.max(s, axis=1, keepdims=True)
        p = jnp.exp2(s - m)
        l = jnp.sum(p, axis=1, keepdims=True)
        stats.append((p.astype(BF16), 1.0 / l, m + jnp.log2(l)))
    for g, ((r, rows, k, v, prev), (p, inv, lse)) in enumerate(zip(blocks, stats)):
        for h in range(HEADS_PER_GROUP):
            cs = slice(h * HEAD_DIM, (h + 1) * HEAD_DIM)
            hr = slice(h * ATTN_BLK, (h + 1) * ATTN_BLK)
            vals = jnp.concatenate([prev[:, GROUP_WIDTH:][:, cs], v[:, cs]], axis=0)
            o = jnp.dot(p[hr], vals, preferred_element_type=F32)
            out_scr[g, h, rows, :] = o * inv[hr]
            lse_scr[g, h, rows, :] = jnp.broadcast_to(lse[hr], (ATTN_BLK, HEAD_DIM))
        hist_refs[g][r, :, :GROUP_WIDTH] = k
        hist_refs[g][r, :, GROUP_WIDTH:] = v

    @pl.when(u == units - 1)
    def _():
        for h in range(HEADS_PER_GROUP):
            lses = [lse_scr[g, h] for g in range(N_GROUPS)]
            top = functools.reduce(jnp.maximum, lses)
            ws = [jnp.exp2(x - top) for x in lses]
            num = sum(w * out_scr[g, h] for g, w in enumerate(ws))
            o_ref[:, h * HEAD_DIM:(h + 1) * HEAD_DIM] = (num / sum(ws)).astype(o_ref.dtype)


def _attention(proj, bsz, seq):
    assert seq % ATTN_TILE == 0 and ATTN_TILE % PROJ_TILE == 0
    tiles = seq // ATTN_TILE
    units = ATTN_TILE // ATTN_BLK
    bias = jnp.asarray(_attn_bias_table())
    in_specs = [pl.BlockSpec(bias.shape, lambda b, tile, u: (0, 0, 0, 0))]
    hist = []
    for g, (window, dil) in enumerate(ATTN_GROUPS):
        assert window // dil == ATTN_BLK and units % dil == 0
        span = ATTN_BLK * dil
        spans_per_tile = ATTN_TILE // span
        if span <= PROJ_TILE:
            block = (None, None, ATTN_BLK, GROUP_WIDTH)

            def index(b, tile, u, *, col, dil=dil, spt=spans_per_tile, sppt=PROJ_TILE // span,
                      per_seq=seq // PROJ_TILE):
                sp = tile * spt + u // dil
                return b * per_seq + sp // sppt, col, _residue_block(u % dil, dil) * sppt + sp % sppt, 0
        else:
            block = (span // PROJ_TILE, None, PROJ_TILE // dil, GROUP_WIDTH)

            def index(b, tile, u, *, col, dil=dil, spt=spans_per_tile, per_seq=seq // span):
                return b * per_seq + tile * spt + u // dil, col, _residue_block(u % dil, dil), 0

        for part in range(3):
            in_specs.append(pl.BlockSpec(block, functools.partial(index, col=part * N_GROUPS + g)))
        hist.append(pltpu.VMEM((dil, ATTN_BLK, 2 * GROUP_WIDTH), BF16))
    scratch = pltpu.VMEM((N_GROUPS, HEADS_PER_GROUP, ATTN_TILE, HEAD_DIM), F32)
    return pl.pallas_call(
        _attn_kernel,
        grid=(bsz, tiles, units),
        in_specs=in_specs,
        out_specs=pl.BlockSpec((ATTN_TILE, GROUP_WIDTH), lambda b, tile, u: (b * tiles + tile, 0)),
        out_shape=jax.ShapeDtypeStruct((bsz * seq, GROUP_WIDTH), BF16),
        scratch_shapes=[scratch, scratch] + hist,
        compiler_params=_params("arbitrary", "arbitrary", "arbitrary"),
        name="dilated_attention",
    )(bias, *([proj] * (3 * N_GROUPS)))


def _cmul(a, b):
    return a[0] * b[0] - a[1] * b[1], a[0] * b[1] + a[1] * b[0]


def _ssm_weights(a_re, a_im, log_dt, b_re, b_im, c_re, c_im):
    n_groups = a_re.shape[0]
    nblk = n_groups // GROUPS_PER_LANE_BLOCK
    gl = GROUPS_PER_LANE_BLOCK
    L = SSM_CHUNK
    a_re, a_im = a_re.astype(F32), a_im.astype(F32)
    dt = jnp.exp(log_dt.astype(F32))[:, None]
    steps = jnp.arange(L + 1, dtype=F32)[None, :, None]
    mag = jnp.exp((a_re * dt)[:, None, :] * steps)
    ang = (a_im * dt)[:, None, :] * steps
    powers = (mag * jnp.cos(ang), mag * jnp.sin(ang))
    lam_bar = (powers[0][:, 1], powers[1][:, 1])
    den = a_re * a_re + a_im * a_im
    num = (lam_bar[0] - 1.0, lam_bar[1])
    ratio = ((num[0] * a_re + num[1] * a_im) / den, (num[1] * a_re - num[0] * a_im) / den)
    b_bar = _cmul((ratio[0][..., None], ratio[1][..., None]), (b_re.astype(F32), b_im.astype(F32)))
    c_t = (c_re.astype(F32).transpose(0, 2, 1), c_im.astype(F32).transpose(0, 2, 1))
    eye = jnp.eye(gl, dtype=F32)

    def block_diag(m):
        rows, cols = m.shape[1:]
        m = m.reshape(nblk, gl, rows, 1, cols) * eye[None, :, None, :, None]
        return m.reshape(nblk, gl * rows, gl * cols)

    b_in = jnp.stack([block_diag(b.transpose(0, 2, 1)) for b in b_bar], axis=1)
    c_out = jnp.stack([block_diag(c) for c in c_t], axis=1)
    pw = jnp.stack(powers, axis=0).reshape(2, nblk, gl, L + 1, SSM_STATE)
    pw_row = pw.transpose(1, 0, 3, 2, 4).reshape(nblk, 2, L + 1, gl * SSM_STATE)
    pw_col = pw_row.transpose(0, 1, 3, 2)
    return b_in, c_out, pw_row, pw_col


def _ssm_kernel(*refs, tiles_per_seq):
    u_refs = refs[:SSM_CHUNK]
    (bin_ref, cout_ref, pwr_ref, pwc_ref, d_ref, y_ref,
     wt_scr, wb_scr, wl_scr, wc_scr, s_scr, xp_scr, carry_scr, y_scr) = refs[SSM_CHUNK:]
    L = SSM_CHUNK
    nb = wt_scr.shape[0]
    tc = s_scr.shape[0]
    half = s_scr.shape[1] // 2
    sw = half // nb

    @pl.when(pl.program_id(1) == 0)
    def _():
        for q in range(nb):
            c_hi = [cout_ref[q, ri].astype(BF16) for ri in range(2)]
            c_lo = [(cout_ref[q, ri] - c_hi[ri].astype(F32)).astype(BF16) for ri in range(2)]
            for j in range(L):
                rows = slice(j * LANES, (j + 1) * LANES)
                n = L - 1 - j
                pr = pwr_ref[q, 0, n:n + 1, :]
                pi = pwr_ref[q, 1, n:n + 1, :]
                for ri, val in enumerate((bin_ref[q, 0] * pr - bin_ref[q, 1] * pi,
                                          bin_ref[q, 0] * pi + bin_ref[q, 1] * pr)):
                    cols = slice(ri * sw, (ri + 1) * sw)
                    top = val.astype(BF16)
                    wb_scr[q, rows, cols] = top
                    wl_scr[rows, cols] = (val - top.astype(F32)).astype(BF16)
                pr = pwc_ref[q, 0, :, j + 1:j + 2]
                pi = pwc_ref[q, 1, :, j + 1:j + 2]
                wc_scr[q, :sw, rows] = (cout_ref[q, 0] * pr - cout_ref[q, 1] * pi).astype(BF16)
                wc_scr[q, sw:, rows] = (-(cout_ref[q, 0] * pi + cout_ref[q, 1] * pr)).astype(BF16)
            prods = []
            for ri in range(2):
                cols = slice(ri * sw, (ri + 1) * sw)
                top, low = wb_scr[q, :, cols], wl_scr[:, cols]
                prods.append(jnp.dot(top, c_hi[ri], preferred_element_type=F32)
                             + jnp.dot(top, c_lo[ri], preferred_element_type=F32)
                             + jnp.dot(low, c_hi[ri], preferred_element_type=F32))
            k_all = (prods[0] - prods[1]).astype(BF16)
            for j in range(L):
                n = L - 1 - j
                kn = k_all[j * LANES:(j + 1) * LANES]
                for jj in range(L):
                    ii = jj + n
                    if ii < L:
                        wt_scr[q, jj * LANES:(jj + 1) * LANES, ii * LANES:(ii + 1) * LANES] = kn
                    if j < jj:
                        wt_scr[q, jj * LANES:(jj + 1) * LANES, j * LANES:(j + 1) * LANES] = jnp.zeros((LANES, LANES), BF16)

    @pl.when(pl.program_id(1) % tiles_per_seq == 0)
    def _():
        carry_scr[...] = jnp.zeros_like(carry_scr)

    us = [r[...].reshape(tc, nb * LANES) for r in u_refs]
    ucat = [jnp.concatenate([u[:, q * LANES:(q + 1) * LANES] for u in us], axis=1)
            for q in range(nb)]
    for q in range(nb):
        s = jnp.dot(ucat[q], wb_scr[q], preferred_element_type=F32)
        s_scr[:, q * sw:(q + 1) * sw] = s[:, :sw]
        s_scr[:, half + q * sw:half + (q + 1) * sw] = s[:, sw:]
    ys = [jnp.dot(ucat[q], wt_scr[q], preferred_element_type=F32) for q in range(nb)]
    for q in range(nb):
        re = slice(q * sw, (q + 1) * sw)
        im = slice(half + q * sw, half + (q + 1) * sw)
        ar = pwr_ref[q, 0, L:L + 1, :]
        ai = pwr_ref[q, 1, L:L + 1, :]

        xr, xi = carry_scr[:, re], carry_scr[:, im]
        for c in range(tc):
            xp_scr[c:c + 1, re] = xr
            xp_scr[c:c + 1, im] = xi
            xr, xi = (ar * xr - ai * xi + s_scr[c:c + 1, re], ar * xi + ai * xr + s_scr[c:c + 1, im])
        carry_scr[:, re] = xr
        carry_scr[:, im] = xi

    for q in range(nb):
        lanes = slice(q * LANES, (q + 1) * LANES)
        xq = jnp.concatenate([xp_scr[:, q * sw:(q + 1) * sw], xp_scr[:, half + q * sw:half + (q + 1) * sw]], axis=1)
        y = ys[q] + jnp.dot(xq.astype(BF16), wc_scr[q], preferred_element_type=F32)
        for i in range(L):
            yi = y[:, i * LANES:(i + 1) * LANES] + d_ref[0, :, lanes] * us[i][:, lanes].astype(F32)
            y_scr[q, pl.ds(i, tc, stride=L), :] = jax.nn.gelu(yi)
        y_ref[:, lanes] = y_scr[q].astype(y_ref.dtype)


def _ssm(proj, u_tile0, ssm_w, d_skip, bsz, seq, *, ptiles, nb):
    nt, _, tm, tn = proj.shape
    L = SSM_CHUNK
    per_tile = tn // (nb * LANES)
    crows = tm // L
    nblk = ssm_w[0].shape[0]
    width = nblk * LANES
    assert crows == LANES and nblk % nb == 0 and tn % (nb * LANES) == 0
    tc = ptiles * crows
    rows_per_seq = seq // L
    assert rows_per_seq % tc == 0 and nt % ptiles == 0
    u_specs = [
        pl.BlockSpec((ptiles, None, crows, nb * LANES), functools.partial(
            lambda blk, i, j: (i, u_tile0 + blk // per_tile, j, blk % per_tile), j=_residue_block(j, L)))
        for j in range(L)
    ]
    w_specs = [pl.BlockSpec((nb,) + w.shape[1:], lambda blk, i: (blk, 0, 0, 0)) for w in ssm_w]
    wide = L * LANES
    states = 2 * GROUPS_PER_LANE_BLOCK * SSM_STATE
    return pl.pallas_call(
        functools.partial(_ssm_kernel, tiles_per_seq=rows_per_seq // tc),
        grid=(nblk // nb, nt // ptiles),
        in_specs=u_specs + w_specs + [pl.BlockSpec((1, 1, nb * LANES), lambda blk, i: (blk, 0, 0))],
        out_specs=pl.BlockSpec((tc * L, nb * LANES), lambda blk, i: (i, blk)),
        out_shape=jax.ShapeDtypeStruct((nt * tm, width), BF16),
        scratch_shapes=[
            pltpu.VMEM((nb, wide, wide), BF16),
            pltpu.VMEM((nb, wide, states), BF16),
            pltpu.VMEM((wide, states), BF16),
            pltpu.VMEM((nb, states, wide), BF16),
            pltpu.VMEM((tc, nb * states), F32),
            pltpu.VMEM((tc, nb * states), F32),
            pltpu.VMEM((1, nb * states), F32),
            pltpu.VMEM((nb, tc * L, LANES), F32),
        ],
        compiler_params=_params("parallel", "arbitrary"),
        name="s5_chunked",
    )(*([proj] * L), *ssm_w, d_skip.astype(F32).reshape(nblk // nb, 1, nb * LANES))


def _merge_kernel(attn_ref, y_ref, ga_ref, gs_ref, wup_ref, wv_ref, wg_ref, *refs):
    riders = len(refs) // 2
    ride_in, (o_ref, *ride_out) = refs[:riders], refs[riders:]
    y = y_ref[...]
    gate = _sigmoid(jnp.dot(y, wg_ref[...].astype(BF16), preferred_element_type=F32))
    gate_s = _sigmoid(gs_ref[...].astype(F32))
    gate_a = _sigmoid(ga_ref[...].astype(F32))
    val = jnp.dot(y, wv_ref[...].astype(BF16), preferred_element_type=F32)
    attn_branch = jnp.dot(attn_ref[...], wup_ref[...].astype(BF16), preferred_element_type=F32)
    merged = gate_a * attn_branch + gate_s * (val * gate)
    o_ref[...] = merged.astype(o_ref.dtype)
    for src, dst in zip(ride_in, ride_out):
        dst[...] = src[...].astype(dst.dtype)


def _merge(attn, y, proj, gate_tile0, w_up, w_v, w_g, ride):
    t = attn.shape[0]
    n = w_up.shape[1]
    _, _, tm, tn = proj.shape
    assert n % tn == 0
    grid = (t // tm, n // tn)
    ride_in, ride_out, ride_shapes = _cast_riders(ride, grid)
    return pl.pallas_call(
        _merge_kernel,
        grid=grid,
        in_specs=[
            pl.BlockSpec((tm, attn.shape[1]), lambda i, j: (i, 0)),
            pl.BlockSpec((tm, y.shape[1]), lambda i, j: (i, 0)),
            pl.BlockSpec((None, None, tm, tn), lambda i, j: (i, gate_tile0 + j, 0, 0)),
            pl.BlockSpec((None, None, tm, tn), lambda i, j: (i, gate_tile0 + n // tn + j, 0, 0)),
            pl.BlockSpec((w_up.shape[0], tn), lambda i, j: (0, j)),
            pl.BlockSpec((w_v.shape[0], tn), lambda i, j: (0, j)),
            pl.BlockSpec((w_g.shape[0], tn), lambda i, j: (0, j)),
        ] + ride_in,
        out_specs=[pl.BlockSpec((tm, tn), lambda i, j: (i, j))] + ride_out,
        out_shape=[jax.ShapeDtypeStruct((t, n), BF16)] + ride_shapes,
        compiler_params=_params("parallel", "arbitrary"),
        name="gated_merge",
    )(attn, y, proj, proj, w_up, w_v, w_g, *(w for w, _ in ride))


def _outproj_kernel(m_ref, w_ref, x_ref, g_ref, o_ref):
    z = jnp.dot(m_ref[...], w_ref[...], preferred_element_type=F32)
    o_ref[...] = x_ref[...] + _rms(z, g_ref[...])


def _outproj(merged, w, x, gain, *, tm):
    t, d = x.shape
    return pl.pallas_call(
        _outproj_kernel,
        grid=(t // tm,),
        in_specs=[
            pl.BlockSpec((tm, merged.shape[1]), lambda i: (i, 0)),
            pl.BlockSpec(w.shape, lambda i: (0, 0)),
            pl.BlockSpec((tm, d), lambda i: (i, 0)),
            pl.BlockSpec((1, d), lambda i: (0, 0)),
        ],
        out_specs=pl.BlockSpec((tm, d), lambda i: (i, 0)),
        out_shape=jax.ShapeDtypeStruct((t, d), F32),
        compiler_params=_params("parallel"),
        name="outproj_norm_residual",
    )(merged, w, x, gain.reshape(1, d))


def _ffn_kernel(x_ref, gpre_ref, gpost_ref, wg_ref, wu_ref, wd_ref, o_ref, h_ref, acc_ref):
    k = pl.program_id(1)

    @pl.when(k == 0)
    def _():
        h_ref[...] = _rms(x_ref[...], gpre_ref[...]).astype(BF16)
        acc_ref[...] = jnp.zeros_like(acc_ref)

    h = h_ref[...]
    gate = jnp.dot(h, wg_ref[...], preferred_element_type=F32)
    up = jnp.dot(h, wu_ref[...], preferred_element_type=F32)
    f = (jax.nn.silu(gate) * up).astype(BF16)
    acc_ref[...] += jnp.dot(f, wd_ref[...], preferred_element_type=F32)

    @pl.when(k == pl.num_programs(1) - 1)
    def _():
        o_ref[...] = x_ref[...] + _rms(acc_ref[...], gpost_ref[...])


def _ffn(x, gain_pre, gain_post, w_gate, w_up, w_down, *, tm, tf):
    t, d = x.shape
    dff = w_gate.shape[1]
    return pl.pallas_call(
        _ffn_kernel,
        grid=(t // tm, dff // tf),
        in_specs=[
            pl.BlockSpec((tm, d), lambda i, k: (i, 0)),
            pl.BlockSpec((1, d), lambda i, k: (0, 0)),
            pl.BlockSpec((1, d), lambda i, k: (0, 0)),
            pl.BlockSpec((d, tf), lambda i, k: (0, k)),
            pl.BlockSpec((d, tf), lambda i, k: (0, k)),
            pl.BlockSpec((tf, d), lambda i, k: (k, 0)),
        ],
        out_specs=pl.BlockSpec((tm, d), lambda i, k: (i, 0)),
        out_shape=jax.ShapeDtypeStruct((t, d), F32),
        scratch_shapes=[pltpu.VMEM((tm, d), BF16), pltpu.VMEM((tm, d), F32)],
        compiler_params=_params("parallel", "arbitrary"),
        name="swiglu_ffn",
    )(x, gain_pre.reshape(1, d), gain_post.reshape(1, d), w_gate, w_up, w_down)


def _layer(x, norm_mix_pre, w_in, w_attn_up, ssm_a_re, ssm_a_im, ssm_log_dt, ssm_b_re, ssm_b_im,
           ssm_c_re, ssm_c_im, ssm_d, w_glu_v, w_glu_g, w_out, norm_mix_post, norm_ffn_pre,
           w_ffn_gate, w_ffn_up, w_ffn_down, norm_ffn_post):
    bsz, seq, d = x.shape
    t = bsz * seq
    ssm_width = ssm_d.shape[0]
    u_col0 = 3 * N_HEADS * HEAD_DIM
    assert w_in.shape[1] == u_col0 + ssm_width + 2 * d

    x2 = x.reshape(t, d)
    tn = GROUP_WIDTH
    ffn_tiles = w_ffn_gate.shape[1] // 512
    proj, w_gate, w_up = _inproj(x2, norm_mix_pre, w_in.astype(BF16), ssm_width,
                                 [(w_ffn_gate, ffn_tiles), (w_ffn_up, ffn_tiles)])
    attn = _attention(proj, bsz, seq)
    ssm_w = _ssm_weights(ssm_a_re, ssm_a_im, ssm_log_dt, ssm_b_re, ssm_b_im, ssm_c_re, ssm_c_im)
    y = _ssm(proj, u_col0 // tn, ssm_w, ssm_d, bsz, seq, ptiles=4, nb=2)
    merged, w_down, w_outp = _merge(attn, y, proj, (u_col0 + ssm_width) // tn, w_attn_up, w_glu_v, w_glu_g,
                                    [(w_ffn_down, d // tn), (w_out, d // tn)])
    x1 = _outproj(merged, w_outp, x2, norm_mix_post, tm=512)
    out = _ffn(x1, norm_ffn_pre, norm_ffn_post, w_gate, w_up, w_down, tm=512, tf=512)
    return out.reshape(bsz, seq, d)


def kernel(x, norm_mix_pre, w_in, w_attn_up, ssm_a_re, ssm_a_im, ssm_log_dt, ssm_b_re, ssm_b_im, ssm_c_re, ssm_c_im, ssm_d, w_glu_v, w_glu_g, w_out, norm_mix_post, norm_ffn_pre, w_ffn_gate, w_ffn_up, w_ffn_down, norm_ffn_post):
    stacked = (norm_mix_pre, w_in, w_attn_up, ssm_a_re, ssm_a_im, ssm_log_dt, ssm_b_re, ssm_b_im, ssm_c_re,
               ssm_c_im, ssm_d, w_glu_v, w_glu_g, w_out, norm_mix_post, norm_ffn_pre, w_ffn_gate, w_ffn_up,
               w_ffn_down, norm_ffn_post)
    for layer in range(norm_mix_pre.shape[0]):
        x = _layer(x, *(p[layer] for p in stacked))
    return x
```

```python
import functools

import jax
import jax.numpy as jnp
import numpy as np
from jax import lax
from jax.experimental import pallas as pl
from jax.experimental.pallas import tpu as pltpu

F32 = jnp.float32
BF16 = jnp.bfloat16

EPS = 1e-6
HEAD_DIM = 128
HEADS_PER_GROUP = 4
ATTN_GROUPS = ((128, 1), (512, 4), (2048, 16))
N_GROUPS = len(ATTN_GROUPS)
N_HEADS = HEADS_PER_GROUP * N_GROUPS
GROUP_WIDTH = HEADS_PER_GROUP * HEAD_DIM
ATTN_BLK = 128
ATTN_TILE = 2048
SSM_GROUP = 16
SSM_STATE = 64
SSM_CHUNK = 8
LANES = 128
GROUPS_PER_LANE_BLOCK = LANES // SSM_GROUP
PROJ_TILE = 1024
ROW_ORDERS = tuple(dil for _, dil in ATTN_GROUPS) + (SSM_CHUNK,)
PERM_BASE = 4
NEG = -1e30
LOG2E = 1.4426950408889634
VMEM_LIMIT = 56 * 1024 * 1024


def _params(*sem):
    return pltpu.CompilerParams(dimension_semantics=sem, vmem_limit_bytes=VMEM_LIMIT)


def _sigmoid(x):
    return 0.5 * jnp.tanh(0.5 * x) + 0.5


def _rms(x, gain):
    return x * lax.rsqrt(jnp.mean(x * x, axis=-1, keepdims=True) + EPS) * gain


def _cast_riders(weights, grid):
    ni, nj = grid
    in_specs, out_specs, out_shapes = [], [], []
    for w, ncols in weights:
        k, n = w.shape
        assert k % (ni * 16) == 0 and n % (ncols * LANES) == 0 and ncols <= nj
        spec = pl.BlockSpec((k // ni, n // ncols), functools.partial(
            lambda i, j, last: (i, jnp.minimum(j, last)), last=ncols - 1))
        in_specs.append(spec)
        out_specs.append(spec)
        out_shapes.append(jax.ShapeDtypeStruct(w.shape, BF16))
    return in_specs, out_specs, out_shapes


def _residue_block(r, dil):
    if dil <= PERM_BASE:
        return r
    return (r % PERM_BASE) * (dil // PERM_BASE) + r // PERM_BASE


def _proj_tile(col):
    n_qkv = 3 * N_GROUPS
    return jnp.where(col < n_qkv, (col % N_GROUPS) * 3 + col // N_GROUPS, col)


def _inproj_kernel(x_ref, g_ref, w_ref, *refs, n_qkv, u_tiles, riders):
    ride_in, (o_ref, *ride_out) = refs[:riders], refs[riders:2 * riders + 1]
    h_ref, hn_ref, hb_ref, inv_ref = refs[2 * riders + 1:]
    j = pl.program_id(1)
    slabs, tm, _ = hn_ref.shape

    @pl.when(j == 0)
    def _():
        x = x_ref[...]
        inv_ref[...] = jnp.broadcast_to(lax.rsqrt(jnp.mean(x * x, axis=-1, keepdims=True) + EPS), inv_ref.shape)
        q = tm // PERM_BASE
        piece = 256
        for c in range(x_ref.shape[1] // LANES):
            cols = slice(c * LANES, (c + 1) * LANES)
            s = c % slabs
            for r0 in range(0, tm, piece):
                rows = slice(r0, r0 + piece)
                hn = x_ref[rows, cols] * inv_ref[rows, :] * g_ref[:, cols]
                h_ref[0, rows, cols] = hn.astype(BF16)
                hn_ref[s, rows, :] = hn
            for b in range(PERM_BASE):
                part = hn_ref[s, pl.ds(b, q, stride=PERM_BASE), :]
                hb_ref[s, b * q:(b + 1) * q, :] = part
                for v, dil in enumerate(ROW_ORDERS):
                    if dil == PERM_BASE:
                        h_ref[v, b * q:(b + 1) * q, cols] = part.astype(BF16)
            for v, dil in enumerate(ROW_ORDERS):
                if dil > PERM_BASE:
                    k, n = dil // PERM_BASE, tm // dil
                    for b in range(PERM_BASE):
                        for a in range(k):
                            blk = b * k + a
                            h_ref[v, blk * n:(blk + 1) * n, cols] = (
                                hb_ref[s, pl.ds(b * q + a, n, stride=k), :].astype(BF16))

    order = jnp.where(j < n_qkv, j % N_GROUPS, jnp.where(j < n_qkv + u_tiles, N_GROUPS, 0))
    o_ref[...] = jnp.dot(h_ref[order], w_ref[...], preferred_element_type=F32).astype(o_ref.dtype)
    for src, dst in zip(ride_in, ride_out):
        dst[...] = src[...].astype(dst.dtype)


def _inproj(x, gain, w, ssm_width, ride):
    t, d = x.shape
    tm = PROJ_TILE
    tn = GROUP_WIDTH
    ncol = w.shape[1] // tn
    assert t % tm == 0 and w.shape[1] % tn == 0 and ssm_width % tn == 0 and d % LANES == 0
    assert ROW_ORDERS[0] == 1 and all(tm % (dil * 16) == 0 for dil in ROW_ORDERS)
    assert all(dil in (1, PERM_BASE) or (dil % PERM_BASE == 0 and dil // PERM_BASE <= PERM_BASE) for dil in ROW_ORDERS)
    slabs = 8
    grid = (t // tm, ncol)
    ride_in, ride_out, ride_shapes = _cast_riders(ride, grid)
    return pl.pallas_call(
        functools.partial(_inproj_kernel, n_qkv=3 * N_GROUPS, u_tiles=ssm_width // tn, riders=len(ride)),
        grid=grid,
        in_specs=[
            pl.BlockSpec((tm, d), lambda i, j: (i, 0)),
            pl.BlockSpec((1, d), lambda i, j: (0, 0)),
            pl.BlockSpec((d, tn), lambda i, j: (0, j)),
        ] + ride_in,
        out_specs=[pl.BlockSpec((None, None, tm, tn), lambda i, j: (i, _proj_tile(j), 0, 0))] + ride_out,
        out_shape=[jax.ShapeDtypeStruct((t // tm, ncol, tm, tn), BF16)] + ride_shapes,
        scratch_shapes=[
            pltpu.VMEM((len(ROW_ORDERS), tm, d), BF16),
            pltpu.VMEM((slabs, tm, LANES), F32),
            pltpu.VMEM((slabs, tm, LANES), F32),
            pltpu.VMEM((tm, LANES), F32),
        ],
        compiler_params=_params("parallel", "arbitrary"),
        name="inproj",
    )(x, gain.reshape(1, d), w, *(w for w, _ in ride))


def _attn_bias_table():
    qi = np.arange(ATTN_BLK)[:, None]
    kj = np.arange(ATTN_BLK)[None, :]
    table = np.full((N_GROUPS, 2, HEADS_PER_GROUP * ATTN_BLK, 2 * ATTN_BLK), NEG, np.float32)
    for g, (_, dil) in enumerate(ATTN_GROUPS):
        for h in range(HEADS_PER_GROUP):
            slope = 2.0 ** (-8.0 * (g * HEADS_PER_GROUP + h + 1) / N_HEADS) * dil * LOG2E
            rows = slice(h * ATTN_BLK, (h + 1) * ATTN_BLK)
            cur = np.where(kj <= qi, -slope * (qi - kj), NEG)
            prev = np.where(kj >= qi, -slope * (ATTN_BLK + qi - kj), NEG)
            table[g, :, rows, ATTN_BLK:] = cur
            table[g, 0, rows, :ATTN_BLK] = prev
    return table


def _attn_kernel(bias_ref, *refs):
    qkv_refs = refs[:N_GROUPS]
    o_ref, out_scr, lse_scr = refs[N_GROUPS:N_GROUPS + 3]
    hist_refs = refs[N_GROUPS + 3:]
    tile = pl.program_id(1)
    u = pl.program_id(2)
    units = ATTN_TILE // ATTN_BLK
    nt = (((1,), (1,)), ((), ()))

    @pl.when(jnp.logical_and(tile == 0, u == 0))
    def _():
        for hist in hist_refs:
            hist[...] = jnp.zeros_like(hist)

    blocks, logits, stats = [], [], []
    for g, (_, dil) in enumerate(ATTN_GROUPS):
        nb = u // dil
        r = u % dil
        first = jnp.logical_and(tile == 0, nb == 0).astype(jnp.int32)
        start = nb * (ATTN_BLK * dil) + r
        rows = pl.ds(start, ATTN_BLK) if dil == 1 else pl.ds(start, ATTN_BLK, stride=dil)
        qkv = qkv_refs[g]
        q, k, v = (qkv[..., part, :, :].reshape(ATTN_BLK, GROUP_WIDTH) for part in range(3))
        prev = hist_refs[g][r]
        scores = []
        for h in range(HEADS_PER_GROUP):
            cs = slice(h * HEAD_DIM, (h + 1) * HEAD_DIM)
            keys = jnp.concatenate([prev[:, cs], k[:, cs]], axis=0)
            scores.append(lax.dot_general(q[:, cs], keys, nt, preferred_element_type=F32))
        logits.append(jnp.concatenate(scores, axis=0) * (HEAD_DIM ** -0.5 * LOG2E) + bias_ref[g, first])
        blocks.append((r, rows, k, v, prev))
    for s in logits:
        m = jnp.max(s, axis=1, keepdims=True)
        p = jnp.exp2(s - m)
        l = jnp.sum(p, axis=1, keepdims=True)
        stats.append((p.astype(BF16), 1.0 / l, m + jnp.log2(l)))
    for g, ((r, rows, k, v, prev), (p, inv, lse)) in enumerate(zip(blocks, stats)):
        for h in range(HEADS_PER_GROUP):
            cs = slice(h * HEAD_DIM, (h + 1) * HEAD_DIM)
            hr = slice(h * ATTN_BLK, (h + 1) * ATTN_BLK)
            vals = jnp.concatenate([prev[:, GROUP_WIDTH:][:, cs], v[:, cs]], axis=0)
            o = jnp.dot(p[hr], vals, preferred_element_type=F32)
            out_scr[g, h, rows, :] = o * inv[hr]
            lse_scr[g, h, rows, :] = jnp.broadcast_to(lse[hr], (ATTN_BLK, HEAD_DIM))
        hist_refs[g][r, :, :GROUP_WIDTH] = k
        hist_refs[g][r, :, GROUP_WIDTH:] = v

    @pl.when(u == units - 1)
    def _():
        for h in range(HEADS_PER_GROUP):
            lses = [lse_scr[g, h] for g in range(N_GROUPS)]
            top = functools.reduce(jnp.maximum, lses)
            ws = [jnp.exp2(x - top) for x in lses]
            num = sum(w * out_scr[g, h] for g, w in enumerate(ws))
            o_ref[:, h * HEAD_DIM:(h + 1) * HEAD_DIM] = (num / sum(ws)).astype(o_ref.dtype)


def _attention(proj, bsz, seq):
    assert seq % ATTN_TILE == 0 and ATTN_TILE % PROJ_TILE == 0
    tiles = seq // ATTN_TILE
    units = ATTN_TILE // ATTN_BLK
    bias = jnp.asarray(_attn_bias_table())
    in_specs = [pl.BlockSpec(bias.shape, lambda b, tile, u: (0, 0, 0, 0))]
    hist = []
    for g, (window, dil) in enumerate(ATTN_GROUPS):
        assert window // dil == ATTN_BLK and units % dil == 0
        span = ATTN_BLK * dil
        spans_per_tile = ATTN_TILE // span
        if span <= PROJ_TILE:
            block = (None, 3, ATTN_BLK, GROUP_WIDTH)

            def index(b, tile, u, *, g=g, dil=dil, spt=spans_per_tile, sppt=PROJ_TILE // span,
                      per_seq=seq // PROJ_TILE):
                sp = tile * spt + u // dil
                return b * per_seq + sp // sppt, g, _residue_block(u % dil, dil) * sppt + sp % sppt, 0
        else:
            block = (span // PROJ_TILE, 3, PROJ_TILE // dil, GROUP_WIDTH)

            def index(b, tile, u, *, g=g, dil=dil, spt=spans_per_tile, per_seq=seq // span):
                return b * per_seq + tile * spt + u // dil, g, _residue_block(u % dil, dil), 0

        in_specs.append(pl.BlockSpec(block, index))
        hist.append(pltpu.VMEM((dil, ATTN_BLK, 2 * GROUP_WIDTH), BF16))
    scratch = pltpu.VMEM((N_GROUPS, HEADS_PER_GROUP, ATTN_TILE, HEAD_DIM), F32)
    return pl.pallas_call(
        _attn_kernel,
        grid=(bsz, tiles, units),
        in_specs=in_specs,
        out_specs=pl.BlockSpec((ATTN_TILE, GROUP_WIDTH), lambda b, tile, u: (b * tiles + tile, 0)),
        out_shape=jax.ShapeDtypeStruct((bsz * seq, GROUP_WIDTH), BF16),
        scratch_shapes=[scratch, scratch] + hist,
        compiler_params=_params("arbitrary", "arbitrary", "arbitrary"),
        name="dilated_attention",
    )(bias, *([proj] * N_GROUPS))


def _cmul(a, b):
    return a[0] * b[0] - a[1] * b[1], a[0] * b[1] + a[1] * b[0]


def _ssm_weights(a_re, a_im, log_dt, b_re, b_im, c_re, c_im):
    n_groups = a_re.shape[0]
    nblk = n_groups // GROUPS_PER_LANE_BLOCK
    gl = GROUPS_PER_LANE_BLOCK
    L = SSM_CHUNK
    a_re, a_im = a_re.astype(F32), a_im.astype(F32)
    dt = jnp.exp(log_dt.astype(F32))[:, None]
    steps = jnp.arange(L + 1, dtype=F32)[None, :, None]
    mag = jnp.exp((a_re * dt)[:, None, :] * steps)
    ang = (a_im * dt)[:, None, :] * steps
    powers = (mag * jnp.cos(ang), mag * jnp.sin(ang))
    lam_bar = (powers[0][:, 1], powers[1][:, 1])
    den = a_re * a_re + a_im * a_im
    num = (lam_bar[0] - 1.0, lam_bar[1])
    ratio = ((num[0] * a_re + num[1] * a_im) / den, (num[1] * a_re - num[0] * a_im) / den)
    b_bar = _cmul((ratio[0][..., None], ratio[1][..., None]), (b_re.astype(F32), b_im.astype(F32)))
    c_t = (c_re.astype(F32).transpose(0, 2, 1), c_im.astype(F32).transpose(0, 2, 1))
    eye = jnp.eye(gl, dtype=F32)

    def block_diag(m):
        rows, cols = m.shape[1:]
        m = m.reshape(nblk, gl, rows, 1, cols) * eye[None, :, None, :, None]
        return m.reshape(nblk, gl * rows, gl * cols)

    b_in = jnp.stack([block_diag(b.transpose(0, 2, 1)) for b in b_bar], axis=1)
    c_out = jnp.stack([block_diag(c) for c in c_t], axis=1)
    pw = jnp.stack(powers, axis=0).reshape(2, nblk, gl, L + 1, SSM_STATE)
    pw_row = pw.transpose(1, 0, 3, 2, 4).reshape(nblk, 2, L + 1, gl * SSM_STATE)
    pw_col = pw_row.transpose(0, 1, 3, 2)
    return b_in, c_out, pw_row, pw_col


def _ssm_kernel(*refs, tiles_per_seq):
    u_ref = refs[0]
    (bin_ref, cout_ref, pwr_ref, pwc_ref, d_ref, y_ref,
     wt_scr, wb_scr, wl_scr, wc_scr, s_scr, xp_scr, carry_scr, y_scr) = refs[1:]
    L = SSM_CHUNK
    nb = wt_scr.shape[0]
    tc = s_scr.shape[0]
    half = s_scr.shape[1] // 2
    sw = half // nb

    @pl.when(pl.program_id(1) == 0)
    def _():
        for q in range(nb):
            c_hi = [cout_ref[q, ri].astype(BF16) for ri in range(2)]
            c_lo = [(cout_ref[q, ri] - c_hi[ri].astype(F32)).astype(BF16) for ri in range(2)]
            for j in range(L):
                rows = slice(j * LANES, (j + 1) * LANES)
                n = L - 1 - j
                pr = pwr_ref[q, 0, n:n + 1, :]
                pi = pwr_ref[q, 1, n:n + 1, :]
                for ri, val in enumerate((bin_ref[q, 0] * pr - bin_ref[q, 1] * pi,
                                          bin_ref[q, 0] * pi + bin_ref[q, 1] * pr)):
                    cols = slice(ri * sw, (ri + 1) * sw)
                    top = val.astype(BF16)
                    wb_scr[q, rows, cols] = top
                    wl_scr[rows, cols] = (val - top.astype(F32)).astype(BF16)
                pr = pwc_ref[q, 0, :, j + 1:j + 2]
                pi = pwc_ref[q, 1, :, j + 1:j + 2]
                wc_scr[q, :sw, rows] = (cout_ref[q, 0] * pr - cout_ref[q, 1] * pi).astype(BF16)
                wc_scr[q, sw:, rows] = (-(cout_ref[q, 0] * pi + cout_ref[q, 1] * pr)).astype(BF16)
            prods = []
            for ri in range(2):
                cols = slice(ri * sw, (ri + 1) * sw)
                top, low = wb_scr[q, :, cols], wl_scr[:, cols]
                prods.append(jnp.dot(top, c_hi[ri], preferred_element_type=F32)
                             + jnp.dot(top, c_lo[ri], preferred_element_type=F32)
                             + jnp.dot(low, c_hi[ri], preferred_element_type=F32))
            k_all = (prods[0] - prods[1]).astype(BF16)
            for j in range(L):
                n = L - 1 - j
                kn = k_all[j * LANES:(j + 1) * LANES]
                for jj in range(L):
                    ii = jj + n
                    if ii < L:
                        wt_scr[q, jj * LANES:(jj + 1) * LANES, ii * LANES:(ii + 1) * LANES] = kn
                    if j < jj:
                        wt_scr[q, jj * LANES:(jj + 1) * LANES, j * LANES:(j + 1) * LANES] = jnp.zeros((LANES, LANES), BF16)

    @pl.when(pl.program_id(1) % tiles_per_seq == 0)
    def _():
        carry_scr[...] = jnp.zeros_like(carry_scr)

    crows = u_ref.shape[1] // L
    us = [u_ref[:, _residue_block(j, L) * crows:(_residue_block(j, L) + 1) * crows, :].reshape(tc, nb * LANES)
          for j in range(L)]
    ucat = [jnp.concatenate([u[:, q * LANES:(q + 1) * LANES] for u in us], axis=1)
            for q in range(nb)]
    for q in range(nb):
        s = jnp.dot(ucat[q], wb_scr[q], preferred_element_type=F32)
        s_scr[:, q * sw:(q + 1) * sw] = s[:, :sw]
        s_scr[:, half + q * sw:half + (q + 1) * sw] = s[:, sw:]
    ys = [jnp.dot(ucat[q], wt_scr[q], preferred_element_type=F32) for q in range(nb)]
    for q in range(nb):
        re = slice(q * sw, (q + 1) * sw)
        im = slice(half + q * sw, half + (q + 1) * sw)
        ar = pwr_ref[q, 0, L:L + 1, :]
        ai = pwr_ref[q, 1, L:L + 1, :]

        xr, xi = carry_scr[:, re], carry_scr[:, im]
        for c in range(tc):
            xp_scr[c:c + 1, re] = xr
            xp_scr[c:c + 1, im] = xi
            xr, xi = (ar * xr - ai * xi + s_scr[c:c + 1, re], ar * xi + ai * xr + s_scr[c:c + 1, im])
        carry_scr[:, re] = xr
        carry_scr[:, im] = xi

    for q in range(nb):
        lanes = slice(q * LANES, (q + 1) * LANES)
        xq = jnp.concatenate([xp_scr[:, q * sw:(q + 1) * sw], xp_scr[:, half + q * sw:half + (q + 1) * sw]], axis=1)
        y = ys[q] + jnp.dot(xq.astype(BF16), wc_scr[q], preferred_element_type=F32)
        for i in range(L):
            yi = y[:, i * LANES:(i + 1) * LANES] + d_ref[0, :, lanes] * us[i][:, lanes].astype(F32)
            y_scr[q, pl.ds(i, tc, stride=L), :] = jax.nn.gelu(yi)
        y_ref[:, lanes] = y_scr[q].astype(y_ref.dtype)


def _ssm(proj, u_tile0, ssm_w, d_skip, bsz, seq, *, ptiles, nb):
    nt, _, tm, tn = proj.shape
    L = SSM_CHUNK
    per_tile = tn // (nb * LANES)
    crows = tm // L
    nblk = ssm_w[0].shape[0]
    width = nblk * LANES
    assert crows == LANES and nblk % nb == 0 and tn % (nb * LANES) == 0
    tc = ptiles * crows
    rows_per_seq = seq // L
    assert rows_per_seq % tc == 0 and nt % ptiles == 0
    u_spec = pl.BlockSpec((ptiles, None, tm, nb * LANES), lambda blk, i: (i, u_tile0 + blk // per_tile, 0, blk % per_tile))
    w_specs = [pl.BlockSpec((nb,) + w.shape[1:], lambda blk, i: (blk, 0, 0, 0)) for w in ssm_w]
    wide = L * LANES
    states = 2 * GROUPS_PER_LANE_BLOCK * SSM_STATE
    return pl.pallas_call(
        functools.partial(_ssm_kernel, tiles_per_seq=rows_per_seq // tc),
        grid=(nblk // nb, nt // ptiles),
        in_specs=[u_spec] + w_specs + [pl.BlockSpec((1, 1, nb * LANES), lambda blk, i: (blk, 0, 0))],
        out_specs=pl.BlockSpec((tc * L, nb * LANES), lambda blk, i: (i, blk)),
        out_shape=jax.ShapeDtypeStruct((nt * tm, width), BF16),
        scratch_shapes=[
            pltpu.VMEM((nb, wide, wide), BF16),
            pltpu.VMEM((nb, wide, states), BF16),
            pltpu.VMEM((wide, states), BF16),
            pltpu.VMEM((nb, states, wide), BF16),
            pltpu.VMEM((tc, nb * states), F32),
            pltpu.VMEM((tc, nb * states), F32),
            pltpu.VMEM((1, nb * states), F32),
            pltpu.VMEM((nb, tc * L, LANES), F32),
        ],
        compiler_params=_params("parallel", "arbitrary"),
        name="s5_chunked",
    )(proj, *ssm_w, d_skip.astype(F32).reshape(nblk // nb, 1, nb * LANES))


def _merge_kernel(attn_ref, y_ref, ga_ref, gs_ref, wup_ref, wv_ref, wg_ref, *refs):
    riders = len(refs) // 2
    ride_in, (o_ref, *ride_out) = refs[:riders], refs[riders:]
    y = y_ref[...]
    gate = _sigmoid(jnp.dot(y, wg_ref[...].astype(BF16), preferred_element_type=F32))
    gate_s = _sigmoid(gs_ref[...].astype(F32))
    gate_a = _sigmoid(ga_ref[...].astype(F32))
    val = jnp.dot(y, wv_ref[...].astype(BF16), preferred_element_type=F32)
    attn_branch = jnp.dot(attn_ref[...], wup_ref[...].astype(BF16), preferred_element_type=F32)
    merged = gate_a * attn_branch + gate_s * (val * gate)
    o_ref[...] = merged.astype(o_ref.dtype)
    for src, dst in zip(ride_in, ride_out):
        dst[...] = src[...].astype(dst.dtype)


def _merge(attn, y, proj, gate_tile0, w_up, w_v, w_g, ride):
    t = attn.shape[0]
    n = w_up.shape[1]
    _, _, tm, tn = proj.shape
    assert n % tn == 0
    grid = (t // tm, n // tn)
    ride_in, ride_out, ride_shapes = _cast_riders(ride, grid)
    return pl.pallas_call(
        _merge_kernel,
        grid=grid,
        in_specs=[
            pl.BlockSpec((tm, attn.shape[1]), lambda i, j: (i, 0)),
            pl.BlockSpec((tm, y.shape[1]), lambda i, j: (i, 0)),
            pl.BlockSpec((None, None, tm, tn), lambda i, j: (i, gate_tile0 + j, 0, 0)),
            pl.BlockSpec((None, None, tm, tn), lambda i, j: (i, gate_tile0 + n // tn + j, 0, 0)),
            pl.BlockSpec((w_up.shape[0], tn), lambda i, j: (0, j)),
            pl.BlockSpec((w_v.shape[0], tn), lambda i, j: (0, j)),
            pl.BlockSpec((w_g.shape[0], tn), lambda i, j: (0, j)),
        ] + ride_in,
        out_specs=[pl.BlockSpec((tm, tn), lambda i, j: (i, j))] + ride_out,
        out_shape=[jax.ShapeDtypeStruct((t, n), BF16)] + ride_shapes,
        compiler_params=_params("parallel", "arbitrary"),
        name="gated_merge",
    )(attn, y, proj, proj, w_up, w_v, w_g, *(w for w, _ in ride))


def _outproj_kernel(m_ref, w_ref, x_ref, g_ref, o_ref):
    z = jnp.dot(m_ref[...], w_ref[...], preferred_element_type=F32)
    o_ref[...] = x_ref[...] + _rms(z, g_ref[...])


def _outproj(merged, w, x, gain, *, tm):
    t, d = x.shape
    return pl.pallas_call(
        _outproj_kernel,
        grid=(t // tm,),
        in_specs=[
            pl.BlockSpec((tm, merged.shape[1]), lambda i: (i, 0)),
            pl.BlockSpec(w.shape, lambda i: (0, 0)),
            pl.BlockSpec((tm, d), lambda i: (i, 0)),
            pl.BlockSpec((1, d), lambda i: (0, 0)),
        ],
        out_specs=pl.BlockSpec((tm, d), lambda i: (i, 0)),
        out_shape=jax.ShapeDtypeStruct((t, d), F32),
        compiler_params=_params("parallel"),
        name="outproj_norm_residual",
    )(merged, w, x, gain.reshape(1, d))


def _ffn_kernel(x_ref, gpre_ref, gpost_ref, wg_ref, wu_ref, wd_ref, o_ref, h_ref, acc_ref):
    k = pl.program_id(1)

    @pl.when(k == 0)
    def _():
        h_ref[...] = _rms(x_ref[...], gpre_ref[...]).astype(BF16)
        acc_ref[...] = jnp.zeros_like(acc_ref)

    h = h_ref[...]
    gate = jnp.dot(h, wg_ref[...], preferred_element_type=F32)
    up = jnp.dot(h, wu_ref[...], preferred_element_type=F32)
    f = (jax.nn.silu(gate) * up).astype(BF16)
    acc_ref[...] += jnp.dot(f, wd_ref[...], preferred_element_type=F32)

    @pl.when(k == pl.num_programs(1) - 1)
    def _():
        o_ref[...] = x_ref[...] + _rms(acc_ref[...], gpost_ref[...])


def _ffn(x, gain_pre, gain_post, w_gate, w_up, w_down, *, tm, tf):
    t, d = x.shape
    dff = w_gate.shape[1]
    return pl.pallas_call(
        _ffn_kernel,
        grid=(t // tm, dff // tf),
        in_specs=[
            pl.BlockSpec((tm, d), lambda i, k: (i, 0)),
            pl.BlockSpec((1, d), lambda i, k: (0, 0)),
            pl.BlockSpec((1, d), lambda i, k: (0, 0)),
            pl.BlockSpec((d, tf), lambda i, k: (0, k)),
            pl.BlockSpec((d, tf), lambda i, k: (0, k)),
            pl.BlockSpec((tf, d), lambda i, k: (k, 0)),
        ],
        out_specs=pl.BlockSpec((tm, d), lambda i, k: (i, 0)),
        out_shape=jax.ShapeDtypeStruct((t, d), F32),
        scratch_shapes=[pltpu.VMEM((tm, d), BF16), pltpu.VMEM((tm, d), F32)],
        compiler_params=_params("parallel", "arbitrary"),
        name="swiglu_ffn",
    )(x, gain_pre.reshape(1, d), gain_post.reshape(1, d), w_gate, w_up, w_down)


def _layer(x, norm_mix_pre, w_in, w_attn_up, ssm_a_re, ssm_a_im, ssm_log_dt, ssm_b_re, ssm_b_im,
           ssm_c_re, ssm_c_im, ssm_d, w_glu_v, w_glu_g, w_out, norm_mix_post, norm_ffn_pre,
           w_ffn_gate, w_ffn_up, w_ffn_down, norm_ffn_post):
    bsz, seq, d = x.shape
    t = bsz * seq
    ssm_width = ssm_d.shape[0]
    u_col0 = 3 * N_HEADS * HEAD_DIM
    assert w_in.shape[1] == u_col0 + ssm_width + 2 * d

    x2 = x.reshape(t, d)
    tn = GROUP_WIDTH
    ffn_tiles = w_ffn_gate.shape[1] // 512
    proj, w_gate, w_up = _inproj(x2, norm_mix_pre, w_in.astype(BF16), ssm_width,
                                 [(w_ffn_gate, ffn_tiles), (w_ffn_up, ffn_tiles)])
    attn = _attention(proj, bsz, seq)
    ssm_w = _ssm_weights(ssm_a_re, ssm_a_im, ssm_log_dt, ssm_b_re, ssm_b_im, ssm_c_re, ssm_c_im)
    y = _ssm(proj, u_col0 // tn, ssm_w, ssm_d, bsz, seq, ptiles=4, nb=2)
    merged, w_down, w_outp = _merge(attn, y, proj, (u_col0 + ssm_width) // tn, w_attn_up, w_glu_v, w_glu_g,
                                    [(w_ffn_down, d // tn), (w_out, d // tn)])
    x1 = _outproj(merged, w_outp, x2, norm_mix_post, tm=512)
    out = _ffn(x1, norm_ffn_pre, norm_ffn_post, w_gate, w_up, w_down, tm=512, tf=512)
    return out.reshape(bsz, seq, d)


def kernel(x, norm_mix_pre, w_in, w_attn_up, ssm_a_re, ssm_a_im, ssm_log_dt, ssm_b_re, ssm_b_im, ssm_c_re, ssm_c_im, ssm_d, w_glu_v, w_glu_g, w_out, norm_mix_post, norm_ffn_pre, w_ffn_gate, w_ffn_up, w_ffn_down, norm_ffn_post):
    stacked = (norm_mix_pre, w_in, w_attn_up, ssm_a_re, ssm_a_im, ssm_log_dt, ssm_b_re, ssm_b_im, ssm_c_re,
               ssm_c_im, ssm_d, w_glu_v, w_glu_g, w_out, norm_mix_post, norm_ffn_pre, w_ffn_gate, w_ffn_up,
               w_ffn_down, norm_ffn_post)
    for layer in range(norm_mix_pre.shape[0]):
        x = _layer(x, *(p[layer] for p in stacked))
    return x
```

```python
import functools

import jax
import jax.numpy as jnp
import numpy as np
from jax import lax
from jax.experimental import pallas as pl
from jax.experimental.pallas import tpu as pltpu

F32 = jnp.float32
BF16 = jnp.bfloat16

EPS = 1e-6
HEAD_DIM = 128
HEADS_PER_GROUP = 4
ATTN_GROUPS = ((128, 1), (512, 4), (2048, 16))
N_GROUPS = len(ATTN_GROUPS)
N_HEADS = HEADS_PER_GROUP * N_GROUPS
GROUP_WIDTH = HEADS_PER_GROUP * HEAD_DIM
ATTN_BLK = 128
ATTN_TILE = 2048
SSM_GROUP = 16
SSM_STATE = 64
SSM_CHUNK = 8
LANES = 128
GROUPS_PER_LANE_BLOCK = LANES // SSM_GROUP
PROJ_TILE = 1024
ROW_ORDERS = tuple(dil for _, dil in ATTN_GROUPS) + (SSM_CHUNK,)
PERM_BASE = 4
NEG = -1e30
LOG2E = 1.4426950408889634
VMEM_LIMIT = 56 * 1024 * 1024


def _params(*sem):
    return pltpu.CompilerParams(dimension_semantics=sem, vmem_limit_bytes=VMEM_LIMIT)


def _sigmoid(x):
    return 0.5 * jnp.tanh(0.5 * x) + 0.5


def _rms(x, gain):
    return x * lax.rsqrt(jnp.mean(x * x, axis=-1, keepdims=True) + EPS) * gain


def _cast_riders(weights, grid):
    ni, nj = grid
    in_specs, out_specs, out_shapes = [], [], []
    for w, ncols in weights:
        k, n = w.shape
        assert k % (ni * 16) == 0 and n % (ncols * LANES) == 0 and ncols <= nj
        spec = pl.BlockSpec((k // ni, n // ncols), functools.partial(
            lambda i, j, last: (i, jnp.minimum(j, last)), last=ncols - 1))
        in_specs.append(spec)
        out_specs.append(spec)
        out_shapes.append(jax.ShapeDtypeStruct(w.shape, BF16))
    return in_specs, out_specs, out_shapes


def _residue_block(r, dil):
    if dil <= PERM_BASE:
        return r
    return (r % PERM_BASE) * (dil // PERM_BASE) + r // PERM_BASE


def _proj_tile(col):
    n_qkv = 3 * N_GROUPS
    return jnp.where(col < n_qkv, (col % N_GROUPS) * 3 + col // N_GROUPS, col)


def _inproj_kernel(x_ref, g_ref, w_ref, *refs, n_qkv, u_tiles, riders):
    ride_in, (o_ref, *ride_out) = refs[:riders], refs[riders:2 * riders + 1]
    h_ref, hn_ref, hb_ref, inv_ref = refs[2 * riders + 1:]
    j = pl.program_id(1)
    slabs, tm, _ = hn_ref.shape

    @pl.when(j == 0)
    def _():
        x = x_ref[...]
        inv_ref[...] = jnp.broadcast_to(lax.rsqrt(jnp.mean(x * x, axis=-1, keepdims=True) + EPS), inv_ref.shape)
        q = tm // PERM_BASE
        piece = 256
        for c in range(x_ref.shape[1] // LANES):
            cols = slice(c * LANES, (c + 1) * LANES)
            s = c % slabs
            for r0 in range(0, tm, piece):
                rows = slice(r0, r0 + piece)
                hn = x_ref[rows, cols] * inv_ref[rows, :] * g_ref[:, cols]
                h_ref[0, rows, cols] = hn.astype(BF16)
                hn_ref[s, rows, :] = hn
            for b in range(PERM_BASE):
                part = hn_ref[s, pl.ds(b, q, stride=PERM_BASE), :]
                hb_ref[s, b * q:(b + 1) * q, :] = part
                for v, dil in enumerate(ROW_ORDERS):
                    if dil == PERM_BASE:
                        h_ref[v, b * q:(b + 1) * q, cols] = part.astype(BF16)
            for v, dil in enumerate(ROW_ORDERS):
                if dil > PERM_BASE:
                    k, n = dil // PERM_BASE, tm // dil
                    for b in range(PERM_BASE):
                        for a in range(k):
                            blk = b * k + a
                            h_ref[v, blk * n:(blk + 1) * n, cols] = (
                                hb_ref[s, pl.ds(b * q + a, n, stride=k), :].astype(BF16))

    order = jnp.where(j < n_qkv, j % N_GROUPS, jnp.where(j < n_qkv + u_tiles, N_GROUPS, 0))
    o_ref[...] = jnp.dot(h_ref[order], w_ref[...], preferred_element_type=F32).astype(o_ref.dtype)
    for src, dst in zip(ride_in, ride_out):
        dst[...] = src[...].astype(dst.dtype)


def _inproj(x, gain, w, ssm_width, ride):
    t, d = x.shape
    tm = PROJ_TILE
    tn = GROUP_WIDTH
    ncol = w.shape[1] // tn
    assert t % tm == 0 and w.shape[1] % tn == 0 and ssm_width % tn == 0 and d % LANES == 0
    assert ROW_ORDERS[0] == 1 and all(tm % (dil * 16) == 0 for dil in ROW_ORDERS)
    assert all(dil in (1, PERM_BASE) or (dil % PERM_BASE == 0 and dil // PERM_BASE <= PERM_BASE) for dil in ROW_ORDERS)
    slabs = 8
    grid = (t // tm, ncol)
    ride_in, ride_out, ride_shapes = _cast_riders(ride, grid)
    return pl.pallas_call(
        functools.partial(_inproj_kernel, n_qkv=3 * N_GROUPS, u_tiles=ssm_width // tn, riders=len(ride)),
        grid=grid,
        in_specs=[
            pl.BlockSpec((tm, d), lambda i, j: (i, 0)),
            pl.BlockSpec((1, d), lambda i, j: (0, 0)),
            pl.BlockSpec((d, tn), lambda i, j: (0, j)),
        ] + ride_in,
        out_specs=[pl.BlockSpec((None, None, tm, tn), lambda i, j: (i, _proj_tile(j), 0, 0))] + ride_out,
        out_shape=[jax.ShapeDtypeStruct((t // tm, ncol, tm, tn), BF16)] + ride_shapes,
        scratch_shapes=[
            pltpu.VMEM((len(ROW_ORDERS), tm, d), BF16),
            pltpu.VMEM((slabs, tm, LANES), F32),
            pltpu.VMEM((slabs, tm, LANES), F32),
            pltpu.VMEM((tm, LANES), F32),
        ],
        compiler_params=_params("parallel", "arbitrary"),
        name="inproj",
    )(x, gain.reshape(1, d), w, *(w for w, _ in ride))


def _attn_bias_table():
    qi = np.arange(ATTN_BLK)[:, None]
    kj = np.arange(ATTN_BLK)[None, :]
    table = np.full((N_GROUPS, 2, HEADS_PER_GROUP * ATTN_BLK, 2 * ATTN_BLK), NEG, np.float32)
    for g, (_, dil) in enumerate(ATTN_GROUPS):
        for h in range(HEADS_PER_GROUP):
            slope = 2.0 ** (-8.0 * (g * HEADS_PER_GROUP + h + 1) / N_HEADS) * dil * LOG2E
            rows = slice(h * ATTN_BLK, (h + 1) * ATTN_BLK)
            cur = np.where(kj <= qi, -slope * (qi - kj), NEG)
            prev = np.where(kj >= qi, -slope * (ATTN_BLK + qi - kj), NEG)
            table[g, :, rows, ATTN_BLK:] = cur
            table[g, 0, rows, :ATTN_BLK] = prev
    return table


def _attn_kernel(bias_ref, *refs):
    qkv_refs = refs[:N_GROUPS]
    o_ref, out_scr, lse_scr = refs[N_GROUPS:N_GROUPS + 3]
    hist_refs = refs[N_GROUPS + 3:]
    tile = pl.program_id(1)
    u = pl.program_id(2)
    units = ATTN_TILE // ATTN_BLK
    nt = (((1,), (1,)), ((), ()))

    @pl.when(jnp.logical_and(tile == 0, u == 0))
    def _():
        for hist in hist_refs:
            hist[...] = jnp.zeros_like(hist)

    blocks, logits, stats = [], [], []
    for g, (_, dil) in enumerate(ATTN_GROUPS):
        nb = u // dil
        r = u % dil
        first = jnp.logical_and(tile == 0, nb == 0).astype(jnp.int32)
        start = nb * (ATTN_BLK * dil) + r
        rows = pl.ds(start, ATTN_BLK) if dil == 1 else pl.ds(start, ATTN_BLK, stride=dil)
        qkv = qkv_refs[g]
        q, k, v = (qkv[..., part, :, :].reshape(ATTN_BLK, GROUP_WIDTH) for part in range(3))
        prev = hist_refs[g][r]
        scores = []
        for h in range(HEADS_PER_GROUP):
            cs = slice(h * HEAD_DIM, (h + 1) * HEAD_DIM)
            keys = jnp.concatenate([prev[:, cs], k[:, cs]], axis=0)
            scores.append(lax.dot_general(q[:, cs], keys, nt, preferred_element_type=F32))
        logits.append(jnp.concatenate(scores, axis=0) * (HEAD_DIM ** -0.5 * LOG2E) + bias_ref[g, first])
        blocks.append((r, rows, k, v, prev))
    for s in logits:
        m = jnp.max(s, axis=1, keepdims=True)
        p = jnp.exp2(s - m)
        l = jnp.sum(p, axis=1, keepdims=True)
        stats.append((p.astype(BF16), 1.0 / l, m + jnp.log2(l)))
    for g, ((r, rows, k, v, prev), (p, inv, lse)) in enumerate(zip(blocks, stats)):
        for h in range(HEADS_PER_GROUP):
            cs = slice(h * HEAD_DIM, (h + 1) * HEAD_DIM)
            hr = slice(h * ATTN_BLK, (h + 1) * ATTN_BLK)
            vals = jnp.concatenate([prev[:, GROUP_WIDTH:][:, cs], v[:, cs]], axis=0)
            o = jnp.dot(p[hr], vals, preferred_element_type=F32)
            out_scr[g, h, rows, :] = o * inv[hr]
            lse_scr[g, h, rows, :] = jnp.broadcast_to(lse[hr], (ATTN_BLK, HEAD_DIM))
        hist_refs[g][r, :, :GROUP_WIDTH] = k
        hist_refs[g][r, :, GROUP_WIDTH:] = v

    @pl.when(u == units - 1)
    def _():
        for h in range(HEADS_PER_GROUP):
            lses = [lse_scr[g, h] for g in range(N_GROUPS)]
            top = functools.reduce(jnp.maximum, lses)
            ws = [jnp.exp2(x - top) for x in lses]
            num = sum(w * out_scr[g, h] for g, w in enumerate(ws))
            o_ref[:, h * HEAD_DIM:(h + 1) * HEAD_DIM] = (num / sum(ws)).astype(o_ref.dtype)


def _attention(proj, bsz, seq):
    assert seq % ATTN_TILE == 0 and ATTN_TILE % PROJ_TILE == 0
    tiles = seq // ATTN_TILE
    units = ATTN_TILE // ATTN_BLK
    bias = jnp.asarray(_attn_bias_table())
    in_specs = [pl.BlockSpec(bias.shape, lambda b, tile, u: (0, 0, 0, 0))]
    hist = []
    for g, (window, dil) in enumerate(ATTN_GROUPS):
        assert window // dil == ATTN_BLK and units % dil == 0
        span = ATTN_BLK * dil
        spans_per_tile = ATTN_TILE // span
        if span <= PROJ_TILE:
            block = (None, 3, ATTN_BLK, GROUP_WIDTH)

            def index(b, tile, u, *, g=g, dil=dil, spt=spans_per_tile, sppt=PROJ_TILE // span,
                      per_seq=seq // PROJ_TILE):
                sp = tile * spt + u // dil
                return b * per_seq + sp // sppt, g, _residue_block(u % dil, dil) * sppt + sp % sppt, 0
        else:
            block = (span // PROJ_TILE, 3, PROJ_TILE // dil, GROUP_WIDTH)

            def index(b, tile, u, *, g=g, dil=dil, spt=spans_per_tile, per_seq=seq // span):
                return b * per_seq + tile * spt + u // dil, g, _residue_block(u % dil, dil), 0

        in_specs.append(pl.BlockSpec(block, index))
        hist.append(pltpu.VMEM((dil, ATTN_BLK, 2 * GROUP_WIDTH), BF16))
    scratch = pltpu.VMEM((N_GROUPS, HEADS_PER_GROUP, ATTN_TILE, HEAD_DIM), F32)
    return pl.pallas_call(
        _attn_kernel,
        grid=(bsz, tiles, units),
        in_specs=in_specs,
        out_specs=pl.BlockSpec((ATTN_TILE, GROUP_WIDTH), lambda b, tile, u: (b * tiles + tile, 0)),
        out_shape=jax.ShapeDtypeStruct((bsz * seq, GROUP_WIDTH), BF16),
        scratch_shapes=[scratch, scratch] + hist,
        compiler_params=_params("arbitrary", "arbitrary", "arbitrary"),
        name="dilated_attention",
    )(bias, *([proj] * N_GROUPS))


def _cmul(a, b):
    return a[0] * b[0] - a[1] * b[1], a[0] * b[1] + a[1] * b[0]


def _ssm_weights(a_re, a_im, log_dt, b_re, b_im, c_re, c_im):
    n_groups = a_re.shape[0]
    nblk = n_groups // GROUPS_PER_LANE_BLOCK
    gl = GROUPS_PER_LANE_BLOCK
    L = SSM_CHUNK
    a_re, a_im = a_re.astype(F32), a_im.astype(F32)
    dt = jnp.exp(log_dt.astype(F32))[:, None]
    steps = jnp.arange(L + 1, dtype=F32)[None, :, None]
    mag = jnp.exp((a_re * dt)[:, None, :] * steps)
    ang = (a_im * dt)[:, None, :] * steps
    powers = (mag * jnp.cos(ang), mag * jnp.sin(ang))
    lam_bar = (powers[0][:, 1], powers[1][:, 1])
    den = a_re * a_re + a_im * a_im
    num = (lam_bar[0] - 1.0, lam_bar[1])
    ratio = ((num[0] * a_re + num[1] * a_im) / den, (num[1] * a_re - num[0] * a_im) / den)
    b_bar = _cmul((ratio[0][..., None], ratio[1][..., None]), (b_re.astype(F32), b_im.astype(F32)))
    c_t = (c_re.astype(F32).transpose(0, 2, 1), c_im.astype(F32).transpose(0, 2, 1))
    eye = jnp.eye(gl, dtype=F32)

    def block_diag(m):
        rows, cols = m.shape[1:]
        m = m.reshape(nblk, gl, rows, 1, cols) * eye[None, :, None, :, None]
        return m.reshape(nblk, gl * rows, gl * cols)

    b_in = jnp.stack([block_diag(b.transpose(0, 2, 1)) for b in b_bar], axis=1)
    c_out = jnp.stack([block_diag(c) for c in c_t], axis=1)
    pw = jnp.stack(powers, axis=0).reshape(2, nblk, gl, L + 1, SSM_STATE)
    pw_row = pw.transpose(1, 0, 3, 2, 4).reshape(nblk, 2, L + 1, gl * SSM_STATE)
    pw_col = pw_row.transpose(0, 1, 3, 2)
    return b_in, c_out, pw_row, pw_col


def _ssm_kernel(*refs, tiles_per_seq):
    u_ref = refs[0]
    (bin_ref, cout_ref, pwr_ref, pwc_ref, d_ref, y_ref,
     wt_scr, wb_scr, wl_scr, wc_scr, s_scr, xp_scr, carry_scr, y_scr) = refs[1:]
    L = SSM_CHUNK
    nb = wt_scr.shape[0]
    tc = s_scr.shape[0]
    half = s_scr.shape[1] // 2
    sw = half // nb

    @pl.when(pl.program_id(1) == 0)
    def _():
        for q in range(nb):
            c_hi = [cout_ref[q, ri].astype(BF16) for ri in range(2)]
            c_lo = [(cout_ref[q, ri] - c_hi[ri].astype(F32)).astype(BF16) for ri in range(2)]
            for j in range(L):
                rows = slice(j * LANES, (j + 1) * LANES)
                n = L - 1 - j
                pr = pwr_ref[q, 0, n:n + 1, :]
                pi = pwr_ref[q, 1, n:n + 1, :]
                for ri, val in enumerate((bin_ref[q, 0] * pr - bin_ref[q, 1] * pi,
                                          bin_ref[q, 0] * pi + bin_ref[q, 1] * pr)):
                    cols = slice(ri * sw, (ri + 1) * sw)
                    top = val.astype(BF16)
                    wb_scr[q, rows, cols] = top
                    wl_scr[rows, cols] = (val - top.astype(F32)).astype(BF16)
                pr = pwc_ref[q, 0, :, j + 1:j + 2]
                pi = pwc_ref[q, 1, :, j + 1:j + 2]
                wc_scr[q, :sw, rows] = (cout_ref[q, 0] * pr - cout_ref[q, 1] * pi).astype(BF16)
                wc_scr[q, sw:, rows] = (-(cout_ref[q, 0] * pi + cout_ref[q, 1] * pr)).astype(BF16)
            prods = []
            for ri in range(2):
                cols = slice(ri * sw, (ri + 1) * sw)
                top, low = wb_scr[q, :, cols], wl_scr[:, cols]
                prods.append(jnp.dot(top, c_hi[ri], preferred_element_type=F32)
                             + jnp.dot(top, c_lo[ri], preferred_element_type=F32)
                             + jnp.dot(low, c_hi[ri], preferred_element_type=F32))
            k_all = (prods[0] - prods[1]).astype(BF16)
            for j in range(L):
                n = L - 1 - j
                kn = k_all[j * LANES:(j + 1) * LANES]
                for jj in range(L):
                    ii = jj + n
                    if ii < L:
                        wt_scr[q, jj * LANES:(jj + 1) * LANES, ii * LANES:(ii + 1) * LANES] = kn
                    if j < jj:
                        wt_scr[q, jj * LANES:(jj + 1) * LANES, j * LANES:(j + 1) * LANES] = jnp.zeros((LANES, LANES), BF16)

    @pl.when(pl.program_id(1) % tiles_per_seq == 0)
    def _():
        carry_scr[...] = jnp.zeros_like(carry_scr)

    crows = u_ref.shape[1] // L
    us = [u_ref[:, _residue_block(j, L) * crows:(_residue_block(j, L) + 1) * crows, :].reshape(tc, nb * LANES)
          for j in range(L)]
    ucat = [jnp.concatenate([u[:, q * LANES:(q + 1) * LANES] for u in us], axis=1)
            for q in range(nb)]
    for q in range(nb):
        s = jnp.dot(ucat[q], wb_scr[q], preferred_element_type=F32)
        s_scr[:, q * sw:(q + 1) * sw] = s[:, :sw]
        s_scr[:, half + q * sw:half + (q + 1) * sw] = s[:, sw:]
    ys = [jnp.dot(ucat[q], wt_scr[q], preferred_element_type=F32) for q in range(nb)]
    for q in range(nb):
        re = slice(q * sw, (q + 1) * sw)
        im = slice(half + q * sw, half + (q + 1) * sw)
        ar = pwr_ref[q, 0, L:L + 1, :]
        ai = pwr_ref[q, 1, L:L + 1, :]

        xr, xi = carry_scr[:, re], carry_scr[:, im]
        for c in range(tc):
            xp_scr[c:c + 1, re] = xr
            xp_scr[c:c + 1, im] = xi
            xr, xi = (ar * xr - ai * xi + s_scr[c:c + 1, re], ar * xi + ai * xr + s_scr[c:c + 1, im])
        carry_scr[:, re] = xr
        carry_scr[:, im] = xi

    for q in range(nb):
        lanes = slice(q * LANES, (q + 1) * LANES)
        xq = jnp.concatenate([xp_scr[:, q * sw:(q + 1) * sw], xp_scr[:, half + q * sw:half + (q + 1) * sw]], axis=1)
        y = ys[q] + jnp.dot(xq.astype(BF16), wc_scr[q], preferred_element_type=F32)
        for i in range(L):
            yi = y[:, i * LANES:(i + 1) * LANES] + d_ref[0, :, lanes] * us[i][:, lanes].astype(F32)
            y_scr[q, pl.ds(i, tc, stride=L), :] = jax.nn.gelu(yi)
        y_ref[:, lanes] = y_scr[q].astype(y_ref.dtype)


def _ssm(proj, u_tile0, ssm_w, d_skip, bsz, seq, *, ptiles, nb):
    nt, _, tm, tn = proj.shape
    L = SSM_CHUNK
    per_tile = tn // (nb * LANES)
    crows = tm // L
    nblk = ssm_w[0].shape[0]
    width = nblk * LANES
    assert crows == LANES and nblk % nb == 0 and tn % (nb * LANES) == 0
    tc = ptiles * crows
    rows_per_seq = seq // L
    assert rows_per_seq % tc == 0 and nt % ptiles == 0
    u_spec = pl.BlockSpec((ptiles, None, tm, nb * LANES), lambda blk, i: (i, u_tile0 + blk // per_tile, 0, blk % per_tile))
    w_specs = [pl.BlockSpec((nb,) + w.shape[1:], lambda blk, i: (blk, 0, 0, 0)) for w in ssm_w]
    wide = L * LANES
    states = 2 * GROUPS_PER_LANE_BLOCK * SSM_STATE
    return pl.pallas_call(
        functools.partial(_ssm_kernel, tiles_per_seq=rows_per_seq // tc),
        grid=(nblk // nb, nt // ptiles),
        in_specs=[u_spec] + w_specs + [pl.BlockSpec((1, 1, nb * LANES), lambda blk, i: (blk, 0, 0))],
        out_specs=pl.BlockSpec((tc * L, nb * LANES), lambda blk, i: (i, blk)),
        out_shape=jax.ShapeDtypeStruct((nt * tm, width), BF16),
        scratch_shapes=[
            pltpu.VMEM((nb, wide, wide), BF16),
            pltpu.VMEM((nb, wide, states), BF16),
            pltpu.VMEM((wide, states), BF16),
            pltpu.VMEM((nb, states, wide), BF16),
            pltpu.VMEM((tc, nb * states), F32),
            pltpu.VMEM((tc, nb * states), F32),
            pltpu.VMEM((1, nb * states), F32),
            pltpu.VMEM((nb, tc * L, LANES), F32),
        ],
        compiler_params=_params("parallel", "arbitrary"),
        name="s5_chunked",
    )(proj, *ssm_w, d_skip.astype(F32).reshape(nblk // nb, 1, nb * LANES))


def _merge_kernel(attn_ref, y_ref, ga_ref, gs_ref, wup_ref, wv_ref, wg_ref, *refs):
    riders = len(refs) // 2
    ride_in, (o_ref, *ride_out) = refs[:riders], refs[riders:]
    y = y_ref[...]
    gate = _sigmoid(jnp.dot(y, wg_ref[...].astype(BF16), preferred_element_type=F32))
    gate_s = _sigmoid(gs_ref[...].astype(F32))
    gate_a = _sigmoid(ga_ref[...].astype(F32))
    val = jnp.dot(y, wv_ref[...].astype(BF16), preferred_element_type=F32)
    attn_branch = jnp.dot(attn_ref[...], wup_ref[...].astype(BF16), preferred_element_type=F32)
    merged = gate_a * attn_branch + gate_s * (val * gate)
    o_ref[...] = merged.astype(o_ref.dtype)
    for src, dst in zip(ride_in, ride_out):
        dst[...] = src[...].astype(dst.dtype)


def _merge(attn, y, proj, gate_tile0, w_up, w_v, w_g, ride):
    t = attn.shape[0]
    n = w_up.shape[1]
    _, _, tm, tn = proj.shape
    assert n % tn == 0
    grid = (t // tm, n // tn)
    ride_in, ride_out, ride_shapes = _cast_riders(ride, grid)
    return pl.pallas_call(
        _merge_kernel,
        grid=grid,
        in_specs=[
            pl.BlockSpec((tm, attn.shape[1]), lambda i, j: (i, 0)),
            pl.BlockSpec((tm, y.shape[1]), lambda i, j: (i, 0)),
            pl.BlockSpec((None, None, tm, tn), lambda i, j: (i, gate_tile0 + j, 0, 0)),
            pl.BlockSpec((None, None, tm, tn), lambda i, j: (i, gate_tile0 + n // tn + j, 0, 0)),
            pl.BlockSpec((w_up.shape[0], tn), lambda i, j: (0, j)),
            pl.BlockSpec((w_v.shape[0], tn), lambda i, j: (0, j)),
            pl.BlockSpec((w_g.shape[0], tn), lambda i, j: (0, j)),
        ] + ride_in,
        out_specs=[pl.BlockSpec((tm, tn), lambda i, j: (i, j))] + ride_out,
        out_shape=[jax.ShapeDtypeStruct((t, n), BF16)] + ride_shapes,
        compiler_params=_params("parallel", "arbitrary"),
        name="gated_merge",
    )(attn, y, proj, proj, w_up, w_v, w_g, *(w for w, _ in ride))


def _outproj_kernel(m_ref, w_ref, x_ref, g_ref, o_ref):
    z = jnp.dot(m_ref[...], w_ref[...], preferred_element_type=F32)
    o_ref[...] = x_ref[...] + _rms(z, g_ref[...])


def _outproj(merged, w, x, gain, *, tm):
    t, d = x.shape
    return pl.pallas_call(
        _outproj_kernel,
        grid=(t // tm,),
        in_specs=[
            pl.BlockSpec((tm, merged.shape[1]), lambda i: (i, 0)),
            pl.BlockSpec(w.shape, lambda i: (0, 0)),
            pl.BlockSpec((tm, d), lambda i: (i, 0)),
            pl.BlockSpec((1, d), lambda i: (0, 0)),
        ],
        out_specs=pl.BlockSpec((tm, d), lambda i: (i, 0)),
        out_shape=jax.ShapeDtypeStruct((t, d), F32),
        compiler_params=_params("parallel"),
        name="outproj_norm_residual",
    )(merged, w, x, gain.reshape(1, d))


def _ffn_kernel(x_ref, gpre_ref, gpost_ref, wg_ref, wu_ref, wd_ref, o_ref, h_ref, acc_ref):
    k = pl.program_id(1)
    last = pl.num_programs(1) - 1

    def step(first, final):
        if first:
            h_ref[...] = _rms(x_ref[...], gpre_ref[...]).astype(BF16)
        h = h_ref[...]
        gate = jnp.dot(h, wg_ref[...], preferred_element_type=F32)
        up = jnp.dot(h, wu_ref[...], preferred_element_type=F32)
        f = (jax.nn.silu(gate) * up).astype(BF16)
        down = jnp.dot(f, wd_ref[...], preferred_element_type=F32)
        if first:
            acc_ref[...] = down
        elif final:
            o_ref[...] = x_ref[...] + _rms(acc_ref[...] + down, gpost_ref[...])
        else:
            acc_ref[...] += down

    pl.when(k == 0)(functools.partial(step, True, False))
    pl.when(jnp.logical_and(k > 0, k < last))(functools.partial(step, False, False))
    pl.when(k == last)(functools.partial(step, False, True))


def _ffn(x, gain_pre, gain_post, w_gate, w_up, w_down, *, tm, tf):
    t, d = x.shape
    dff = w_gate.shape[1]
    return pl.pallas_call(
        _ffn_kernel,
        grid=(t // tm, dff // tf),
        in_specs=[
            pl.BlockSpec((tm, d), lambda i, k: (i, 0)),
            pl.BlockSpec((1, d), lambda i, k: (0, 0)),
            pl.BlockSpec((1, d), lambda i, k: (0, 0)),
            pl.BlockSpec((d, tf), lambda i, k: (0, k)),
            pl.BlockSpec((d, tf), lambda i, k: (0, k)),
            pl.BlockSpec((tf, d), lambda i, k: (k, 0)),
        ],
        out_specs=pl.BlockSpec((tm, d), lambda i, k: (i, 0)),
        out_shape=jax.ShapeDtypeStruct((t, d), F32),
        scratch_shapes=[pltpu.VMEM((tm, d), BF16), pltpu.VMEM((tm, d), F32)],
        compiler_params=_params("parallel", "arbitrary"),
        name="swiglu_ffn",
    )(x, gain_pre.reshape(1, d), gain_post.reshape(1, d), w_gate, w_up, w_down)


def _layer(x, norm_mix_pre, w_in, w_attn_up, ssm_a_re, ssm_a_im, ssm_log_dt, ssm_b_re, ssm_b_im,
           ssm_c_re, ssm_c_im, ssm_d, w_glu_v, w_glu_g, w_out, norm_mix_post, norm_ffn_pre,
           w_ffn_gate, w_ffn_up, w_ffn_down, norm_ffn_post):
    bsz, seq, d = x.shape
    t = bsz * seq
    ssm_width = ssm_d.shape[0]
    u_col0 = 3 * N_HEADS * HEAD_DIM
    assert w_in.shape[1] == u_col0 + ssm_width + 2 * d

    x2 = x.reshape(t, d)
    tn = GROUP_WIDTH
    ffn_tiles = w_ffn_gate.shape[1] // 512
    proj, w_gate, w_up = _inproj(x2, norm_mix_pre, w_in.astype(BF16), ssm_width,
                                 [(w_ffn_gate, ffn_tiles), (w_ffn_up, ffn_tiles)])
    attn = _attention(proj, bsz, seq)
    ssm_w = _ssm_weights(ssm_a_re, ssm_a_im, ssm_log_dt, ssm_b_re, ssm_b_im, ssm_c_re, ssm_c_im)
    y = _ssm(proj, u_col0 // tn, ssm_w, ssm_d, bsz, seq, ptiles=4, nb=2)
    merged, w_down, w_outp = _merge(attn, y, proj, (u_col0 + ssm_width) // tn, w_attn_up, w_glu_v, w_glu_g,
                                    [(w_ffn_down, d // tn), (w_out, d // tn)])
    x1 = _outproj(merged, w_outp, x2, norm_mix_post, tm=512)
    out = _ffn(x1, norm_ffn_pre, norm_ffn_post, w_gate, w_up, w_down, tm=512, tf=512)
    return out.reshape(bsz, seq, d)


def kernel(x, norm_mix_pre, w_in, w_attn_up, ssm_a_re, ssm_a_im, ssm_log_dt, ssm_b_re, ssm_b_im, ssm_c_re, ssm_c_im, ssm_d, w_glu_v, w_glu_g, w_out, norm_mix_post, norm_ffn_pre, w_ffn_gate, w_ffn_up, w_ffn_down, norm_ffn_post):
    stacked = (norm_mix_pre, w_in, w_attn_up, ssm_a_re, ssm_a_im, ssm_log_dt, ssm_b_re, ssm_b_im, ssm_c_re,
               ssm_c_im, ssm_d, w_glu_v, w_glu_g, w_out, norm_mix_post, norm_ffn_pre, w_ffn_gate, w_ffn_up,
               w_ffn_down, norm_ffn_post)
    for layer in range(norm_mix_pre.shape[0]):
        x = _layer(x, *(p[layer] for p in stacked))
    return x
```

```python
import functools

import jax
import jax.numpy as jnp
import numpy as np
from jax import lax
from jax.experimental import pallas as pl
from jax.experimental.pallas import tpu as pltpu

F32 = jnp.float32
BF16 = jnp.bfloat16

EPS = 1e-6
HEAD_DIM = 128
HEADS_PER_GROUP = 4
ATTN_GROUPS = ((128, 1), (512, 4), (2048, 16))
N_GROUPS = len(ATTN_GROUPS)
N_HEADS = HEADS_PER_GROUP * N_GROUPS
GROUP_WIDTH = HEADS_PER_GROUP * HEAD_DIM
ATTN_BLK = 128
ATTN_TILE = 2048
SSM_GROUP = 16
SSM_STATE = 64
SSM_CHUNK = 8
LANES = 128
GROUPS_PER_LANE_BLOCK = LANES // SSM_GROUP
PROJ_TILE = 1024
ROW_ORDERS = tuple(dil for _, dil in ATTN_GROUPS) + (SSM_CHUNK,)
PERM_BASE = 4
NEG = -1e30
LOG2E = 1.4426950408889634
VMEM_LIMIT = 56 * 1024 * 1024


def _params(*sem):
    return pltpu.CompilerParams(dimension_semantics=sem, vmem_limit_bytes=VMEM_LIMIT)


def _sigmoid(x):
    return 0.5 * jnp.tanh(0.5 * x) + 0.5


def _rms(x, gain):
    return x * lax.rsqrt(jnp.mean(x * x, axis=-1, keepdims=True) + EPS) * gain


def _cast_riders(weights, grid):
    ni, nj = grid
    in_specs, out_specs, out_shapes = [], [], []
    for w, ncols in weights:
        k, n = w.shape
        assert k % (ni * 16) == 0 and n % (ncols * LANES) == 0 and ncols <= nj
        spec = pl.BlockSpec((k // ni, n // ncols), functools.partial(
            lambda i, j, last: (i, jnp.minimum(j, last)), last=ncols - 1))
        in_specs.append(spec)
        out_specs.append(spec)
        out_shapes.append(jax.ShapeDtypeStruct(w.shape, BF16))
    return in_specs, out_specs, out_shapes


def _residue_block(r, dil):
    if dil <= PERM_BASE:
        return r
    return (r % PERM_BASE) * (dil // PERM_BASE) + r // PERM_BASE


def _proj_tile(col):
    n_qkv = 3 * N_GROUPS
    return jnp.where(col < n_qkv, (col % N_GROUPS) * 3 + col // N_GROUPS, col)


def _inproj_kernel(x_ref, g_ref, w_ref, *refs, n_qkv, u_tiles, riders):
    ride_in, (o_ref, *ride_out) = refs[:riders], refs[riders:2 * riders + 1]
    h_ref, hn_ref, hb_ref, inv_ref = refs[2 * riders + 1:]
    j = pl.program_id(1)
    slabs, tm, _ = hn_ref.shape

    def project(order):
        o_ref[...] = jnp.dot(h_ref[order], w_ref[...], preferred_element_type=F32).astype(o_ref.dtype)
        for src, dst in zip(ride_in, ride_out):
            dst[...] = src[...].astype(dst.dtype)

    @pl.when(j == 0)
    def _():
        x = x_ref[...]
        inv_ref[...] = jnp.broadcast_to(lax.rsqrt(jnp.mean(x * x, axis=-1, keepdims=True) + EPS), inv_ref.shape)
        q = tm // PERM_BASE
        piece = 256
        for c in range(x_ref.shape[1] // LANES):
            cols = slice(c * LANES, (c + 1) * LANES)
            s = c % slabs
            for r0 in range(0, tm, piece):
                rows = slice(r0, r0 + piece)
                hn = x_ref[rows, cols] * inv_ref[rows, :] * g_ref[:, cols]
                h_ref[0, rows, cols] = hn.astype(BF16)
                hn_ref[s, rows, :] = hn
            for b in range(PERM_BASE):
                part = hn_ref[s, pl.ds(b, q, stride=PERM_BASE), :]
                hb_ref[s, b * q:(b + 1) * q, :] = part
                for v, dil in enumerate(ROW_ORDERS):
                    if dil == PERM_BASE:
                        h_ref[v, b * q:(b + 1) * q, cols] = part.astype(BF16)
            for v, dil in enumerate(ROW_ORDERS):
                if dil > PERM_BASE:
                    k, n = dil // PERM_BASE, tm // dil
                    for b in range(PERM_BASE):
                        for a in range(k):
                            blk = b * k + a
                            h_ref[v, blk * n:(blk + 1) * n, cols] = (
                                hb_ref[s, pl.ds(b * q + a, n, stride=k), :].astype(BF16))
        project(0)

    order = jnp.where(j < n_qkv, j % N_GROUPS, jnp.where(j < n_qkv + u_tiles, N_GROUPS, 0))
    pl.when(j > 0)(functools.partial(project, order))


def _inproj(x, gain, w, ssm_width, ride):
    t, d = x.shape
    tm = PROJ_TILE
    tn = GROUP_WIDTH
    ncol = w.shape[1] // tn
    assert t % tm == 0 and w.shape[1] % tn == 0 and ssm_width % tn == 0 and d % LANES == 0
    assert ROW_ORDERS[0] == 1 and all(tm % (dil * 16) == 0 for dil in ROW_ORDERS)
    assert all(dil in (1, PERM_BASE) or (dil % PERM_BASE == 0 and dil // PERM_BASE <= PERM_BASE) for dil in ROW_ORDERS)
    slabs = 8
    grid = (t // tm, ncol)
    ride_in, ride_out, ride_shapes = _cast_riders(ride, grid)
    return pl.pallas_call(
        functools.partial(_inproj_kernel, n_qkv=3 * N_GROUPS, u_tiles=ssm_width // tn, riders=len(ride)),
        grid=grid,
        in_specs=[
            pl.BlockSpec((tm, d), lambda i, j: (i, 0)),
            pl.BlockSpec((1, d), lambda i, j: (0, 0)),
            pl.BlockSpec((d, tn), lambda i, j: (0, j)),
        ] + ride_in,
        out_specs=[pl.BlockSpec((None, None, tm, tn), lambda i, j: (i, _proj_tile(j), 0, 0))] + ride_out,
        out_shape=[jax.ShapeDtypeStruct((t // tm, ncol, tm, tn), BF16)] + ride_shapes,
        scratch_shapes=[
            pltpu.VMEM((len(ROW_ORDERS), tm, d), BF16),
            pltpu.VMEM((slabs, tm, LANES), F32),
            pltpu.VMEM((slabs, tm, LANES), F32),
            pltpu.VMEM((tm, LANES), F32),
        ],
        compiler_params=_params("parallel", "arbitrary"),
        name="inproj",
    )(x, gain.reshape(1, d), w, *(w for w, _ in ride))


def _attn_bias_table():
    qi = np.arange(ATTN_BLK)[:, None]
    kj = np.arange(ATTN_BLK)[None, :]
    table = np.full((N_GROUPS, 2, HEADS_PER_GROUP * ATTN_BLK, 2 * ATTN_BLK), NEG, np.float32)
    for g, (_, dil) in enumerate(ATTN_GROUPS):
        for h in range(HEADS_PER_GROUP):
            slope = 2.0 ** (-8.0 * (g * HEADS_PER_GROUP + h + 1) / N_HEADS) * dil * LOG2E
            rows = slice(h * ATTN_BLK, (h + 1) * ATTN_BLK)
            cur = np.where(kj <= qi, -slope * (qi - kj), NEG)
            prev = np.where(kj >= qi, -slope * (ATTN_BLK + qi - kj), NEG)
            table[g, :, rows, ATTN_BLK:] = cur
            table[g, 0, rows, :ATTN_BLK] = prev
    return table


def _attn_kernel(bias_ref, *refs):
    qkv_refs = refs[:N_GROUPS]
    o_ref, out_scr, lse_scr = refs[N_GROUPS:N_GROUPS + 3]
    hist_refs = refs[N_GROUPS + 3:]
    tile = pl.program_id(1)
    u = pl.program_id(2)
    units = ATTN_TILE // ATTN_BLK
    nt = (((1,), (1,)), ((), ()))

    @pl.when(jnp.logical_and(tile == 0, u == 0))
    def _():
        for hist in hist_refs:
            hist[...] = jnp.zeros_like(hist)

    blocks, logits, stats = [], [], []
    for g, (_, dil) in enumerate(ATTN_GROUPS):
        nb = u // dil
        r = u % dil
        first = jnp.logical_and(tile == 0, nb == 0).astype(jnp.int32)
        start = nb * (ATTN_BLK * dil) + r
        rows = pl.ds(start, ATTN_BLK) if dil == 1 else pl.ds(start, ATTN_BLK, stride=dil)
        qkv = qkv_refs[g]
        q, k, v = (qkv[..., part, :, :].reshape(ATTN_BLK, GROUP_WIDTH) for part in range(3))
        prev = hist_refs[g][r]
        scores = []
        for h in range(HEADS_PER_GROUP):
            cs = slice(h * HEAD_DIM, (h + 1) * HEAD_DIM)
            keys = jnp.concatenate([prev[:, cs], k[:, cs]], axis=0)
            scores.append(lax.dot_general(q[:, cs], keys, nt, preferred_element_type=F32))
        logits.append(jnp.concatenate(scores, axis=0) * (HEAD_DIM ** -0.5 * LOG2E) + bias_ref[g, first])
        blocks.append((r, rows, k, v, prev))
    for s in logits:
        m = jnp.max(s, axis=1, keepdims=True)
        p = jnp.exp2(s - m)
        l = jnp.sum(p, axis=1, keepdims=True)
        stats.append((p.astype(BF16), 1.0 / l, m + jnp.log2(l)))
    for g, ((r, rows, k, v, prev), (p, inv, lse)) in enumerate(zip(blocks, stats)):
        for h in range(HEADS_PER_GROUP):
            cs = slice(h * HEAD_DIM, (h + 1) * HEAD_DIM)
            hr = slice(h * ATTN_BLK, (h + 1) * ATTN_BLK)
            vals = jnp.concatenate([prev[:, GROUP_WIDTH:][:, cs], v[:, cs]], axis=0)
            o = jnp.dot(p[hr], vals, preferred_element_type=F32)
            out_scr[g, h, rows, :] = o * inv[hr]
            lse_scr[g, h, rows, :] = jnp.broadcast_to(lse[hr], (ATTN_BLK, HEAD_DIM))
        hist_refs[g][r, :, :GROUP_WIDTH] = k
        hist_refs[g][r, :, GROUP_WIDTH:] = v

    @pl.when(u == units - 1)
    def _():
        for h in range(HEADS_PER_GROUP):
            lses = [lse_scr[g, h] for g in range(N_GROUPS)]
            top = functools.reduce(jnp.maximum, lses)
            ws = [jnp.exp2(x - top) for x in lses]
            num = sum(w * out_scr[g, h] for g, w in enumerate(ws))
            o_ref[:, h * HEAD_DIM:(h + 1) * HEAD_DIM] = (num / sum(ws)).astype(o_ref.dtype)


def _attention(proj, bsz, seq):
    assert seq % ATTN_TILE == 0 and ATTN_TILE % PROJ_TILE == 0
    tiles = seq // ATTN_TILE
    units = ATTN_TILE // ATTN_BLK
    bias = jnp.asarray(_attn_bias_table())
    in_specs = [pl.BlockSpec(bias.shape, lambda b, tile, u: (0, 0, 0, 0))]
    hist = []
    for g, (window, dil) in enumerate(ATTN_GROUPS):
        assert window // dil == ATTN_BLK and units % dil == 0
        span = ATTN_BLK * dil
        spans_per_tile = ATTN_TILE // span
        if span <= PROJ_TILE:
            block = (None, 3, ATTN_BLK, GROUP_WIDTH)

            def index(b, tile, u, *, g=g, dil=dil, spt=spans_per_tile, sppt=PROJ_TILE // span,
                      per_seq=seq // PROJ_TILE):
                sp = tile * spt + u // dil
                return b * per_seq + sp // sppt, g, _residue_block(u % dil, dil) * sppt + sp % sppt, 0
        else:
            block = (span // PROJ_TILE, 3, PROJ_TILE // dil, GROUP_WIDTH)

            def index(b, tile, u, *, g=g, dil=dil, spt=spans_per_tile, per_seq=seq // span):
                return b * per_seq + tile * spt + u // dil, g, _residue_block(u % dil, dil), 0

        in_specs.append(pl.BlockSpec(block, index))
        hist.append(pltpu.VMEM((dil, ATTN_BLK, 2 * GROUP_WIDTH), BF16))
    scratch = pltpu.VMEM((N_GROUPS, HEADS_PER_GROUP, ATTN_TILE, HEAD_DIM), F32)
    return pl.pallas_call(
        _attn_kernel,
        grid=(bsz, tiles, units),
        in_specs=in_specs,
        out_specs=pl.BlockSpec((ATTN_TILE, GROUP_WIDTH), lambda b, tile, u: (b * tiles + tile, 0)),
        out_shape=jax.ShapeDtypeStruct((bsz * seq, GROUP_WIDTH), BF16),
        scratch_shapes=[scratch, scratch] + hist,
        compiler_params=_params("arbitrary", "arbitrary", "arbitrary"),
        name="dilated_attention",
    )(bias, *([proj] * N_GROUPS))


def _cmul(a, b):
    return a[0] * b[0] - a[1] * b[1], a[0] * b[1] + a[1] * b[0]


def _ssm_weights(a_re, a_im, log_dt, b_re, b_im, c_re, c_im):
    n_groups = a_re.shape[0]
    nblk = n_groups // GROUPS_PER_LANE_BLOCK
    gl = GROUPS_PER_LANE_BLOCK
    L = SSM_CHUNK
    a_re, a_im = a_re.astype(F32), a_im.astype(F32)
    dt = jnp.exp(log_dt.astype(F32))[:, None]
    steps = jnp.arange(L + 1, dtype=F32)[None, :, None]
    mag = jnp.exp((a_re * dt)[:, None, :] * steps)
    ang = (a_im * dt)[:, None, :] * steps
    powers = (mag * jnp.cos(ang), mag * jnp.sin(ang))
    lam_bar = (powers[0][:, 1], powers[1][:, 1])
    den = a_re * a_re + a_im * a_im
    num = (lam_bar[0] - 1.0, lam_bar[1])
    ratio = ((num[0] * a_re + num[1] * a_im) / den, (num[1] * a_re - num[0] * a_im) / den)
    b_bar = _cmul((ratio[0][..., None], ratio[1][..., None]), (b_re.astype(F32), b_im.astype(F32)))
    c_t = (c_re.astype(F32).transpose(0, 2, 1), c_im.astype(F32).transpose(0, 2, 1))
    eye = jnp.eye(gl, dtype=F32)

    def block_diag(m):
        rows, cols = m.shape[1:]
        m = m.reshape(nblk, gl, rows, 1, cols) * eye[None, :, None, :, None]
        return m.reshape(nblk, gl * rows, gl * cols)

    b_in = jnp.stack([block_diag(b.transpose(0, 2, 1)) for b in b_bar], axis=1)
    c_out = jnp.stack([block_diag(c) for c in c_t], axis=1)
    pw = jnp.stack(powers, axis=0).reshape(2, nblk, gl, L + 1, SSM_STATE)
    pw_row = pw.transpose(1, 0, 3, 2, 4).reshape(nblk, 2, L + 1, gl * SSM_STATE)
    pw_col = pw_row.transpose(0, 1, 3, 2)
    return b_in, c_out, pw_row, pw_col


def _ssm_kernel(*refs, tiles_per_seq):
    u_ref = refs[0]
    (bin_ref, cout_ref, pwr_ref, pwc_ref, d_ref, y_ref,
     wt_scr, wb_scr, wl_scr, wc_scr, s_scr, xp_scr, carry_scr, y_scr) = refs[1:]
    L = SSM_CHUNK
    nb = wt_scr.shape[0]
    tc = s_scr.shape[0]
    half = s_scr.shape[1] // 2
    sw = half // nb

    @pl.when(pl.program_id(1) == 0)
    def _():
        for q in range(nb):
            c_hi = [cout_ref[q, ri].astype(BF16) for ri in range(2)]
            c_lo = [(cout_ref[q, ri] - c_hi[ri].astype(F32)).astype(BF16) for ri in range(2)]
            for j in range(L):
                rows = slice(j * LANES, (j + 1) * LANES)
                n = L - 1 - j
                pr = pwr_ref[q, 0, n:n + 1, :]
                pi = pwr_ref[q, 1, n:n + 1, :]
                for ri, val in enumerate((bin_ref[q, 0] * pr - bin_ref[q, 1] * pi,
                                          bin_ref[q, 0] * pi + bin_ref[q, 1] * pr)):
                    cols = slice(ri * sw, (ri + 1) * sw)
                    top = val.astype(BF16)
                    wb_scr[q, rows, cols] = top
                    wl_scr[rows, cols] = (val - top.astype(F32)).astype(BF16)
                pr = pwc_ref[q, 0, :, j + 1:j + 2]
                pi = pwc_ref[q, 1, :, j + 1:j + 2]
                wc_scr[q, :sw, rows] = (cout_ref[q, 0] * pr - cout_ref[q, 1] * pi).astype(BF16)
                wc_scr[q, sw:, rows] = (-(cout_ref[q, 0] * pi + cout_ref[q, 1] * pr)).astype(BF16)
            prods = []
            for ri in range(2):
                cols = slice(ri * sw, (ri + 1) * sw)
                top, low = wb_scr[q, :, cols], wl_scr[:, cols]
                prods.append(jnp.dot(top, c_hi[ri], preferred_element_type=F32)
                             + jnp.dot(top, c_lo[ri], preferred_element_type=F32)
                             + jnp.dot(low, c_hi[ri], preferred_element_type=F32))
            k_all = (prods[0] - prods[1]).astype(BF16)
            for j in range(L):
                n = L - 1 - j
                kn = k_all[j * LANES:(j + 1) * LANES]
                for jj in range(L):
                    ii = jj + n
                    if ii < L:
                        wt_scr[q, jj * LANES:(jj + 1) * LANES, ii * LANES:(ii + 1) * LANES] = kn
                    if j < jj:
                        wt_scr[q, jj * LANES:(jj + 1) * LANES, j * LANES:(j + 1) * LANES] = jnp.zeros((LANES, LANES), BF16)

    @pl.when(pl.program_id(1) % tiles_per_seq == 0)
    def _():
        carry_scr[...] = jnp.zeros_like(carry_scr)

    crows = u_ref.shape[1] // L
    us = [u_ref[:, _residue_block(j, L) * crows:(_residue_block(j, L) + 1) * crows, :].reshape(tc, nb * LANES)
          for j in range(L)]
    ucat = [jnp.concatenate([u[:, q * LANES:(q + 1) * LANES] for u in us], axis=1)
            for q in range(nb)]
    for q in range(nb):
        s = jnp.dot(ucat[q], wb_scr[q], preferred_element_type=F32)
        s_scr[:, q * sw:(q + 1) * sw] = s[:, :sw]
        s_scr[:, half + q * sw:half + (q + 1) * sw] = s[:, sw:]
    ys = [jnp.dot(ucat[q], wt_scr[q], preferred_element_type=F32) for q in range(nb)]
    for q in range(nb):
        re = slice(q * sw, (q + 1) * sw)
        im = slice(half + q * sw, half + (q + 1) * sw)
        ar = pwr_ref[q, 0, L:L + 1, :]
        ai = pwr_ref[q, 1, L:L + 1, :]

        xr, xi = carry_scr[:, re], carry_scr[:, im]
        for c in range(tc):
            xp_scr[c:c + 1, re] = xr
            xp_scr[c:c + 1, im] = xi
            xr, xi = (ar * xr - ai * xi + s_scr[c:c + 1, re], ar * xi + ai * xr + s_scr[c:c + 1, im])
        carry_scr[:, re] = xr
        carry_scr[:, im] = xi

    for q in range(nb):
        lanes = slice(q * LANES, (q + 1) * LANES)
        xq = jnp.concatenate([xp_scr[:, q * sw:(q + 1) * sw], xp_scr[:, half + q * sw:half + (q + 1) * sw]], axis=1)
        y = ys[q] + jnp.dot(xq.astype(BF16), wc_scr[q], preferred_element_type=F32)
        for i in range(L):
            yi = y[:, i * LANES:(i + 1) * LANES] + d_ref[0, :, lanes] * us[i][:, lanes].astype(F32)
            y_scr[q, pl.ds(i, tc, stride=L), :] = jax.nn.gelu(yi)
        y_ref[:, lanes] = y_scr[q].astype(y_ref.dtype)


def _ssm(proj, u_tile0, ssm_w, d_skip, bsz, seq, *, ptiles, nb):
    nt, _, tm, tn = proj.shape
    L = SSM_CHUNK
    per_tile = tn // (nb * LANES)
    crows = tm // L
    nblk = ssm_w[0].shape[0]
    width = nblk * LANES
    assert crows == LANES and nblk % nb == 0 and tn % (nb * LANES) == 0
    tc = ptiles * crows
    rows_per_seq = seq // L
    assert rows_per_seq % tc == 0 and nt % ptiles == 0
    u_spec = pl.BlockSpec((ptiles, None, tm, nb * LANES), lambda blk, i: (i, u_tile0 + blk // per_tile, 0, blk % per_tile))
    w_specs = [pl.BlockSpec((nb,) + w.shape[1:], lambda blk, i: (blk, 0, 0, 0)) for w in ssm_w]
    wide = L * LANES
    states = 2 * GROUPS_PER_LANE_BLOCK * SSM_STATE
    return pl.pallas_call(
        functools.partial(_ssm_kernel, tiles_per_seq=rows_per_seq // tc),
        grid=(nblk // nb, nt // ptiles),
        in_specs=[u_spec] + w_specs + [pl.BlockSpec((1, 1, nb * LANES), lambda blk, i: (blk, 0, 0))],
        out_specs=pl.BlockSpec((tc * L, nb * LANES), lambda blk, i: (i, blk)),
        out_shape=jax.ShapeDtypeStruct((nt * tm, width), BF16),
        scratch_shapes=[
            pltpu.VMEM((nb, wide, wide), BF16),
            pltpu.VMEM((nb, wide, states), BF16),
            pltpu.VMEM((wide, states), BF16),
            pltpu.VMEM((nb, states, wide), BF16),
            pltpu.VMEM((tc, nb * states), F32),
            pltpu.VMEM((tc, nb * states), F32),
            pltpu.VMEM((1, nb * states), F32),
            pltpu.VMEM((nb, tc * L, LANES), F32),
        ],
        compiler_params=_params("parallel", "arbitrary"),
        name="s5_chunked",
    )(proj, *ssm_w, d_skip.astype(F32).reshape(nblk // nb, 1, nb * LANES))


def _merge_kernel(attn_ref, y_ref, ga_ref, gs_ref, wup_ref, wv_ref, wg_ref, *refs):
    riders = len(refs) // 2
    ride_in, (o_ref, *ride_out) = refs[:riders], refs[riders:]
    y = y_ref[...]
    gate = _sigmoid(jnp.dot(y, wg_ref[...].astype(BF16), preferred_element_type=F32))
    gate_s = _sigmoid(gs_ref[...].astype(F32))
    gate_a = _sigmoid(ga_ref[...].astype(F32))
    val = jnp.dot(y, wv_ref[...].astype(BF16), preferred_element_type=F32)
    attn_branch = jnp.dot(attn_ref[...], wup_ref[...].astype(BF16), preferred_element_type=F32)
    merged = gate_a * attn_branch + gate_s * (val * gate)
    o_ref[...] = merged.astype(o_ref.dtype)
    for src, dst in zip(ride_in, ride_out):
        dst[...] = src[...].astype(dst.dtype)


def _merge(attn, y, proj, gate_tile0, w_up, w_v, w_g, ride):
    t = attn.shape[0]
    n = w_up.shape[1]
    _, _, tm, tn = proj.shape
    assert n % tn == 0
    grid = (t // tm, n // tn)
    ride_in, ride_out, ride_shapes = _cast_riders(ride, grid)
    return pl.pallas_call(
        _merge_kernel,
        grid=grid,
        in_specs=[
            pl.BlockSpec((tm, attn.shape[1]), lambda i, j: (i, 0)),
            pl.BlockSpec((tm, y.shape[1]), lambda i, j: (i, 0)),
            pl.BlockSpec((None, None, tm, tn), lambda i, j: (i, gate_tile0 + j, 0, 0)),
            pl.BlockSpec((None, None, tm, tn), lambda i, j: (i, gate_tile0 + n // tn + j, 0, 0)),
            pl.BlockSpec((w_up.shape[0], tn), lambda i, j: (0, j)),
            pl.BlockSpec((w_v.shape[0], tn), lambda i, j: (0, j)),
            pl.BlockSpec((w_g.shape[0], tn), lambda i, j: (0, j)),
        ] + ride_in,
        out_specs=[pl.BlockSpec((tm, tn), lambda i, j: (i, j))] + ride_out,
        out_shape=[jax.ShapeDtypeStruct((t, n), BF16)] + ride_shapes,
        compiler_params=_params("parallel", "arbitrary"),
        name="gated_merge",
    )(attn, y, proj, proj, w_up, w_v, w_g, *(w for w, _ in ride))


def _outproj_kernel(m_ref, w_ref, x_ref, g_ref, o_ref):
    z = jnp.dot(m_ref[...], w_ref[...], preferred_element_type=F32)
    o_ref[...] = x_ref[...] + _rms(z, g_ref[...])


def _outproj(merged, w, x, gain, *, tm):
    t, d = x.shape
    return pl.pallas_call(
        _outproj_kernel,
        grid=(t // tm,),
        in_specs=[
            pl.BlockSpec((tm, merged.shape[1]), lambda i: (i, 0)),
            pl.BlockSpec(w.shape, lambda i: (0, 0)),
            pl.BlockSpec((tm, d), lambda i: (i, 0)),
            pl.BlockSpec((1, d), lambda i: (0, 0)),
        ],
        out_specs=pl.BlockSpec((tm, d), lambda i: (i, 0)),
        out_shape=jax.ShapeDtypeStruct((t, d), F32),
        compiler_params=_params("parallel"),
        name="outproj_norm_residual",
    )(merged, w, x, gain.reshape(1, d))


def _ffn_kernel(x_ref, gpre_ref, gpost_ref, wg_ref, wu_ref, wd_ref, o_ref, h_ref, acc_ref):
    k = pl.program_id(1)
    last = pl.num_programs(1) - 1

    def step(first, final):
        if first:
            h_ref[...] = _rms(x_ref[...], gpre_ref[...]).astype(BF16)
        h = h_ref[...]
        gate = jnp.dot(h, wg_ref[...], preferred_element_type=F32)
        up = jnp.dot(h, wu_ref[...], preferred_element_type=F32)
        f = (jax.nn.silu(gate) * up).astype(BF16)
        down = jnp.dot(f, wd_ref[...], preferred_element_type=F32)
        if first:
            acc_ref[...] = down
        elif final:
            o_ref[...] = x_ref[...] + _rms(acc_ref[...] + down, gpost_ref[...])
        else:
            acc_ref[...] += down

    pl.when(k == 0)(functools.partial(step, True, False))
    pl.when(jnp.logical_and(k > 0, k < last))(functools.partial(step, False, False))
    pl.when(k == last)(functools.partial(step, False, True))


def _ffn(x, gain_pre, gain_post, w_gate, w_up, w_down, *, tm, tf):
    t, d = x.shape
    dff = w_gate.shape[1]
    return pl.pallas_call(
        _ffn_kernel,
        grid=(t // tm, dff // tf),
        in_specs=[
            pl.BlockSpec((tm, d), lambda i, k: (i, 0)),
            pl.BlockSpec((1, d), lambda i, k: (0, 0)),
            pl.BlockSpec((1, d), lambda i, k: (0, 0)),
            pl.BlockSpec((d, tf), lambda i, k: (0, k)),
            pl.BlockSpec((d, tf), lambda i, k: (0, k)),
            pl.BlockSpec((tf, d), lambda i, k: (k, 0)),
        ],
        out_specs=pl.BlockSpec((tm, d), lambda i, k: (i, 0)),
        out_shape=jax.ShapeDtypeStruct((t, d), F32),
        scratch_shapes=[pltpu.VMEM((tm, d), BF16), pltpu.VMEM((tm, d), F32)],
        compiler_params=_params("parallel", "arbitrary"),
        name="swiglu_ffn",
    )(x, gain_pre.reshape(1, d), gain_post.reshape(1, d), w_gate, w_up, w_down)


def _layer(x, norm_mix_pre, w_in, w_attn_up, ssm_a_re, ssm_a_im, ssm_log_dt, ssm_b_re, ssm_b_im,
           ssm_c_re, ssm_c_im, ssm_d, w_glu_v, w_glu_g, w_out, norm_mix_post, norm_ffn_pre,
           w_ffn_gate, w_ffn_up, w_ffn_down, norm_ffn_post):
    bsz, seq, d = x.shape
    t = bsz * seq
    ssm_width = ssm_d.shape[0]
    u_col0 = 3 * N_HEADS * HEAD_DIM
    assert w_in.shape[1] == u_col0 + ssm_width + 2 * d

    x2 = x.reshape(t, d)
    tn = GROUP_WIDTH
    ffn_tiles = w_ffn_gate.shape[1] // 512
    proj, w_gate, w_up = _inproj(x2, norm_mix_pre, w_in.astype(BF16), ssm_width,
                                 [(w_ffn_gate, ffn_tiles), (w_ffn_up, ffn_tiles)])
    attn = _attention(proj, bsz, seq)
    ssm_w = _ssm_weights(ssm_a_re, ssm_a_im, ssm_log_dt, ssm_b_re, ssm_b_im, ssm_c_re, ssm_c_im)
    y = _ssm(proj, u_col0 // tn, ssm_w, ssm_d, bsz, seq, ptiles=4, nb=2)
    merged, w_down, w_outp = _merge(attn, y, proj, (u_col0 + ssm_width) // tn, w_attn_up, w_glu_v, w_glu_g,
                                    [(w_ffn_down, d // tn), (w_out, d // tn)])
    x1 = _outproj(merged, w_outp, x2, norm_mix_post, tm=512)
    out = _ffn(x1, norm_ffn_pre, norm_ffn_post, w_gate, w_up, w_down, tm=512, tf=512)
    return out.reshape(bsz, seq, d)


def kernel(x, norm_mix_pre, w_in, w_attn_up, ssm_a_re, ssm_a_im, ssm_log_dt, ssm_b_re, ssm_b_im, ssm_c_re, ssm_c_im, ssm_d, w_glu_v, w_glu_g, w_out, norm_mix_post, norm_ffn_pre, w_ffn_gate, w_ffn_up, w_ffn_down, norm_ffn_post):
    stacked = (norm_mix_pre, w_in, w_attn_up, ssm_a_re, ssm_a_im, ssm_log_dt, ssm_b_re, ssm_b_im, ssm_c_re,
               ssm_c_im, ssm_d, w_glu_v, w_glu_g, w_out, norm_mix_post, norm_ffn_pre, w_ffn_gate, w_ffn_up,
               w_ffn_down, norm_ffn_post)
    for layer in range(norm_mix_pre.shape[0]):
        x = _layer(x, *(p[layer] for p in stacked))
    return x
```

```python
import functools

import jax
import jax.numpy as jnp
import numpy as np
from jax import lax
from jax.experimental import pallas as pl
from jax.experimental.pallas import tpu as pltpu

F32 = jnp.float32
BF16 = jnp.bfloat16

EPS = 1e-6
HEAD_DIM = 128
HEADS_PER_GROUP = 4
ATTN_GROUPS = ((128, 1), (512, 4), (2048, 16))
N_GROUPS = len(ATTN_GROUPS)
N_HEADS = HEADS_PER_GROUP * N_GROUPS
GROUP_WIDTH = HEADS_PER_GROUP * HEAD_DIM
ATTN_BLK = 128
ATTN_TILE = 2048
SSM_GROUP = 16
SSM_STATE = 64
SSM_CHUNK = 8
LANES = 128
GROUPS_PER_LANE_BLOCK = LANES // SSM_GROUP
PROJ_TILE = 1024
ROW_ORDERS = tuple(dil for _, dil in ATTN_GROUPS) + (SSM_CHUNK,)
PERM_BASE = 4
NEG = -1e30
LOG2E = 1.4426950408889634
VMEM_LIMIT = 56 * 1024 * 1024


def _params(*sem):
    return pltpu.CompilerParams(dimension_semantics=sem, vmem_limit_bytes=VMEM_LIMIT)


def _sigmoid(x):
    return 0.5 * jnp.tanh(0.5 * x) + 0.5


def _rms(x, gain):
    return x * lax.rsqrt(jnp.mean(x * x, axis=-1, keepdims=True) + EPS) * gain


def _cast_riders(weights, grid):
    ni, nj = grid
    in_specs, out_specs, out_shapes = [], [], []
    for w, ncols in weights:
        k, n = w.shape
        assert k % (ni * 16) == 0 and n % (ncols * LANES) == 0 and ncols <= nj
        spec = pl.BlockSpec((k // ni, n // ncols), functools.partial(
            lambda i, j, last: (i, jnp.minimum(j, last)), last=ncols - 1))
        in_specs.append(spec)
        out_specs.append(spec)
        out_shapes.append(jax.ShapeDtypeStruct(w.shape, BF16))
    return in_specs, out_specs, out_shapes


def _residue_block(r, dil):
    if dil <= PERM_BASE:
        return r
    return (r % PERM_BASE) * (dil // PERM_BASE) + r // PERM_BASE


def _proj_tile(col):
    n_qkv = 3 * N_GROUPS
    return jnp.where(col < n_qkv, (col % N_GROUPS) * 3 + col // N_GROUPS, col)


def _inproj_kernel(x_ref, g_ref, w_ref, *refs, n_qkv, u_tiles, riders):
    ride_in, (o_ref, *ride_out) = refs[:riders], refs[riders:2 * riders + 1]
    h_ref, hn_ref, hb_ref, inv_ref = refs[2 * riders + 1:]
    j = pl.program_id(1)
    slabs, tm, _ = hn_ref.shape

    def project(order):
        o_ref[...] = jnp.dot(h_ref[order], w_ref[...], preferred_element_type=F32).astype(o_ref.dtype)
        for src, dst in zip(ride_in, ride_out):
            dst[...] = src[...].astype(dst.dtype)

    @pl.when(j == 0)
    def _():
        x = x_ref[...]
        inv_ref[...] = jnp.broadcast_to(lax.rsqrt(jnp.mean(x * x, axis=-1, keepdims=True) + EPS), inv_ref.shape)
        q = tm // PERM_BASE
        piece = 256
        for c in range(x_ref.shape[1] // LANES):
            cols = slice(c * LANES, (c + 1) * LANES)
            s = c % slabs
            for r0 in range(0, tm, piece):
                rows = slice(r0, r0 + piece)
                hn = x_ref[rows, cols] * inv_ref[rows, :] * g_ref[:, cols]
                h_ref[0, rows, cols] = hn.astype(BF16)
                hn_ref[s, rows, :] = hn
            for b in range(PERM_BASE):
                part = hn_ref[s, pl.ds(b, q, stride=PERM_BASE), :]
                hb_ref[s, b * q:(b + 1) * q, :] = part
                for v, dil in enumerate(ROW_ORDERS):
                    if dil == PERM_BASE:
                        h_ref[v, b * q:(b + 1) * q, cols] = part.astype(BF16)
            for v, dil in enumerate(ROW_ORDERS):
                if dil > PERM_BASE:
                    k, n = dil // PERM_BASE, tm // dil
                    for b in range(PERM_BASE):
                        for a in range(k):
                            blk = b * k + a
                            h_ref[v, blk * n:(blk + 1) * n, cols] = (
                                hb_ref[s, pl.ds(b * q + a, n, stride=k), :].astype(BF16))
        project(0)

    order = jnp.where(j < n_qkv, j % N_GROUPS, jnp.where(j < n_qkv + u_tiles, N_GROUPS, 0))
    pl.when(j > 0)(functools.partial(project, order))


def _inproj(x, gain, w, ssm_width, ride):
    t, d = x.shape
    tm = PROJ_TILE
    tn = GROUP_WIDTH
    ncol = w.shape[1] // tn
    assert t % tm == 0 and w.shape[1] % tn == 0 and ssm_width % tn == 0 and d % LANES == 0
    assert ROW_ORDERS[0] == 1 and all(tm % (dil * 16) == 0 for dil in ROW_ORDERS)
    assert all(dil in (1, PERM_BASE) or (dil % PERM_BASE == 0 and dil // PERM_BASE <= PERM_BASE) for dil in ROW_ORDERS)
    slabs = 8
    grid = (t // tm, ncol)
    ride_in, ride_out, ride_shapes = _cast_riders(ride, grid)
    return pl.pallas_call(
        functools.partial(_inproj_kernel, n_qkv=3 * N_GROUPS, u_tiles=ssm_width // tn, riders=len(ride)),
        grid=grid,
        in_specs=[
            pl.BlockSpec((tm, d), lambda i, j: (i, 0)),
            pl.BlockSpec((1, d), lambda i, j: (0, 0)),
            pl.BlockSpec((d, tn), lambda i, j: (0, j)),
        ] + ride_in,
        out_specs=[pl.BlockSpec((None, None, tm, tn), lambda i, j: (i, _proj_tile(j), 0, 0))] + ride_out,
        out_shape=[jax.ShapeDtypeStruct((t // tm, ncol, tm, tn), BF16)] + ride_shapes,
        scratch_shapes=[
            pltpu.VMEM((len(ROW_ORDERS), tm, d), BF16),
            pltpu.VMEM((slabs, tm, LANES), F32),
            pltpu.VMEM((slabs, tm, LANES), F32),
            pltpu.VMEM((tm, LANES), F32),
        ],
        compiler_params=_params("parallel", "arbitrary"),
        name="inproj",
    )(x, gain.reshape(1, d), w, *(w for w, _ in ride))


def _attn_bias_table():
    qi = np.arange(ATTN_BLK)[:, None]
    kj = np.arange(ATTN_BLK)[None, :]
    table = np.full((N_GROUPS, 2, HEADS_PER_GROUP * ATTN_BLK, 2 * ATTN_BLK), NEG, np.float32)
    for g, (_, dil) in enumerate(ATTN_GROUPS):
        for h in range(HEADS_PER_GROUP):
            slope = 2.0 ** (-8.0 * (g * HEADS_PER_GROUP + h + 1) / N_HEADS) * dil * LOG2E
            rows = slice(h * ATTN_BLK, (h + 1) * ATTN_BLK)
            cur = np.where(kj <= qi, -slope * (qi - kj), NEG)
            prev = np.where(kj >= qi, -slope * (ATTN_BLK + qi - kj), NEG)
            table[g, :, rows, ATTN_BLK:] = cur
            table[g, 0, rows, :ATTN_BLK] = prev
    return table


def _attn_kernel(bias_ref, *refs):
    qkv_refs = refs[:N_GROUPS]
    o_ref, out_scr, lse_scr = refs[N_GROUPS:N_GROUPS + 3]
    hist_refs = refs[N_GROUPS + 3:]
    tile = pl.program_id(1)
    u = pl.program_id(2)
    units = ATTN_TILE // ATTN_BLK
    nt = (((1,), (1,)), ((), ()))

    @pl.when(jnp.logical_and(tile == 0, u == 0))
    def _():
        for hist in hist_refs:
            hist[...] = jnp.zeros_like(hist)

    blocks, logits, stats = [], [], []
    for g, (_, dil) in enumerate(ATTN_GROUPS):
        nb = u // dil
        r = u % dil
        first = jnp.logical_and(tile == 0, nb == 0).astype(jnp.int32)
        start = nb * (ATTN_BLK * dil) + r
        rows = pl.ds(start, ATTN_BLK) if dil == 1 else pl.ds(start, ATTN_BLK, stride=dil)
        qkv = qkv_refs[g]
        q, k, v = (qkv[..., part, :, :].reshape(ATTN_BLK, GROUP_WIDTH) for part in range(3))
        prev = hist_refs[g][r]
        scores = []
        for h in range(HEADS_PER_GROUP):
            cs = slice(h * HEAD_DIM, (h + 1) * HEAD_DIM)
            keys = jnp.concatenate([prev[:, cs], k[:, cs]], axis=0)
            scores.append(lax.dot_general(q[:, cs], keys, nt, preferred_element_type=F32))
        logits.append(jnp.concatenate(scores, axis=0) * (HEAD_DIM ** -0.5 * LOG2E) + bias_ref[g, first])
        blocks.append((r, rows, k, v, prev))
    for s in logits:
        m = jnp.max(s, axis=1, keepdims=True)
        p = jnp.exp2(s - m)
        l = jnp.sum(p, axis=1, keepdims=True)
        stats.append((p.astype(BF16), 1.0 / l, m + jnp.log2(l)))
    for g, ((r, rows, k, v, prev), (p, inv, lse)) in enumerate(zip(blocks, stats)):
        for h in range(HEADS_PER_GROUP):
            cs = slice(h * HEAD_DIM, (h + 1) * HEAD_DIM)
            hr = slice(h * ATTN_BLK, (h + 1) * ATTN_BLK)
            vals = jnp.concatenate([prev[:, GROUP_WIDTH:][:, cs], v[:, cs]], axis=0)
            o = jnp.dot(p[hr], vals, preferred_element_type=F32)
            out_scr[g, h, rows, :] = o * inv[hr]
            lse_scr[g, h, rows, :] = jnp.broadcast_to(lse[hr], (ATTN_BLK, HEAD_DIM))
        hist_refs[g][r, :, :GROUP_WIDTH] = k
        hist_refs[g][r, :, GROUP_WIDTH:] = v

    @pl.when(u == units - 1)
    def _():
        for h in range(HEADS_PER_GROUP):
            lses = [lse_scr[g, h] for g in range(N_GROUPS)]
            top = functools.reduce(jnp.maximum, lses)
            ws = [jnp.exp2(x - top) for x in lses]
            num = sum(w * out_scr[g, h] for g, w in enumerate(ws))
            o_ref[:, h * HEAD_DIM:(h + 1) * HEAD_DIM] = (num / sum(ws)).astype(o_ref.dtype)


def _attention(proj, bsz, seq):
    assert seq % ATTN_TILE == 0 and ATTN_TILE % PROJ_TILE == 0
    tiles = seq // ATTN_TILE
    units = ATTN_TILE // ATTN_BLK
    bias = jnp.asarray(_attn_bias_table())
    in_specs = [pl.BlockSpec(bias.shape, lambda b, tile, u: (0, 0, 0, 0))]
    hist = []
    for g, (window, dil) in enumerate(ATTN_GROUPS):
        assert window // dil == ATTN_BLK and units % dil == 0
        span = ATTN_BLK * dil
        spans_per_tile = ATTN_TILE // span
        if span <= PROJ_TILE:
            block = (None, 3, ATTN_BLK, GROUP_WIDTH)

            def index(b, tile, u, *, g=g, dil=dil, spt=spans_per_tile, sppt=PROJ_TILE // span,
                      per_seq=seq // PROJ_TILE):
                sp = tile * spt + u // dil
                return b * per_seq + sp // sppt, g, _residue_block(u % dil, dil) * sppt + sp % sppt, 0
        else:
            block = (span // PROJ_TILE, 3, PROJ_TILE // dil, GROUP_WIDTH)

            def index(b, tile, u, *, g=g, dil=dil, spt=spans_per_tile, per_seq=seq // span):
                return b * per_seq + tile * spt + u // dil, g, _residue_block(u % dil, dil), 0

        in_specs.append(pl.BlockSpec(block, index))
        hist.append(pltpu.VMEM((dil, ATTN_BLK, 2 * GROUP_WIDTH), BF16))
    scratch = pltpu.VMEM((N_GROUPS, HEADS_PER_GROUP, ATTN_TILE, HEAD_DIM), F32)
    return pl.pallas_call(
        _attn_kernel,
        grid=(bsz, tiles, units),
        in_specs=in_specs,
        out_specs=pl.BlockSpec((ATTN_TILE, GROUP_WIDTH), lambda b, tile, u: (b * tiles + tile, 0)),
        out_shape=jax.ShapeDtypeStruct((bsz * seq, GROUP_WIDTH), BF16),
        scratch_shapes=[scratch, scratch] + hist,
        compiler_params=_params("arbitrary", "arbitrary", "arbitrary"),
        name="dilated_attention",
    )(bias, *([proj] * N_GROUPS))


def _cmul(a, b):
    return a[0] * b[0] - a[1] * b[1], a[0] * b[1] + a[1] * b[0]


def _ssm_weights(a_re, a_im, log_dt, b_re, b_im, c_re, c_im):
    n_groups = a_re.shape[0]
    nblk = n_groups // GROUPS_PER_LANE_BLOCK
    gl = GROUPS_PER_LANE_BLOCK
    L = SSM_CHUNK
    a_re, a_im = a_re.astype(F32), a_im.astype(F32)
    dt = jnp.exp(log_dt.astype(F32))[:, None]
    steps = jnp.arange(L + 1, dtype=F32)[None, :, None]
    mag = jnp.exp((a_re * dt)[:, None, :] * steps)
    ang = (a_im * dt)[:, None, :] * steps
    powers = (mag * jnp.cos(ang), mag * jnp.sin(ang))
    lam_bar = (powers[0][:, 1], powers[1][:, 1])
    den = a_re * a_re + a_im * a_im
    num = (lam_bar[0] - 1.0, lam_bar[1])
    ratio = ((num[0] * a_re + num[1] * a_im) / den, (num[1] * a_re - num[0] * a_im) / den)
    b_bar = _cmul((ratio[0][..., None], ratio[1][..., None]), (b_re.astype(F32), b_im.astype(F32)))
    c_t = (c_re.astype(F32).transpose(0, 2, 1), c_im.astype(F32).transpose(0, 2, 1))
    eye = jnp.eye(gl, dtype=F32)

    def block_diag(m):
        rows, cols = m.shape[1:]
        m = m.reshape(nblk, gl, rows, 1, cols) * eye[None, :, None, :, None]
        return m.reshape(nblk, gl * rows, gl * cols)

    b_in = jnp.stack([block_diag(b.transpose(0, 2, 1)) for b in b_bar], axis=1)
    c_out = jnp.stack([block_diag(c) for c in c_t], axis=1)
    pw = jnp.stack(powers, axis=0).reshape(2, nblk, gl, L + 1, SSM_STATE)
    pw_row = pw.transpose(1, 0, 3, 2, 4).reshape(nblk, 2, L + 1, gl * SSM_STATE)
    pw_col = pw_row.transpose(0, 1, 3, 2)
    return b_in, c_out, pw_row, pw_col


def _ssm_kernel(*refs, tiles_per_seq):
    u_ref = refs[0]
    (bin_ref, cout_ref, pwr_ref, pwc_ref, d_ref, y_ref,
     wt_scr, wb_scr, wl_scr, wc_scr, s_scr, xp_scr, carry_scr, y_scr) = refs[1:]
    L = SSM_CHUNK
    nb = wt_scr.shape[0]
    tc = s_scr.shape[0]
    half = s_scr.shape[1] // 2
    sw = half // nb

    @pl.when(pl.program_id(1) == 0)
    def _():
        for q in range(nb):
            c_hi = [cout_ref[q, ri].astype(BF16) for ri in range(2)]
            c_lo = [(cout_ref[q, ri] - c_hi[ri].astype(F32)).astype(BF16) for ri in range(2)]
            for j in range(L):
                rows = slice(j * LANES, (j + 1) * LANES)
                n = L - 1 - j
                pr = pwr_ref[q, 0, n:n + 1, :]
                pi = pwr_ref[q, 1, n:n + 1, :]
                for ri, val in enumerate((bin_ref[q, 0] * pr - bin_ref[q, 1] * pi,
                                          bin_ref[q, 0] * pi + bin_ref[q, 1] * pr)):
                    cols = slice(ri * sw, (ri + 1) * sw)
                    top = val.astype(BF16)
                    wb_scr[q, rows, cols] = top
                    wl_scr[rows, cols] = (val - top.astype(F32)).astype(BF16)
                pr = pwc_ref[q, 0, :, j + 1:j + 2]
                pi = pwc_ref[q, 1, :, j + 1:j + 2]
                wc_scr[q, :sw, rows] = (cout_ref[q, 0] * pr - cout_ref[q, 1] * pi).astype(BF16)
                wc_scr[q, sw:, rows] = (-(cout_ref[q, 0] * pi + cout_ref[q, 1] * pr)).astype(BF16)
            prods = []
            for ri in range(2):
                cols = slice(ri * sw, (ri + 1) * sw)
                top, low = wb_scr[q, :, cols], wl_scr[:, cols]
                prods.append(jnp.dot(top, c_hi[ri], preferred_element_type=F32)
                             + jnp.dot(top, c_lo[ri], preferred_element_type=F32)
                             + jnp.dot(low, c_hi[ri], preferred_element_type=F32))
            k_all = (prods[0] - prods[1]).astype(BF16)
            for j in range(L):
                n = L - 1 - j
                kn = k_all[j * LANES:(j + 1) * LANES]
                for jj in range(L):
                    ii = jj + n
                    if ii < L:
                        wt_scr[q, jj * LANES:(jj + 1) * LANES, ii * LANES:(ii + 1) * LANES] = kn
                    if j < jj:
                        wt_scr[q, jj * LANES:(jj + 1) * LANES, j * LANES:(j + 1) * LANES] = jnp.zeros((LANES, LANES), BF16)

    @pl.when(pl.program_id(1) % tiles_per_seq == 0)
    def _():
        carry_scr[...] = jnp.zeros_like(carry_scr)

    crows = u_ref.shape[1] // L
    us = [u_ref[:, _residue_block(j, L) * crows:(_residue_block(j, L) + 1) * crows, :].reshape(tc, nb * LANES)
          for j in range(L)]
    ucat = [jnp.concatenate([u[:, q * LANES:(q + 1) * LANES] for u in us], axis=1)
            for q in range(nb)]
    for q in range(nb):
        s = jnp.dot(ucat[q], wb_scr[q], preferred_element_type=F32)
        s_scr[:, q * sw:(q + 1) * sw] = s[:, :sw]
        s_scr[:, half + q * sw:half + (q + 1) * sw] = s[:, sw:]
    ys = [jnp.dot(ucat[q], wt_scr[q], preferred_element_type=F32) for q in range(nb)]
    for q in range(nb):
        re = slice(q * sw, (q + 1) * sw)
        im = slice(half + q * sw, half + (q + 1) * sw)
        ar = pwr_ref[q, 0, L:L + 1, :]
        ai = pwr_ref[q, 1, L:L + 1, :]

        xr, xi = carry_scr[:, re], carry_scr[:, im]
        for c in range(tc):
            xp_scr[c:c + 1, re] = xr
            xp_scr[c:c + 1, im] = xi
            xr, xi = (ar * xr - ai * xi + s_scr[c:c + 1, re], ar * xi + ai * xr + s_scr[c:c + 1, im])
        carry_scr[:, re] = xr
        carry_scr[:, im] = xi

    for q in range(nb):
        lanes = slice(q * LANES, (q + 1) * LANES)
        xq = jnp.concatenate([xp_scr[:, q * sw:(q + 1) * sw], xp_scr[:, half + q * sw:half + (q + 1) * sw]], axis=1)
        y = ys[q] + jnp.dot(xq.astype(BF16), wc_scr[q], preferred_element_type=F32)
        for i in range(L):
            yi = y[:, i * LANES:(i + 1) * LANES] + d_ref[0, :, lanes] * us[i][:, lanes].astype(F32)
            y_scr[q, pl.ds(i, tc, stride=L), :] = jax.nn.gelu(yi)
        y_ref[:, lanes] = y_scr[q].astype(y_ref.dtype)


def _ssm(proj, u_tile0, ssm_w, d_skip, bsz, seq, *, ptiles, nb):
    nt, _, tm, tn = proj.shape
    L = SSM_CHUNK
    per_tile = tn // (nb * LANES)
    crows = tm // L
    nblk = ssm_w[0].shape[0]
    width = nblk * LANES
    assert crows == LANES and nblk % nb == 0 and tn % (nb * LANES) == 0
    tc = ptiles * crows
    rows_per_seq = seq // L
    assert rows_per_seq % tc == 0 and nt % ptiles == 0
    u_spec = pl.BlockSpec((ptiles, None, tm, nb * LANES), lambda blk, i: (i, u_tile0 + blk // per_tile, 0, blk % per_tile))
    w_specs = [pl.BlockSpec((nb,) + w.shape[1:], lambda blk, i: (blk, 0, 0, 0)) for w in ssm_w]
    wide = L * LANES
    states = 2 * GROUPS_PER_LANE_BLOCK * SSM_STATE
    return pl.pallas_call(
        functools.partial(_ssm_kernel, tiles_per_seq=rows_per_seq // tc),
        grid=(nblk // nb, nt // ptiles),
        in_specs=[u_spec] + w_specs + [pl.BlockSpec((1, 1, nb * LANES), lambda blk, i: (blk, 0, 0))],
        out_specs=pl.BlockSpec((tc * L, nb * LANES), lambda blk, i: (i, blk)),
        out_shape=jax.ShapeDtypeStruct((nt * tm, width), BF16),
        scratch_shapes=[
            pltpu.VMEM((nb, wide, wide), BF16),
            pltpu.VMEM((nb, wide, states), BF16),
            pltpu.VMEM((wide, states), BF16),
            pltpu.VMEM((nb, states, wide), BF16),
            pltpu.VMEM((tc, nb * states), F32),
            pltpu.VMEM((tc, nb * states), F32),
            pltpu.VMEM((1, nb * states), F32),
            pltpu.VMEM((nb, tc * L, LANES), F32),
        ],
        compiler_params=_params("parallel", "arbitrary"),
        name="s5_chunked",
    )(proj, *ssm_w, d_skip.astype(F32).reshape(nblk // nb, 1, nb * LANES))


def _merge_kernel(attn_ref, y_ref, ga_ref, gs_ref, wup_ref, wv_ref, wg_ref, *refs):
    riders = len(refs) // 2
    ride_in, (o_ref, *ride_out) = refs[:riders], refs[riders:]
    y = y_ref[...]
    gate = _sigmoid(jnp.dot(y, wg_ref[...].astype(BF16), preferred_element_type=F32))
    gate_s = _sigmoid(gs_ref[...].astype(F32))
    gate_a = _sigmoid(ga_ref[...].astype(F32))
    val = jnp.dot(y, wv_ref[...].astype(BF16), preferred_element_type=F32)
    attn_branch = jnp.dot(attn_ref[...], wup_ref[...].astype(BF16), preferred_element_type=F32)
    merged = gate_a * attn_branch + gate_s * (val * gate)
    o_ref[...] = merged.astype(o_ref.dtype)
    for src, dst in zip(ride_in, ride_out):
        dst[...] = src[...].astype(dst.dtype)


def _merge(attn, y, proj, gate_tile0, w_up, w_v, w_g, ride):
    t = attn.shape[0]
    n = w_up.shape[1]
    _, _, tm, tn = proj.shape
    assert n % tn == 0
    grid = (t // tm, n // tn)
    ride_in, ride_out, ride_shapes = _cast_riders(ride, grid)
    return pl.pallas_call(
        _merge_kernel,
        grid=grid,
        in_specs=[
            pl.BlockSpec((tm, attn.shape[1]), lambda i, j: (i, 0)),
            pl.BlockSpec((tm, y.shape[1]), lambda i, j: (i, 0)),
            pl.BlockSpec((None, None, tm, tn), lambda i, j: (i, gate_tile0 + j, 0, 0)),
            pl.BlockSpec((None, None, tm, tn), lambda i, j: (i, gate_tile0 + n // tn + j, 0, 0)),
            pl.BlockSpec((w_up.shape[0], tn), lambda i, j: (0, j)),
            pl.BlockSpec((w_v.shape[0], tn), lambda i, j: (0, j)),
            pl.BlockSpec((w_g.shape[0], tn), lambda i, j: (0, j)),
        ] + ride_in,
        out_specs=[pl.BlockSpec((tm, tn), lambda i, j: (i, j))] + ride_out,
        out_shape=[jax.ShapeDtypeStruct((t, n), BF16)] + ride_shapes,
        compiler_params=_params("parallel", "arbitrary"),
        name="gated_merge",
    )(attn, y, proj, proj, w_up, w_v, w_g, *(w for w, _ in ride))


def _outproj_kernel(m_ref, w_ref, x_ref, g_ref, o_ref):
    z = jnp.dot(m_ref[...], w_ref[...], preferred_element_type=F32)
    o_ref[...] = x_ref[...] + _rms(z, g_ref[...])


def _outproj(merged, w, x, gain, *, tm):
    t, d = x.shape
    return pl.pallas_call(
        _outproj_kernel,
        grid=(t // tm,),
        in_specs=[
            pl.BlockSpec((tm, merged.shape[1]), lambda i: (i, 0)),
            pl.BlockSpec(w.shape, lambda i: (0, 0)),
            pl.BlockSpec((tm, d), lambda i: (i, 0)),
            pl.BlockSpec((1, d), lambda i: (0, 0)),
        ],
        out_specs=pl.BlockSpec((tm, d), lambda i: (i, 0)),
        out_shape=jax.ShapeDtypeStruct((t, d), F32),
        compiler_params=_params("parallel"),
        name="outproj_norm_residual",
    )(merged, w, x, gain.reshape(1, d))


def _ffn_kernel(x_ref, gpre_ref, gpost_ref, wg_ref, wu_ref, wd_ref, o_ref, h_ref, acc_ref):
    k = pl.program_id(1)
    last = pl.num_programs(1) - 1

    def step(first, final):
        if first:
            h_ref[...] = _rms(x_ref[...], gpre_ref[...]).astype(BF16)
        h = h_ref[...]
        gate = jnp.dot(h, wg_ref[...], preferred_element_type=F32)
        up = jnp.dot(h, wu_ref[...], preferred_element_type=F32)
        f = (gate * _sigmoid(gate) * up).astype(BF16)
        down = jnp.dot(f, wd_ref[...], preferred_element_type=F32)
        if first:
            acc_ref[...] = down
        elif final:
            o_ref[...] = x_ref[...] + _rms(acc_ref[...] + down, gpost_ref[...])
        else:
            acc_ref[...] += down

    pl.when(k == 0)(functools.partial(step, True, False))
    pl.when(jnp.logical_and(k > 0, k < last))(functools.partial(step, False, False))
    pl.when(k == last)(functools.partial(step, False, True))


def _ffn(x, gain_pre, gain_post, w_gate, w_up, w_down, *, tm, tf):
    t, d = x.shape
    dff = w_gate.shape[1]
    return pl.pallas_call(
        _ffn_kernel,
        grid=(t // tm, dff // tf),
        in_specs=[
            pl.BlockSpec((tm, d), lambda i, k: (i, 0)),
            pl.BlockSpec((1, d), lambda i, k: (0, 0)),
            pl.BlockSpec((1, d), lambda i, k: (0, 0)),
            pl.BlockSpec((d, tf), lambda i, k: (0, k)),
            pl.BlockSpec((d, tf), lambda i, k: (0, k)),
            pl.BlockSpec((tf, d), lambda i, k: (k, 0)),
        ],
        out_specs=pl.BlockSpec((tm, d), lambda i, k: (i, 0)),
        out_shape=jax.ShapeDtypeStruct((t, d), F32),
        scratch_shapes=[pltpu.VMEM((tm, d), BF16), pltpu.VMEM((tm, d), F32)],
        compiler_params=_params("parallel", "arbitrary"),
        name="swiglu_ffn",
    )(x, gain_pre.reshape(1, d), gain_post.reshape(1, d), w_gate, w_up, w_down)


def _layer(x, norm_mix_pre, w_in, w_attn_up, ssm_a_re, ssm_a_im, ssm_log_dt, ssm_b_re, ssm_b_im,
           ssm_c_re, ssm_c_im, ssm_d, w_glu_v, w_glu_g, w_out, norm_mix_post, norm_ffn_pre,
           w_ffn_gate, w_ffn_up, w_ffn_down, norm_ffn_post):
    bsz, seq, d = x.shape
    t = bsz * seq
    ssm_width = ssm_d.shape[0]
    u_col0 = 3 * N_HEADS * HEAD_DIM
    assert w_in.shape[1] == u_col0 + ssm_width + 2 * d

    x2 = x.reshape(t, d)
    tn = GROUP_WIDTH
    ffn_tiles = w_ffn_gate.shape[1] // 512
    proj, w_gate, w_up = _inproj(x2, norm_mix_pre, w_in.astype(BF16), ssm_width,
                                 [(w_ffn_gate, ffn_tiles), (w_ffn_up, ffn_tiles)])
    attn = _attention(proj, bsz, seq)
    ssm_w = _ssm_weights(ssm_a_re, ssm_a_im, ssm_log_dt, ssm_b_re, ssm_b_im, ssm_c_re, ssm_c_im)
    y = _ssm(proj, u_col0 // tn, ssm_w, ssm_d, bsz, seq, ptiles=4, nb=2)
    merged, w_down, w_outp = _merge(attn, y, proj, (u_col0 + ssm_width) // tn, w_attn_up, w_glu_v, w_glu_g,
                                    [(w_ffn_down, d // tn), (w_out, d // tn)])
    x1 = _outproj(merged, w_outp, x2, norm_mix_post, tm=512)
    out = _ffn(x1, norm_ffn_pre, norm_ffn_post, w_gate, w_up, w_down, tm=512, tf=512)
    return out.reshape(bsz, seq, d)


def kernel(x, norm_mix_pre, w_in, w_attn_up, ssm_a_re, ssm_a_im, ssm_log_dt, ssm_b_re, ssm_b_im, ssm_c_re, ssm_c_im, ssm_d, w_glu_v, w_glu_g, w_out, norm_mix_post, norm_ffn_pre, w_ffn_gate, w_ffn_up, w_ffn_down, norm_ffn_post):
    stacked = (norm_mix_pre, w_in, w_attn_up, ssm_a_re, ssm_a_im, ssm_log_dt, ssm_b_re, ssm_b_im, ssm_c_re,
               ssm_c_im, ssm_d, w_glu_v, w_glu_g, w_out, norm_mix_post, norm_ffn_pre, w_ffn_gate, w_ffn_up,
               w_ffn_down, norm_ffn_post)
    for layer in range(norm_mix_pre.shape[0]):
        x = _layer(x, *(p[layer] for p in stacked))
    return x
```

```python
import functools

import jax
import jax.numpy as jnp
import numpy as np
from jax import lax
from jax.experimental import pallas as pl
from jax.experimental.pallas import tpu as pltpu

F32 = jnp.float32
BF16 = jnp.bfloat16

EPS = 1e-6
HEAD_DIM = 128
HEADS_PER_GROUP = 4
ATTN_GROUPS = ((128, 1), (512, 4), (2048, 16))
N_GROUPS = len(ATTN_GROUPS)
N_HEADS = HEADS_PER_GROUP * N_GROUPS
GROUP_WIDTH = HEADS_PER_GROUP * HEAD_DIM
ATTN_BLK = 128
ATTN_TILE = 2048
ATTN_UNITS_PER_STEP = 2
SSM_GROUP = 16
SSM_STATE = 64
SSM_CHUNK = 8
LANES = 128
GROUPS_PER_LANE_BLOCK = LANES // SSM_GROUP
PROJ_TILE = 1024
ROW_ORDERS = tuple(dil for _, dil in ATTN_GROUPS) + (SSM_CHUNK,)
PERM_BASE = 4
NEG = -1e30
LOG2E = 1.4426950408889634
VMEM_LIMIT = 56 * 1024 * 1024


def _params(*sem):
    return pltpu.CompilerParams(dimension_semantics=sem, vmem_limit_bytes=VMEM_LIMIT)


def _sigmoid(x):
    return 0.5 * jnp.tanh(0.5 * x) + 0.5


def _rms(x, gain):
    return x * lax.rsqrt(jnp.mean(x * x, axis=-1, keepdims=True) + EPS) * gain


def _cast_riders(weights, grid):
    ni, nj = grid
    in_specs, out_specs, out_shapes = [], [], []
    for w, ncols in weights:
        k, n = w.shape
        assert k % (ni * 16) == 0 and n % (ncols * LANES) == 0 and ncols <= nj
        spec = pl.BlockSpec((k // ni, n // ncols), functools.partial(
            lambda i, j, last: (i, jnp.minimum(j, last)), last=ncols - 1))
        in_specs.append(spec)
        out_specs.append(spec)
        out_shapes.append(jax.ShapeDtypeStruct(w.shape, BF16))
    return in_specs, out_specs, out_shapes


def _residue_block(r, dil):
    if dil <= PERM_BASE:
        return r
    return (r % PERM_BASE) * (dil // PERM_BASE) + r // PERM_BASE


def _proj_tile(col):
    n_qkv = 3 * N_GROUPS
    return jnp.where(col < n_qkv, (col % N_GROUPS) * 3 + col // N_GROUPS, col)


def _inproj_kernel(x_ref, g_ref, w_ref, *refs, n_qkv, u_tiles, riders):
    ride_in, (o_ref, *ride_out) = refs[:riders], refs[riders:2 * riders + 1]
    h_ref, hn_ref, hb_ref, inv_ref = refs[2 * riders + 1:]
    j = pl.program_id(1)
    slabs, tm, _ = hn_ref.shape

    def project(order):
        o_ref[...] = jnp.dot(h_ref[order], w_ref[...], preferred_element_type=F32).astype(o_ref.dtype)
        for src, dst in zip(ride_in, ride_out):
            dst[...] = src[...].astype(dst.dtype)

    @pl.when(j == 0)
    def _():
        x = x_ref[...]
        inv_ref[...] = jnp.broadcast_to(lax.rsqrt(jnp.mean(x * x, axis=-1, keepdims=True) + EPS), inv_ref.shape)
        q = tm // PERM_BASE
        piece = 256
        for c in range(x_ref.shape[1] // LANES):
            cols = slice(c * LANES, (c + 1) * LANES)
            s = c % slabs
            for r0 in range(0, tm, piece):
                rows = slice(r0, r0 + piece)
                hn = x_ref[rows, cols] * inv_ref[rows, :] * g_ref[:, cols]
                h_ref[0, rows, cols] = hn.astype(BF16)
                hn_ref[s, rows, :] = hn
            for b in range(PERM_BASE):
                part = hn_ref[s, pl.ds(b, q, stride=PERM_BASE), :]
                hb_ref[s, b * q:(b + 1) * q, :] = part
                for v, dil in enumerate(ROW_ORDERS):
                    if dil == PERM_BASE:
                        h_ref[v, b * q:(b + 1) * q, cols] = part.astype(BF16)
            for v, dil in enumerate(ROW_ORDERS):
                if dil > PERM_BASE:
                    k, n = dil // PERM_BASE, tm // dil
                    for b in range(PERM_BASE):
                        for a in range(k):
                            blk = b * k + a
                            h_ref[v, blk * n:(blk + 1) * n, cols] = (
                                hb_ref[s, pl.ds(b * q + a, n, stride=k), :].astype(BF16))
        project(0)

    order = jnp.where(j < n_qkv, j % N_GROUPS, jnp.where(j < n_qkv + u_tiles, N_GROUPS, 0))
    pl.when(j > 0)(functools.partial(project, order))


def _inproj(x, gain, w, ssm_width, ride):
    t, d = x.shape
    tm = PROJ_TILE
    tn = GROUP_WIDTH
    ncol = w.shape[1] // tn
    assert t % tm == 0 and w.shape[1] % tn == 0 and ssm_width % tn == 0 and d % LANES == 0
    assert ROW_ORDERS[0] == 1 and all(tm % (dil * 16) == 0 for dil in ROW_ORDERS)
    assert all(dil in (1, PERM_BASE) or (dil % PERM_BASE == 0 and dil // PERM_BASE <= PERM_BASE) for dil in ROW_ORDERS)
    slabs = 8
    grid = (t // tm, ncol)
    ride_in, ride_out, ride_shapes = _cast_riders(ride, grid)
    return pl.pallas_call(
        functools.partial(_inproj_kernel, n_qkv=3 * N_GROUPS, u_tiles=ssm_width // tn, riders=len(ride)),
        grid=grid,
        in_specs=[
            pl.BlockSpec((tm, d), lambda i, j: (i, 0)),
            pl.BlockSpec((1, d), lambda i, j: (0, 0)),
            pl.BlockSpec((d, tn), lambda i, j: (0, j)),
        ] + ride_in,
        out_specs=[pl.BlockSpec((None, None, tm, tn), lambda i, j: (i, _proj_tile(j), 0, 0))] + ride_out,
        out_shape=[jax.ShapeDtypeStruct((t // tm, ncol, tm, tn), BF16)] + ride_shapes,
        scratch_shapes=[
            pltpu.VMEM((len(ROW_ORDERS), tm, d), BF16),
            pltpu.VMEM((slabs, tm, LANES), F32),
            pltpu.VMEM((slabs, tm, LANES), F32),
            pltpu.VMEM((tm, LANES), F32),
        ],
        compiler_params=_params("parallel", "arbitrary"),
        name="inproj",
    )(x, gain.reshape(1, d), w, *(w for w, _ in ride))


def _attn_bias_table():
    qi = np.arange(ATTN_BLK)[:, None]
    kj = np.arange(ATTN_BLK)[None, :]
    table = np.full((N_GROUPS, 2, HEADS_PER_GROUP * ATTN_BLK, 2 * ATTN_BLK), NEG, np.float32)
    for g, (_, dil) in enumerate(ATTN_GROUPS):
        for h in range(HEADS_PER_GROUP):
            slope = 2.0 ** (-8.0 * (g * HEADS_PER_GROUP + h + 1) / N_HEADS) * dil * LOG2E
            rows = slice(h * ATTN_BLK, (h + 1) * ATTN_BLK)
            cur = np.where(kj <= qi, -slope * (qi - kj), NEG)
            prev = np.where(kj >= qi, -slope * (ATTN_BLK + qi - kj), NEG)
            table[g, :, rows, ATTN_BLK:] = cur
            table[g, 0, rows, :ATTN_BLK] = prev
    return table


def _attn_unit(u, tile, bias_ref, qkv_refs, out_scr, lse_scr, hist_refs):
    nt = (((1,), (1,)), ((), ()))

    blocks, logits, stats = [], [], []
    for g, (_, dil) in enumerate(ATTN_GROUPS):
        nb = u // dil
        r = u % dil
        first = jnp.logical_and(tile == 0, nb == 0).astype(jnp.int32)
        start = nb * (ATTN_BLK * dil) + r
        rows = pl.ds(start, ATTN_BLK) if dil == 1 else pl.ds(start, ATTN_BLK, stride=dil)
        qkv = qkv_refs[g]
        q, k, v = (qkv[..., part, :, :].reshape(ATTN_BLK, GROUP_WIDTH) for part in range(3))
        prev = hist_refs[g][r]
        scores = []
        for h in range(HEADS_PER_GROUP):
            cs = slice(h * HEAD_DIM, (h + 1) * HEAD_DIM)
            keys = jnp.concatenate([prev[:, cs], k[:, cs]], axis=0)
            scores.append(lax.dot_general(q[:, cs], keys, nt, preferred_element_type=F32))
        logits.append(jnp.concatenate(scores, axis=0) * (HEAD_DIM ** -0.5 * LOG2E) + bias_ref[g, first])
        blocks.append((r, rows, k, v, prev))
    for s in logits:
        m = jnp.max(s, axis=1, keepdims=True)
        p = jnp.exp2(s - m)
        l = jnp.sum(p, axis=1, keepdims=True)
        stats.append((p.astype(BF16), 1.0 / l, m + jnp.log2(l)))
    for g, ((r, rows, k, v, prev), (p, inv, lse)) in enumerate(zip(blocks, stats)):
        for h in range(HEADS_PER_GROUP):
            cs = slice(h * HEAD_DIM, (h + 1) * HEAD_DIM)
            hr = slice(h * ATTN_BLK, (h + 1) * ATTN_BLK)
            vals = jnp.concatenate([prev[:, GROUP_WIDTH:][:, cs], v[:, cs]], axis=0)
            o = jnp.dot(p[hr], vals, preferred_element_type=F32)
            out_scr[g, h, rows, :] = o * inv[hr]
            lse_scr[g, h, rows, :] = jnp.broadcast_to(lse[hr], (ATTN_BLK, HEAD_DIM))
        hist_refs[g][r, :, :GROUP_WIDTH] = k
        hist_refs[g][r, :, GROUP_WIDTH:] = v


def _attn_kernel(bias_ref, *refs, per_step):
    n_in = per_step * N_GROUPS
    o_ref, out_scr, lse_scr = refs[n_in:n_in + 3]
    hist_refs = refs[n_in + 3:]
    tile = pl.program_id(1)
    step = pl.program_id(2)

    @pl.when(jnp.logical_and(tile == 0, step == 0))
    def _():
        for hist in hist_refs:
            hist[...] = jnp.zeros_like(hist)

    for e in range(per_step):
        _attn_unit(step * per_step + e, tile, bias_ref, refs[e * N_GROUPS:(e + 1) * N_GROUPS],
                   out_scr, lse_scr, hist_refs)

    @pl.when(step == pl.num_programs(2) - 1)
    def _():
        for h in range(HEADS_PER_GROUP):
            lses = [lse_scr[g, h] for g in range(N_GROUPS)]
            top = functools.reduce(jnp.maximum, lses)
            ws = [jnp.exp2(x - top) for x in lses]
            num = sum(w * out_scr[g, h] for g, w in enumerate(ws))
            o_ref[:, h * HEAD_DIM:(h + 1) * HEAD_DIM] = (num / sum(ws)).astype(o_ref.dtype)


def _attention(proj, bsz, seq):
    assert seq % ATTN_TILE == 0 and ATTN_TILE % PROJ_TILE == 0
    tiles = seq // ATTN_TILE
    units = ATTN_TILE // ATTN_BLK
    bias = jnp.asarray(_attn_bias_table())
    per_step = ATTN_UNITS_PER_STEP
    assert units % per_step == 0
    group_specs, hist = [], []
    for g, (window, dil) in enumerate(ATTN_GROUPS):
        assert window // dil == ATTN_BLK and units % dil == 0
        span = ATTN_BLK * dil
        spans_per_tile = ATTN_TILE // span
        if span <= PROJ_TILE:
            block = (None, 3, ATTN_BLK, GROUP_WIDTH)

            def index(b, tile, u, *, g=g, dil=dil, spt=spans_per_tile, sppt=PROJ_TILE // span,
                      per_seq=seq // PROJ_TILE):
                sp = tile * spt + u // dil
                return b * per_seq + sp // sppt, g, _residue_block(u % dil, dil) * sppt + sp % sppt, 0
        else:
            block = (span // PROJ_TILE, 3, PROJ_TILE // dil, GROUP_WIDTH)

            def index(b, tile, u, *, g=g, dil=dil, spt=spans_per_tile, per_seq=seq // span):
                return b * per_seq + tile * spt + u // dil, g, _residue_block(u % dil, dil), 0

        group_specs.append((block, index))
        hist.append(pltpu.VMEM((dil, ATTN_BLK, 2 * GROUP_WIDTH), BF16))
    in_specs = [pl.BlockSpec(bias.shape, lambda b, tile, s: (0, 0, 0, 0))]
    for e in range(per_step):
        for block, index in group_specs:
            in_specs.append(pl.BlockSpec(block, functools.partial(
                lambda b, tile, s, e, index: index(b, tile, s * per_step + e), e=e, index=index)))
    scratch = pltpu.VMEM((N_GROUPS, HEADS_PER_GROUP, ATTN_TILE, HEAD_DIM), F32)
    return pl.pallas_call(
        functools.partial(_attn_kernel, per_step=per_step),
        grid=(bsz, tiles, units // per_step),
        in_specs=in_specs,
        out_specs=pl.BlockSpec((ATTN_TILE, GROUP_WIDTH), lambda b, tile, u: (b * tiles + tile, 0)),
        out_shape=jax.ShapeDtypeStruct((bsz * seq, GROUP_WIDTH), BF16),
        scratch_shapes=[scratch, scratch] + hist,
        compiler_params=_params("arbitrary", "arbitrary", "arbitrary"),
        name="dilated_attention",
    )(bias, *([proj] * (N_GROUPS * per_step)))


def _cmul(a, b):
    return a[0] * b[0] - a[1] * b[1], a[0] * b[1] + a[1] * b[0]


def _ssm_weights(a_re, a_im, log_dt, b_re, b_im, c_re, c_im):
    n_groups = a_re.shape[0]
    nblk = n_groups // GROUPS_PER_LANE_BLOCK
    gl = GROUPS_PER_LANE_BLOCK
    L = SSM_CHUNK
    a_re, a_im = a_re.astype(F32), a_im.astype(F32)
    dt = jnp.exp(log_dt.astype(F32))[:, None]
    steps = jnp.arange(L + 1, dtype=F32)[None, :, None]
    mag = jnp.exp((a_re * dt)[:, None, :] * steps)
    ang = (a_im * dt)[:, None, :] * steps
    powers = (mag * jnp.cos(ang), mag * jnp.sin(ang))
    lam_bar = (powers[0][:, 1], powers[1][:, 1])
    den = a_re * a_re + a_im * a_im
    num = (lam_bar[0] - 1.0, lam_bar[1])
    ratio = ((num[0] * a_re + num[1] * a_im) / den, (num[1] * a_re - num[0] * a_im) / den)
    b_bar = _cmul((ratio[0][..., None], ratio[1][..., None]), (b_re.astype(F32), b_im.astype(F32)))
    c_t = (c_re.astype(F32).transpose(0, 2, 1), c_im.astype(F32).transpose(0, 2, 1))
    eye = jnp.eye(gl, dtype=F32)

    def block_diag(m):
        rows, cols = m.shape[1:]
        m = m.reshape(nblk, gl, rows, 1, cols) * eye[None, :, None, :, None]
        return m.reshape(nblk, gl * rows, gl * cols)

    b_in = jnp.stack([block_diag(b.transpose(0, 2, 1)) for b in b_bar], axis=1)
    c_out = jnp.stack([block_diag(c) for c in c_t], axis=1)
    pw = jnp.stack(powers, axis=0).reshape(2, nblk, gl, L + 1, SSM_STATE)
    pw_row = pw.transpose(1, 0, 3, 2, 4).reshape(nblk, 2, L + 1, gl * SSM_STATE)
    pw_col = pw_row.transpose(0, 1, 3, 2)
    return b_in, c_out, pw_row, pw_col


def _ssm_kernel(*refs, tiles_per_seq):
    u_ref = refs[0]
    (bin_ref, cout_ref, pwr_ref, pwc_ref, d_ref, y_ref,
     wt_scr, wb_scr, wl_scr, wc_scr, s_scr, xp_scr, carry_scr, y_scr) = refs[1:]
    L = SSM_CHUNK
    nb = wt_scr.shape[0]
    tc = s_scr.shape[0]
    half = s_scr.shape[1] // 2
    sw = half // nb

    @pl.when(pl.program_id(1) == 0)
    def _():
        for q in range(nb):
            c_hi = [cout_ref[q, ri].astype(BF16) for ri in range(2)]
            c_lo = [(cout_ref[q, ri] - c_hi[ri].astype(F32)).astype(BF16) for ri in range(2)]
            for j in range(L):
                rows = slice(j * LANES, (j + 1) * LANES)
                n = L - 1 - j
                pr = pwr_ref[q, 0, n:n + 1, :]
                pi = pwr_ref[q, 1, n:n + 1, :]
                for ri, val in enumerate((bin_ref[q, 0] * pr - bin_ref[q, 1] * pi,
                                          bin_ref[q, 0] * pi + bin_ref[q, 1] * pr)):
                    cols = slice(ri * sw, (ri + 1) * sw)
                    top = val.astype(BF16)
                    wb_scr[q, rows, cols] = top
                    wl_scr[rows, cols] = (val - top.astype(F32)).astype(BF16)
                pr = pwc_ref[q, 0, :, j + 1:j + 2]
                pi = pwc_ref[q, 1, :, j + 1:j + 2]
                wc_scr[q, :sw, rows] = (cout_ref[q, 0] * pr - cout_ref[q, 1] * pi).astype(BF16)
                wc_scr[q, sw:, rows] = (-(cout_ref[q, 0] * pi + cout_ref[q, 1] * pr)).astype(BF16)
            prods = []
            for ri in range(2):
                cols = slice(ri * sw, (ri + 1) * sw)
                top, low = wb_scr[q, :, cols], wl_scr[:, cols]
                prods.append(jnp.dot(top, c_hi[ri], preferred_element_type=F32)
                             + jnp.dot(top, c_lo[ri], preferred_element_type=F32)
                             + jnp.dot(low, c_hi[ri], preferred_element_type=F32))
            k_all = (prods[0] - prods[1]).astype(BF16)
            for j in range(L):
                n = L - 1 - j
                kn = k_all[j * LANES:(j + 1) * LANES]
                for jj in range(L):
                    ii = jj + n
                    if ii < L:
                        wt_scr[q, jj * LANES:(jj + 1) * LANES, ii * LANES:(ii + 1) * LANES] = kn
                    if j < jj:
                        wt_scr[q, jj * LANES:(jj + 1) * LANES, j * LANES:(j + 1) * LANES] = jnp.zeros((LANES, LANES), BF16)

    @pl.when(pl.program_id(1) % tiles_per_seq == 0)
    def _():
        carry_scr[...] = jnp.zeros_like(carry_scr)

    crows = u_ref.shape[1] // L
    us = [u_ref[:, _residue_block(j, L) * crows:(_residue_block(j, L) + 1) * crows, :].reshape(tc, nb * LANES)
          for j in range(L)]
    ucat = [jnp.concatenate([u[:, q * LANES:(q + 1) * LANES] for u in us], axis=1)
            for q in range(nb)]
    for q in range(nb):
        s = jnp.dot(ucat[q], wb_scr[q], preferred_element_type=F32)
        s_scr[:, q * sw:(q + 1) * sw] = s[:, :sw]
        s_scr[:, half + q * sw:half + (q + 1) * sw] = s[:, sw:]
    ys = [jnp.dot(ucat[q], wt_scr[q], preferred_element_type=F32) for q in range(nb)]
    for q in range(nb):
        re = slice(q * sw, (q + 1) * sw)
        im = slice(half + q * sw, half + (q + 1) * sw)
        ar = pwr_ref[q, 0, L:L + 1, :]
        ai = pwr_ref[q, 1, L:L + 1, :]

        xr, xi = carry_scr[:, re], carry_scr[:, im]
        for c in range(tc):
            xp_scr[c:c + 1, re] = xr
            xp_scr[c:c + 1, im] = xi
            xr, xi = (ar * xr - ai * xi + s_scr[c:c + 1, re], ar * xi + ai * xr + s_scr[c:c + 1, im])
        carry_scr[:, re] = xr
        carry_scr[:, im] = xi

    for q in range(nb):
        lanes = slice(q * LANES, (q + 1) * LANES)
        xq = jnp.concatenate([xp_scr[:, q * sw:(q + 1) * sw], xp_scr[:, half + q * sw:half + (q + 1) * sw]], axis=1)
        y = ys[q] + jnp.dot(xq.astype(BF16), wc_scr[q], preferred_element_type=F32)
        for i in range(L):
            yi = y[:, i * LANES:(i + 1) * LANES] + d_ref[0, :, lanes] * us[i][:, lanes].astype(F32)
            y_scr[q, pl.ds(i, tc, stride=L), :] = jax.nn.gelu(yi)
        y_ref[:, lanes] = y_scr[q].astype(y_ref.dtype)


def _ssm(proj, u_tile0, ssm_w, d_skip, bsz, seq, *, ptiles, nb):
    nt, _, tm, tn = proj.shape
    L = SSM_CHUNK
    per_tile = tn // (nb * LANES)
    crows = tm // L
    nblk = ssm_w[0].shape[0]
    width = nblk * LANES
    assert crows == LANES and nblk % nb == 0 and tn % (nb * LANES) == 0
    tc = ptiles * crows
    rows_per_seq = seq // L
    assert rows_per_seq % tc == 0 and nt % ptiles == 0
    u_spec = pl.BlockSpec((ptiles, None, tm, nb * LANES), lambda blk, i: (i, u_tile0 + blk // per_tile, 0, blk % per_tile))
    w_specs = [pl.BlockSpec((nb,) + w.shape[1:], lambda blk, i: (blk, 0, 0, 0)) for w in ssm_w]
    wide = L * LANES
    states = 2 * GROUPS_PER_LANE_BLOCK * SSM_STATE
    return pl.pallas_call(
        functools.partial(_ssm_kernel, tiles_per_seq=rows_per_seq // tc),
        grid=(nblk // nb, nt // ptiles),
        in_specs=[u_spec] + w_specs + [pl.BlockSpec((1, 1, nb * LANES), lambda blk, i: (blk, 0, 0))],
        out_specs=pl.BlockSpec((tc * L, nb * LANES), lambda blk, i: (i, blk)),
        out_shape=jax.ShapeDtypeStruct((nt * tm, width), BF16),
        scratch_shapes=[
            pltpu.VMEM((nb, wide, wide), BF16),
            pltpu.VMEM((nb, wide, states), BF16),
            pltpu.VMEM((wide, states), BF16),
            pltpu.VMEM((nb, states, wide), BF16),
            pltpu.VMEM((tc, nb * states), F32),
            pltpu.VMEM((tc, nb * states), F32),
            pltpu.VMEM((1, nb * states), F32),
            pltpu.VMEM((nb, tc * L, LANES), F32),
        ],
        compiler_params=_params("parallel", "arbitrary"),
        name="s5_chunked",
    )(proj, *ssm_w, d_skip.astype(F32).reshape(nblk // nb, 1, nb * LANES))


def _merge_kernel(attn_ref, y_ref, ga_ref, gs_ref, wup_ref, wv_ref, wg_ref, *refs):
    riders = len(refs) // 2
    ride_in, (o_ref, *ride_out) = refs[:riders], refs[riders:]
    y = y_ref[...]
    gate = _sigmoid(jnp.dot(y, wg_ref[...].astype(BF16), preferred_element_type=F32))
    gate_s = _sigmoid(gs_ref[...].astype(F32))
    gate_a = _sigmoid(ga_ref[...].astype(F32))
    val = jnp.dot(y, wv_ref[...].astype(BF16), preferred_element_type=F32)
    attn_branch = jnp.dot(attn_ref[...], wup_ref[...].astype(BF16), preferred_element_type=F32)
    merged = gate_a * attn_branch + gate_s * (val * gate)
    o_ref[...] = merged.astype(o_ref.dtype)
    for src, dst in zip(ride_in, ride_out):
        dst[...] = src[...].astype(dst.dtype)


def _merge(attn, y, proj, gate_tile0, w_up, w_v, w_g, ride):
    t = attn.shape[0]
    n = w_up.shape[1]
    _, _, tm, tn = proj.shape
    assert n % tn == 0
    grid = (t // tm, n // tn)
    ride_in, ride_out, ride_shapes = _cast_riders(ride, grid)
    return pl.pallas_call(
        _merge_kernel,
        grid=grid,
        in_specs=[
            pl.BlockSpec((tm, attn.shape[1]), lambda i, j: (i, 0)),
            pl.BlockSpec((tm, y.shape[1]), lambda i, j: (i, 0)),
            pl.BlockSpec((None, None, tm, tn), lambda i, j: (i, gate_tile0 + j, 0, 0)),
            pl.BlockSpec((None, None, tm, tn), lambda i, j: (i, gate_tile0 + n // tn + j, 0, 0)),
            pl.BlockSpec((w_up.shape[0], tn), lambda i, j: (0, j)),
            pl.BlockSpec((w_v.shape[0], tn), lambda i, j: (0, j)),
            pl.BlockSpec((w_g.shape[0], tn), lambda i, j: (0, j)),
        ] + ride_in,
        out_specs=[pl.BlockSpec((tm, tn), lambda i, j: (i, j))] + ride_out,
        out_shape=[jax.ShapeDtypeStruct((t, n), BF16)] + ride_shapes,
        compiler_params=_params("parallel", "arbitrary"),
        name="gated_merge",
    )(attn, y, proj, proj, w_up, w_v, w_g, *(w for w, _ in ride))


def _outproj_kernel(m_ref, w_ref, x_ref, g_ref, o_ref):
    z = jnp.dot(m_ref[...], w_ref[...], preferred_element_type=F32)
    o_ref[...] = x_ref[...] + _rms(z, g_ref[...])


def _outproj(merged, w, x, gain, *, tm):
    t, d = x.shape
    return pl.pallas_call(
        _outproj_kernel,
        grid=(t // tm,),
        in_specs=[
            pl.BlockSpec((tm, merged.shape[1]), lambda i: (i, 0)),
            pl.BlockSpec(w.shape, lambda i: (0, 0)),
            pl.BlockSpec((tm, d), lambda i: (i, 0)),
            pl.BlockSpec((1, d), lambda i: (0, 0)),
        ],
        out_specs=pl.BlockSpec((tm, d), lambda i: (i, 0)),
        out_shape=jax.ShapeDtypeStruct((t, d), F32),
        compiler_params=_params("parallel"),
        name="outproj_norm_residual",
    )(merged, w, x, gain.reshape(1, d))


def _ffn_kernel(x_ref, gpre_ref, gpost_ref, wg_ref, wu_ref, wd_ref, o_ref, h_ref, acc_ref):
    k = pl.program_id(1)
    last = pl.num_programs(1) - 1

    def step(first, final):
        if first:
            h_ref[...] = _rms(x_ref[...], gpre_ref[...]).astype(BF16)
        h = h_ref[...]
        gate = jnp.dot(h, wg_ref[...], preferred_element_type=F32)
        up = jnp.dot(h, wu_ref[...], preferred_element_type=F32)
        f = (gate * _sigmoid(gate) * up).astype(BF16)
        down = jnp.dot(f, wd_ref[...], preferred_element_type=F32)
        if first:
            acc_ref[...] = down
        elif final:
            o_ref[...] = x_ref[...] + _rms(acc_ref[...] + down, gpost_ref[...])
        else:
            acc_ref[...] += down

    pl.when(k == 0)(functools.partial(step, True, False))
    pl.when(jnp.logical_and(k > 0, k < last))(functools.partial(step, False, False))
    pl.when(k == last)(functools.partial(step, False, True))


def _ffn(x, gain_pre, gain_post, w_gate, w_up, w_down, *, tm, tf):
    t, d = x.shape
    dff = w_gate.shape[1]
    return pl.pallas_call(
        _ffn_kernel,
        grid=(t // tm, dff // tf),
        in_specs=[
            pl.BlockSpec((tm, d), lambda i, k: (i, 0)),
            pl.BlockSpec((1, d), lambda i, k: (0, 0)),
            pl.BlockSpec((1, d), lambda i, k: (0, 0)),
            pl.BlockSpec((d, tf), lambda i, k: (0, k)),
            pl.BlockSpec((d, tf), lambda i, k: (0, k)),
            pl.BlockSpec((tf, d), lambda i, k: (k, 0)),
        ],
        out_specs=pl.BlockSpec((tm, d), lambda i, k: (i, 0)),
        out_shape=jax.ShapeDtypeStruct((t, d), F32),
        scratch_shapes=[pltpu.VMEM((tm, d), BF16), pltpu.VMEM((tm, d), F32)],
        compiler_params=_params("parallel", "arbitrary"),
        name="swiglu_ffn",
    )(x, gain_pre.reshape(1, d), gain_post.reshape(1, d), w_gate, w_up, w_down)


def _layer(x, norm_mix_pre, w_in, w_attn_up, ssm_a_re, ssm_a_im, ssm_log_dt, ssm_b_re, ssm_b_im,
           ssm_c_re, ssm_c_im, ssm_d, w_glu_v, w_glu_g, w_out, norm_mix_post, norm_ffn_pre,
           w_ffn_gate, w_ffn_up, w_ffn_down, norm_ffn_post):
    bsz, seq, d = x.shape
    t = bsz * seq
    ssm_width = ssm_d.shape[0]
    u_col0 = 3 * N_HEADS * HEAD_DIM
    assert w_in.shape[1] == u_col0 + ssm_width + 2 * d

    x2 = x.reshape(t, d)
    tn = GROUP_WIDTH
    ffn_tiles = w_ffn_gate.shape[1] // 512
    proj, w_gate, w_up = _inproj(x2, norm_mix_pre, w_in.astype(BF16), ssm_width,
                                 [(w_ffn_gate, ffn_tiles), (w_ffn_up, ffn_tiles)])
    attn = _attention(proj, bsz, seq)
    ssm_w = _ssm_weights(ssm_a_re, ssm_a_im, ssm_log_dt, ssm_b_re, ssm_b_im, ssm_c_re, ssm_c_im)
    y = _ssm(proj, u_col0 // tn, ssm_w, ssm_d, bsz, seq, ptiles=4, nb=2)
    merged, w_down, w_outp = _merge(attn, y, proj, (u_col0 + ssm_width) // tn, w_attn_up, w_glu_v, w_glu_g,
                                    [(w_ffn_down, d // tn), (w_out, d // tn)])
    x1 = _outproj(merged, w_outp, x2, norm_mix_post, tm=512)
    out = _ffn(x1, norm_ffn_pre, norm_ffn_post, w_gate, w_up, w_down, tm=512, tf=512)
    return out.reshape(bsz, seq, d)


def kernel(x, norm_mix_pre, w_in, w_attn_up, ssm_a_re, ssm_a_im, ssm_log_dt, ssm_b_re, ssm_b_im, ssm_c_re, ssm_c_im, ssm_d, w_glu_v, w_glu_g, w_out, norm_mix_post, norm_ffn_pre, w_ffn_gate, w_ffn_up, w_ffn_down, norm_ffn_post):
    stacked = (norm_mix_pre, w_in, w_attn_up, ssm_a_re, ssm_a_im, ssm_log_dt, ssm_b_re, ssm_b_im, ssm_c_re,
               ssm_c_im, ssm_d, w_glu_v, w_glu_g, w_out, norm_mix_post, norm_ffn_pre, w_ffn_gate, w_ffn_up,
               w_ffn_down, norm_ffn_post)
    for layer in range(norm_mix_pre.shape[0]):
        x = _layer(x, *(p[layer] for p in stacked))
    return x
```
